```python
import math, functools
import jax, jax.numpy as jnp
from jax import lax
import numpy as np

D_MODEL = 2048
BATCH = 2
SEQ = 4096
DEPTH = 4
DEC_BATCH = 8
DEC_SEQ = 1
PAST_LEN = 16384
PAGE_SIZE = 128

HA = 4
DKA = 128
DVA = 256
HB = 4
DKB = 128
DVB = 256
CHUNK = 128
ROPE_BASE = 10000.0
HC = 8
DHC = 128
IDX_HEADS = 16
IDX_DIM = 64
TOPK_MAX = 256
QBLK = 128
N_BUCKETS = 32
MAX_DISTANCE = 128
D_CONV = 1024
CONV_W = 31
D_FF = 5632
EPS = 1e-6

N_EVEN_LAYERS = (DEPTH + 1) // 2
N_ODD_LAYERS = DEPTH // 2
EVEN_SPLIT_SIZES = (HA * DKA, HA * DKA, HA * DVA, HA * DVA, HA, HA, HB * DKB, HB * DKB, HB * DVB, HB * DVB)
ODD_SPLIT_SIZES = (HC * DHC, HC * DHC, HC * DHC, IDX_HEADS * IDX_DIM, IDX_DIM, IDX_HEADS, D_CONV, D_CONV)

kernel_name = 'hybrid_mlstm_retention_dsa_conformer_step'


def _split(x, sizes):
    out, o = [], 0
    for s in sizes:
        out.append(x[..., o:o + s])
        o += s
    return out


def rmsnorm(x, g):
    xf = x.astype(jnp.float32)
    y = xf * lax.rsqrt(jnp.mean(xf * xf, axis=-1, keepdims=True) + EPS)
    return (y * g.astype(jnp.float32)).astype(x.dtype)


def swiglu(h, w_in, w_out):
    a, b = jnp.split(h @ w_in, 2, axis=-1)
    return (jax.nn.silu(a) * b) @ w_out


def rope(x, pos):
    half = x.shape[-1] // 2
    freqs = ROPE_BASE ** (-jnp.arange(half, dtype=jnp.float32) / half)
    ang = pos.astype(jnp.float32)[:, None] * freqs[None, :]
    cos, sin = jnp.cos(ang), jnp.sin(ang)
    x1, x2 = x[..., :half], x[..., half:]
    return jnp.concatenate([x1 * cos - x2 * sin, x1 * sin + x2 * cos], axis=-1)


def _chunk_len(t):
    return CHUNK if t % CHUNK == 0 else t


def _to_chunks(x, L):
    b, h, t = x.shape[:3]
    return jnp.moveaxis(x.reshape((b, h, t // L, L) + x.shape[3:]), 2, 0)


def _from_chunks(y):
    y = jnp.moveaxis(y, 0, 2)
    return y.reshape(y.shape[:2] + (y.shape[2] * y.shape[3],) + y.shape[4:])


def mlstm_chunked(q, k, v, i_pre, log_f, c0, n0, m0):
    L = _chunk_len(q.shape[2])
    causal = jnp.tril(jnp.ones((L, L), dtype=bool))

    def step(carry, xs):
        c_prev, n_prev, m_prev = carry
        qc, kc, vc, ic, fc = xs
        b = jnp.cumsum(fc, axis=-1)
        d_log = jnp.where(causal, b[..., :, None] - b[..., None, :] + ic[..., None, :], -jnp.inf)
        g = b + m_prev[..., None]
        m = jnp.maximum(g, jnp.max(d_log, axis=-1))
        s = jnp.einsum('bhtk,bhsk->bhts', qc, kc) * jnp.exp(d_log - m[..., None])
        inter = jnp.exp(g - m)
        num = jnp.einsum('bhts,bhsv->bhtv', s, vc) + inter[..., None] * jnp.einsum('bhvk,bhtk->bhtv', c_prev, qc)
        den = jnp.sum(s, axis=-1) + inter * jnp.einsum('bhk,bhtk->bht', n_prev, qc)
        h = num / jnp.maximum(jnp.abs(den), jnp.exp(-m))[..., None]
        m_new = m[..., -1]
        decay = jnp.exp(b[..., -1] + m_prev - m_new)
        wk = jnp.exp(b[..., -1:] - b + ic - m_new[..., None])
        c_new = decay[..., None, None] * c_prev + jnp.einsum('bhs,bhsv,bhsk->bhvk', wk, vc, kc)
        n_new = decay[..., None] * n_prev + jnp.einsum('bhs,bhsk->bhk', wk, kc)
        return (c_new, n_new, m_new), h

    xs = tuple(_to_chunks(a, L) for a in (q, k, v, i_pre, log_f))
    (c, n, m), h = lax.scan(step, (c0, n0, m0), xs)
    return _from_chunks(h), c, n, m


def retention_chunked(q, k, v, s0):
    n_heads = q.shape[1]
    L = _chunk_len(q.shape[2])
    log_gamma = jnp.log1p(-jnp.exp2(-5.0 - jnp.arange(n_heads, dtype=jnp.float32)))
    j = jnp.arange(L, dtype=jnp.float32)
    diff = j[:, None] - j[None, :]
    decay_mat = jnp.where(diff >= 0, jnp.exp(log_gamma[:, None, None] * jnp.maximum(diff, 0.0)), 0.0)
    q_decay = jnp.exp(log_gamma[:, None] * (j + 1.0))
    k_decay = jnp.exp(log_gamma[:, None] * (L - 1.0 - j))
    state_decay = jnp.exp(log_gamma * L)

    def step(s_prev, xs):
        qc, kc, vc = xs
        a = jnp.einsum('bhtk,bhsk->bhts', qc, kc) * decay_mat
        o = jnp.einsum('bhts,bhsv->bhtv', a, vc) + q_decay[..., None] * jnp.einsum('bhtk,bhkv->bhtv', qc, s_prev)
        s_new = state_decay[:, None, None] * s_prev + jnp.einsum('hs,bhsk,bhsv->bhkv', k_decay, kc, vc)
        return s_new, o

    s, o = lax.scan(step, s0, tuple(_to_chunks(a, L) for a in (q, k, v)))
    return _from_chunks(o), s


def mixer_ab(h, pos, w_in, b_gate, g_a, g_b, w_out, c0, n0, m0, s0):
    b_sz, t, _ = h.shape
    f32 = jnp.float32
    qa, ka, va, oa, ia, fa, qb, kb, vb, gb = _split(h @ w_in, EVEN_SPLIT_SIZES)
    heads = lambda x, n: x.reshape(b_sz, t, n, -1).transpose(0, 2, 1, 3).astype(f32)
    i_pre = (ia + b_gate[:HA]).astype(f32).transpose(0, 2, 1)
    log_f = jax.nn.log_sigmoid((fa + b_gate[HA:]).astype(f32)).transpose(0, 2, 1)
    ha, c, n, m = mlstm_chunked(heads(qa, HA), heads(ka, HA) * DKA ** -0.5, heads(va, HA), i_pre, log_f, c0, n0, m0)
    ha = ha.transpose(0, 2, 1, 3)
    ha = ha * lax.rsqrt(jnp.mean(ha * ha, axis=-1, keepdims=True) + EPS) * g_a.reshape(HA, DVA)
    ha = ha.reshape(b_sz, t, HA * DVA) * jax.nn.sigmoid(oa.astype(f32))
    hb, s = retention_chunked(rope(heads(qb, HB), pos), rope(heads(kb, HB), pos) * DKB ** -0.5, heads(vb, HB), s0)
    hb = hb.transpose(0, 2, 1, 3)
    mu = jnp.mean(hb, axis=-1, keepdims=True)
    var = jnp.mean(jnp.square(hb - mu), axis=-1, keepdims=True)
    hb = (hb - mu) * lax.rsqrt(var + EPS) * g_b.reshape(HB, DVB)
    hb = hb.reshape(b_sz, t, HB * DVB) * jax.nn.silu(gb.astype(f32))
    y = jnp.concatenate([ha, hb], axis=-1).astype(h.dtype) @ w_out
    return y, (c, n, m, s)


def t5_bucket(dist):
    exact = N_BUCKETS // 2
    dist = jnp.maximum(dist, 0)
    log_ratio = jnp.log(jnp.maximum(dist, 1).astype(jnp.float32) / exact) / math.log(MAX_DISTANCE / exact)
    large = jnp.minimum(exact + (log_ratio * (N_BUCKETS - exact)).astype(jnp.int32), N_BUCKETS - 1)
    return jnp.where(dist < exact, dist, large)


def index_topk(qi, wi, ki, qpos, topk):
    logits = jnp.einsum('bthd,bsd->bths', qi.astype(jnp.float32), ki.astype(jnp.float32))
    score = jnp.einsum('bth,bths->bts', wi.astype(jnp.float32), jax.nn.relu(logits))
    kpos = jnp.arange(ki.shape[1])
    score = jnp.where(kpos[None, None, :] <= qpos[None, :, None], score, -jnp.inf)
    _, idx = lax.top_k(score, topk)
    return idx, idx <= qpos[None, :, None]


def sparse_attend(q, k_sel, v_sel, qpos, idx, valid, rel_bias):
    logits = jnp.einsum('bthd,btkhd->bthk', q.astype(jnp.float32), k_sel.astype(jnp.float32)) * DHC ** -0.5
    bias = jnp.take(rel_bias.astype(jnp.float32), t5_bucket(qpos[None, :, None] - idx), axis=0)
    logits = jnp.where(valid[:, :, None, :], logits + jnp.moveaxis(bias, -1, 2), -jnp.inf)
    p = jax.nn.softmax(logits, axis=-1)
    return jnp.einsum('bthk,btkhd->bthd', p, v_sel.astype(jnp.float32))


_rows = jax.vmap(lambda a, i: a[i])


def attend_prompt(qc, kc, vc, qi, ki, wi, rel_bias):
    b_sz, t = qc.shape[:2]
    qb = QBLK if t % QBLK == 0 else t
    nb = t // qb
    topk = min(TOPK_MAX, t // 4)
    blocks = lambda x: jnp.moveaxis(x.reshape((b_sz, nb, qb) + x.shape[2:]), 1, 0)

    def block(args):
        q_b, qi_b, wi_b, qpos = args
        idx, valid = index_topk(qi_b, wi_b, ki, qpos, topk)
        return sparse_attend(q_b, _rows(kc, idx), _rows(vc, idx), qpos, idx, valid, rel_bias)

    out = lax.map(block, (blocks(qc), blocks(qi), blocks(wi), jnp.arange(t).reshape(nb, qb)))
    return jnp.moveaxis(out, 0, 1).reshape(b_sz, t, HC, DHC)


def attend_sample(qc, kc, vc, qi, ki, wi, rel_bias, pool_k, pool_v, pool_idx, page_table):
    b_sz, t = qc.shape[:2]
    past = page_table.shape[1] * PAGE_SIZE
    ki_past = pool_idx[page_table].reshape(b_sz, past, IDX_DIM)
    ki_all = jnp.concatenate([ki_past.astype(ki.dtype), ki], axis=1)
    qpos = past + jnp.arange(t)
    topk = min(TOPK_MAX, (past + t) // 4)
    idx, valid = index_topk(qi, wi, ki_all, qpos, topk)
    in_past = idx < past
    p_idx = jnp.minimum(idx, past - 1)
    phys = jax.vmap(lambda pt, i: pt[i])(page_table, p_idx // PAGE_SIZE)
    off = p_idx % PAGE_SIZE
    n_idx = jnp.clip(idx - past, 0, t - 1)
    sel = lambda pool, new: jnp.where(in_past[..., None, None], pool[phys, off].astype(new.dtype), _rows(new, n_idx))
    return sparse_attend(qc, sel(pool_k, kc), sel(pool_v, vc), qpos, idx, valid, rel_bias)


def conv_module(ua, ug, prev, conv_w, conv_b, g_n, b_n):
    u = ua * jax.nn.sigmoid(ug)
    ext = jnp.concatenate([prev.astype(u.dtype), u], axis=1)
    y = lax.conv_general_dilated(ext, conv_w[:, None, :].astype(u.dtype), window_strides=(1,), padding='VALID',
                                 dimension_numbers=('NWC', 'WIO', 'NWC'), feature_group_count=D_CONV) + conv_b
    yf = y.astype(jnp.float32)
    mu = jnp.mean(yf, axis=-1, keepdims=True)
    var = jnp.mean(jnp.square(yf - mu), axis=-1, keepdims=True)
    yn = (yf - mu) * lax.rsqrt(var + EPS) * g_n + b_n
    return jax.nn.silu(yn), ext[:, -(CONV_W - 1):]


def mixer_cd(h, w_in, w_out, conv_w, conv_b, g_n, b_n, conv_prev, attend):
    b_sz, t, _ = h.shape
    qc, kc, vc, qi, ki, wi, ua, ug = _split(h @ w_in, ODD_SPLIT_SIZES)
    qc, kc, vc = (a.reshape(b_sz, t, HC, DHC) for a in (qc, kc, vc))
    qi = qi.reshape(b_sz, t, IDX_HEADS, IDX_DIM)
    attn = attend(qc, kc, vc, qi, ki, wi)
    conv_out, conv_state = conv_module(ua, ug, conv_prev, conv_w, conv_b, g_n, b_n)
    y = jnp.concatenate([attn.reshape(b_sz, t, HC * DHC), conv_out], axis=-1).astype(h.dtype) @ w_out
    return y, (kc, vc, ki, conv_state)


def setup_inputs(seed: int = 0) -> dict:
    key = jax.random.key(seed)
    keys = iter(jax.random.split(key, 48))

    def nrm(shape, scale=1.0):
        return scale * jax.random.normal(next(keys), shape, jnp.float32)

    def gain(shape):
        return 1.0 + nrm(shape, 0.02)

    n_pages = PAST_LEN // PAGE_SIZE
    n_pool = (DEC_BATCH * n_pages * 5) // 4
    page_table = jax.random.permutation(next(keys), n_pool)[:DEC_BATCH * n_pages].reshape(DEC_BATCH, n_pages).astype(jnp.int32)
    ne, no = N_EVEN_LAYERS, N_ODD_LAYERS
    even_in, odd_in = sum(EVEN_SPLIT_SIZES), sum(ODD_SPLIT_SIZES)
    mix_ab_w = HA * DVA + HB * DVB
    mix_cd_w = HC * DHC + D_CONV
    b_gate = jnp.concatenate([nrm((ne, HA), 0.1),
                              jnp.linspace(3.0, 6.0, HA, dtype=jnp.float32)[None, :] + nrm((ne, HA), 0.1)], axis=-1)
    return {
        'x_prompt': nrm((BATCH, SEQ, D_MODEL)),
        'x_sample': nrm((DEC_BATCH, DEC_SEQ, D_MODEL)),
        'state_mlstm_c': nrm((ne, DEC_BATCH, HA, DVA, DKA), 0.1),
        'state_mlstm_n': nrm((ne, DEC_BATCH, HA, DKA), 0.1),
        'state_mlstm_m': nrm((ne, DEC_BATCH, HA)),
        'state_ret': nrm((ne, DEC_BATCH, HB, DKB, DVB), 0.1),
        'cache_k': nrm((no, n_pool, PAGE_SIZE, HC, DHC)),
        'cache_v': nrm((no, n_pool, PAGE_SIZE, HC, DHC)),
        'cache_idx_k': nrm((no, n_pool, PAGE_SIZE, IDX_DIM)),
        'state_conv': nrm((no, DEC_BATCH, CONV_W - 1, D_CONV), 0.5),
        'page_table': page_table,
        'norm_ffn1': gain((DEPTH, D_MODEL)),
        'w_ffn1_in': nrm((DEPTH, D_MODEL, 2 * D_FF), D_MODEL ** -0.5),
        'w_ffn1_out': nrm((DEPTH, D_FF, D_MODEL), D_FF ** -0.5),
        'norm_mix': gain((DEPTH, D_MODEL)),
        'norm_ffn2': gain((DEPTH, D_MODEL)),
        'w_ffn2_in': nrm((DEPTH, D_MODEL, 2 * D_FF), D_MODEL ** -0.5),
        'w_ffn2_out': nrm((DEPTH, D_FF, D_MODEL), D_FF ** -0.5),
        'norm_final': gain((D_MODEL,)),
        'w_ab_in': nrm((ne, D_MODEL, even_in), D_MODEL ** -0.5),
        'b_ab_gate': b_gate,
        'g_mlstm_norm': gain((ne, HA * DVA)),
        'g_ret_norm': gain((ne, HB * DVB)),
        'w_ab_out': nrm((ne, mix_ab_w, D_MODEL), mix_ab_w ** -0.5),
        'w_cd_in': nrm((no, D_MODEL, odd_in), D_MODEL ** -0.5),
        'w_cd_out': nrm((no, mix_cd_w, D_MODEL), mix_cd_w ** -0.5),
        'rel_bias': nrm((N_BUCKETS, HC), 0.5),
        'conv_w': nrm((no, CONV_W, D_CONV), CONV_W ** -0.5),
        'conv_b': nrm((no, D_CONV), 0.02),
        'g_conv_norm': gain((no, D_CONV)),
        'b_conv_norm': nrm((no, D_CONV), 0.02),
    }


def reference(x_prompt, x_sample, state_mlstm_c, state_mlstm_n, state_mlstm_m, state_ret, cache_k, cache_v,
              cache_idx_k, state_conv, page_table, norm_ffn1, w_ffn1_in, w_ffn1_out, norm_mix, norm_ffn2,
              w_ffn2_in, w_ffn2_out, norm_final, w_ab_in, b_ab_gate, g_mlstm_norm, g_ret_norm, w_ab_out,
              w_cd_in, w_cd_out, rel_bias, conv_w, conv_b, g_conv_norm, b_conv_norm):
    f32 = jnp.float32

    def run(x, prompt):
        b_sz, t, _ = x.shape
        past = 0 if prompt else page_table.shape[1] * PAGE_SIZE
        pos = past + jnp.arange(t)
        ab_states, cd_states = [], []
        for l in range(DEPTH):
            x = x + 0.5 * swiglu(rmsnorm(x, norm_ffn1[l]), w_ffn1_in[l], w_ffn1_out[l])
            h = rmsnorm(x, norm_mix[l])
            j = l // 2
            if l % 2 == 0:
                if prompt:
                    c0 = jnp.zeros((b_sz, HA, DVA, DKA), f32)
                    n0 = jnp.zeros((b_sz, HA, DKA), f32)
                    m0 = jnp.zeros((b_sz, HA), f32)
                    s0 = jnp.zeros((b_sz, HB, DKB, DVB), f32)
                else:
                    c0, n0, m0, s0 = (a[j].astype(f32) for a in (state_mlstm_c, state_mlstm_n, state_mlstm_m, state_ret))
                y, st = mixer_ab(h, pos, w_ab_in[j], b_ab_gate[j], g_mlstm_norm[j], g_ret_norm[j], w_ab_out[j], c0, n0, m0, s0)
                ab_states.append(st)
            else:
                if prompt:
                    attend = functools.partial(attend_prompt, rel_bias=rel_bias)
                    prev = jnp.zeros((b_sz, CONV_W - 1, D_CONV), x.dtype)
                else:
                    attend = functools.partial(attend_sample, rel_bias=rel_bias, pool_k=cache_k[j], pool_v=cache_v[j],
                                               pool_idx=cache_idx_k[j], page_table=page_table)
                    prev = state_conv[j]
                y, st = mixer_cd(h, w_cd_in[j], w_cd_out[j], conv_w[j], conv_b[j], g_conv_norm[j], b_conv_norm[j], prev, attend)
                cd_states.append(st)
            x = x + y
            x = x + 0.5 * swiglu(rmsnorm(x, norm_ffn2[l]), w_ffn2_in[l], w_ffn2_out[l])
        x = rmsnorm(x, norm_final)
        return x, [jnp.stack(a) for a in zip(*ab_states)], [jnp.stack(a) for a in zip(*cd_states)]

    y_prompt, (pc, pn, pm, ps), (pk, pv, pik, pcv) = run(x_prompt, True)
    y_sample, (sc, sn, sm, ss), (sk, sv, sik, scv) = run(x_sample, False)
    return (y_prompt, y_sample, pc, sc, pn, sn, pm, sm, ps, ss, pk, sk, pv, sv, pik, sik, pcv, scv)
```

```python
import functools
import math

import numpy as np
import jax
import jax.numpy as jnp
from jax import lax
from jax.experimental import pallas as pl
from jax.experimental.pallas import tpu as pltpu

F32 = jnp.float32
BF16 = jnp.bfloat16
I32 = jnp.int32

D_MODEL = 2048
PAGE_SIZE = 128
HA, DKA, DVA = 4, 128, 256
HB, DKB, DVB = 4, 128, 256
CHUNK = 128
ROPE_BASE = 10000.0
HC, DHC = 8, 128
IDX_HEADS, IDX_DIM = 16, 64
TOPK_MAX = 256
QBLK = 128
N_BUCKETS = 32
MAX_DISTANCE = 128
D_CONV = 1024
CONV_W = 31
EPS = 1e-6

LANE = 128
BF16_ROWS = 16
MIB = 2 ** 20
INT_MIN = -(2 ** 31)
NEG = -1e30
ATT_KC = 512
ATT_D0 = ATT_KC + LANE
ATT_W = 2 * ATT_KC + LANE
CONV_HALO = 32

AB_A_W = 2 * HA * DKA + 2 * HA * DVA
AB_B0 = AB_A_W + 2 * HA
AB_B_W = 2 * HB * DKB + 2 * HB * DVB
CD_A_W = 3 * HC * DHC + IDX_HEADS * IDX_DIM
CD_B0 = CD_A_W + IDX_DIM + IDX_HEADS
CD_B_W = 2 * D_CONV


def _cp(sem, vmem_mib):
    return pltpu.CompilerParams(dimension_semantics=sem, vmem_limit_bytes=vmem_mib * MIB)


def _sigmoid(x):
    return 1.0 / (1.0 + jnp.exp(-x))


def _log_sigmoid(x):
    return jnp.minimum(x, 0.0) - jnp.log1p(jnp.exp(-jnp.abs(x)))


def _dot(a, b):
    return jnp.dot(a, b, preferred_element_type=F32)


def _dot_nt(a, b):
    return lax.dot_general(a, b, (((1,), (1,)), ((), ())), preferred_element_type=F32)


def _pad_rows(a, rows):
    if a.shape[0] >= rows:
        return a
    return jnp.concatenate([a, jnp.zeros((rows - a.shape[0], a.shape[1]), a.dtype)], axis=0)


def _rms_body(x_ref, g_ref, o_ref):
    x = x_ref[...]
    y = x * lax.rsqrt(jnp.mean(x * x, axis=-1, keepdims=True) + EPS) * g_ref[...]
    o_ref[...] = y.astype(o_ref.dtype)


def _rmsnorm(x, g, out_dtype):
    m, d = x.shape
    tm = min(m, 512)
    return pl.pallas_call(
        _rms_body,
        grid=(m // tm,),
        in_specs=[pl.BlockSpec((tm, d), lambda i: (i, 0)), pl.BlockSpec((1, d), lambda i: (0, 0))],
        out_specs=pl.BlockSpec((tm, d), lambda i: (i, 0)),
        out_shape=jax.ShapeDtypeStruct((m, d), out_dtype),
        compiler_params=_cp(("parallel",), 32),
        name="rmsnorm",
    )(x, g.reshape(1, d))


def _ffn_body(x_ref, g_ref, wa_ref, wb_ref, wo_ref, o_ref, xn_ref, *, tm):
    j = pl.program_id(1)

    @pl.when(j == 0)
    def _():
        x = x_ref[...]
        y = x * lax.rsqrt(jnp.mean(x * x, axis=-1, keepdims=True) + EPS) * g_ref[...]
        xn_ref[...] = _pad_rows(y, BF16_ROWS).astype(BF16)
        o_ref[...] = x

    xn = xn_ref[...]
    a = _dot(xn, wa_ref[...].astype(BF16))
    b = _dot(xn, wb_ref[...].astype(BF16))
    h = (a * _sigmoid(a) * b * 0.5).astype(BF16)
    o_ref[...] += _dot(h, wo_ref[...].astype(BF16))[:tm]


def _ffn(x, g, w_in, w_out, layer):
    m, d = x.shape
    f = w_out.shape[1]
    tm = min(m, 1024)
    tf = 256 if m > 64 else 512
    assert m % tm == 0 and f % tf == 0
    nf = f // tf
    return pl.pallas_call(
        functools.partial(_ffn_body, tm=tm),
        grid=(m // tm, nf),
        in_specs=[
            pl.BlockSpec((tm, d), lambda i, j: (i, 0)),
            pl.BlockSpec((1, d), lambda i, j: (0, 0)),
            pl.BlockSpec((None, d, tf), lambda i, j: (layer, 0, j)),
            pl.BlockSpec((None, d, tf), lambda i, j: (layer, 0, j + nf)),
            pl.BlockSpec((None, tf, d), lambda i, j: (layer, j, 0)),
        ],
        out_specs=pl.BlockSpec((tm, d), lambda i, j: (i, 0)),
        out_shape=jax.ShapeDtypeStruct((m, d), F32),
        scratch_shapes=[pltpu.VMEM((max(tm, BF16_ROWS), d), BF16)],
        compiler_params=_cp(("parallel", "arbitrary"), 56),
        name="ffn",
    )(x, g.reshape(1, d), w_in, w_in, w_out)


def _mm_body(*refs, k_sizes, shift, tn, has_res, n_out, tm):
    n_a = len(k_sizes)
    a_refs = refs[:n_a]
    pos = n_a
    w1_ref = refs[pos]
    pos += 1
    w2_ref = None
    if shift:
        w2_ref = refs[pos]
        pos += 1
    res_ref = None
    if has_res:
        res_ref = refs[pos]
        pos += 1
    out_refs = refs[pos:pos + n_out]
    wt_ref = refs[pos + n_out]
    k_total = sum(k_sizes)
    rows_per_step = 256

    @pl.when(pl.program_id(1) == 0)
    def _():
        def cast_rows(c, carry):
            rows = pl.ds(pl.multiple_of(c * rows_per_step, rows_per_step), rows_per_step)
            w = w1_ref[rows, :]
            if shift:
                w = jnp.concatenate([w, w2_ref[rows, :]], axis=1)[:, shift:shift + tn]
            wt_ref[rows, :] = w.astype(BF16)
            return carry

        lax.fori_loop(0, k_total // rows_per_step, cast_rows, 0)

    acc = None
    k0 = 0
    for a_ref, ks in zip(a_refs, k_sizes):
        a = _pad_rows(a_ref[...], BF16_ROWS).astype(BF16)
        d = _dot(a, wt_ref[k0:k0 + ks, :])
        acc = d if acc is None else acc + d
        k0 += ks
    acc = acc[:tm]
    if has_res:
        acc = acc + res_ref[...]
    for o_ref in out_refs:
        o_ref[...] = acc.astype(o_ref.dtype)


def _matmul(a_parts, w, layer, col0, width, tn, res=None, out_dtypes=(F32,)):
    m = a_parts[0].shape[0]
    k_sizes = tuple(a.shape[1] for a in a_parts)
    k_total = sum(k_sizes)
    assert w.shape[1] == k_total
    shift = col0 % LANE
    c_al = col0 - shift
    assert c_al % tn == 0 and width % tn == 0 and tn % LANE == 0
    tm = min(m, 1024)
    assert m % tm == 0
    cb = c_al // tn
    in_specs = [pl.BlockSpec((tm, ks), lambda j, i: (i, 0)) for ks in k_sizes]
    args = list(a_parts)
    in_specs.append(pl.BlockSpec((None, k_total, tn), lambda j, i: (layer, 0, cb + j)))
    args.append(w)
    if shift:
        lb, step = (c_al + tn) // LANE, tn // LANE
        in_specs.append(pl.BlockSpec((None, k_total, LANE), lambda j, i: (layer, 0, lb + j * step)))
        args.append(w)
    if res is not None:
        in_specs.append(pl.BlockSpec((tm, tn), lambda j, i: (i, j)))
        args.append(res)
    outs = pl.pallas_call(
        functools.partial(_mm_body, k_sizes=k_sizes, shift=shift, tn=tn, has_res=res is not None,
                          n_out=len(out_dtypes), tm=tm),
        grid=(width // tn, m // tm),
        in_specs=in_specs,
        out_specs=[pl.BlockSpec((tm, tn), lambda j, i: (i, j)) for _ in out_dtypes],
        out_shape=[jax.ShapeDtypeStruct((m, width), dt) for dt in out_dtypes],
        scratch_shapes=[pltpu.VMEM((k_total, tn), BF16)],
        compiler_params=_cp(("parallel", "arbitrary"), 48),
        name="matmul",
    )(*args)
    return outs


def _rope(x, cosf, sinf):
    return x * cosf + pltpu.roll(x, shift=x.shape[-1] // 2, axis=1) * sinf


def _ab_scan_body(sdec_ref, bg_ref, ya_ref, yg_ref, yb_ref, cos_ref, sin_ref, dmat_ref, qdec_ref, kdec_ref,
                  ga_ref, gb_ref, ma_ref, mb_ref, c_out, n_out, m_out, s_out, ct_scr, n_scr, m_scr, s_scr):
    c = pl.program_id(1)
    L = CHUNK

    @pl.when(c == 0)
    def _():
        ct_scr[...] = jnp.zeros_like(ct_scr)
        n_scr[...] = jnp.zeros_like(n_scr)
        m_scr[...] = jnp.zeros_like(m_scr)
        s_scr[...] = jnp.zeros_like(s_scr)

    g = yg_ref[...]
    g_t = g.T
    row = lax.broadcasted_iota(I32, (L, L), 0)
    col = lax.broadcasted_iota(I32, (L, L), 1)
    lower = col <= row
    cosf = cos_ref[...]
    sinf = sin_ref[...]

    for h in range(HA):
        q = ya_ref[:, h * DKA:(h + 1) * DKA]
        k = ya_ref[:, HA * DKA + h * DKA:HA * DKA + (h + 1) * DKA] * (DKA ** -0.5)
        v = ya_ref[:, 2 * HA * DKA + h * DVA:2 * HA * DKA + (h + 1) * DVA]
        o = ya_ref[:, 2 * HA * DKA + HA * DVA + h * DVA:2 * HA * DKA + HA * DVA + (h + 1) * DVA]
        b_i = bg_ref[h]
        b_f = bg_ref[HA + h]
        i_col = g[:, h:h + 1] + b_i
        f_col = _log_sigmoid(g[:, HA + h:HA + h + 1] + b_f)
        i_row = g_t[h:h + 1, :] + b_i
        f_row = _log_sigmoid(g_t[HA + h:HA + h + 1, :] + b_f)
        b_row = jnp.sum(jnp.where(row <= col, f_col, 0.0), axis=0, keepdims=True)
        b_col = jnp.sum(jnp.where(lower, f_row, 0.0), axis=1, keepdims=True)
        d_log = jnp.where(lower, b_col - b_row + i_row, -jnp.inf)
        m_prev = m_scr[h][:, 0:1]
        g_in = b_col + m_prev
        m = jnp.maximum(g_in, jnp.max(d_log, axis=1, keepdims=True))
        k_t = k.T.astype(BF16)
        q16 = q.astype(BF16)
        s = _dot(q16, k_t) * jnp.exp(d_log - m)
        inter = jnp.exp(g_in - m)
        ct = ct_scr[h]
        num = _dot(s.astype(BF16), v.astype(BF16)) + inter * _dot(q16, ct.astype(BF16))
        n_prev = n_scr[h]
        den = jnp.sum(s, axis=1, keepdims=True) + inter * jnp.sum(q * n_prev, axis=1, keepdims=True)
        hh = num / jnp.maximum(jnp.abs(den), jnp.exp(-m))
        m_new = m[L - 1:L, :]
        b_last = b_col[L - 1:L, :]
        decay = jnp.exp(b_last + m_prev - m_new)
        wk = jnp.exp(b_last - b_col + i_col - m_new)
        ct_scr[h] = decay * ct + _dot(k_t, (wk * v).astype(BF16))
        n_scr[h] = decay * n_prev + jnp.sum(wk * k, axis=0, keepdims=True)
        m_scr[h] = jnp.broadcast_to(m_new, (1, LANE))
        ha = hh * lax.rsqrt(jnp.mean(hh * hh, axis=1, keepdims=True) + EPS) * ga_ref[h]
        ma_ref[:, h * DVA:(h + 1) * DVA] = (ha * _sigmoid(o)).astype(BF16)

    for h in range(HB):
        qr = _rope(yb_ref[:, h * DKB:(h + 1) * DKB], cosf, sinf)
        kr = _rope(yb_ref[:, HB * DKB + h * DKB:HB * DKB + (h + 1) * DKB], cosf, sinf) * (DKB ** -0.5)
        vb = yb_ref[:, 2 * HB * DKB + h * DVB:2 * HB * DKB + (h + 1) * DVB]
        gate = yb_ref[:, 2 * HB * DKB + HB * DVB + h * DVB:2 * HB * DKB + HB * DVB + (h + 1) * DVB]
        kr_t = kr.T.astype(BF16)
        qr16 = qr.astype(BF16)
        a = _dot(qr16, kr_t) * dmat_ref[h]
        sp = s_scr[h]
        out = _dot(a.astype(BF16), vb.astype(BF16)) + qdec_ref[h] * _dot(qr16, sp.astype(BF16))
        s_scr[h] = sdec_ref[h] * sp + _dot(kr_t, (kdec_ref[h] * vb).astype(BF16))
        mu = jnp.mean(out, axis=1, keepdims=True)
        var = jnp.mean(jnp.square(out - mu), axis=1, keepdims=True)
        hb = (out - mu) * lax.rsqrt(var + EPS) * gb_ref[h]
        mb_ref[:, h * DVB:(h + 1) * DVB] = (hb * (gate * _sigmoid(gate))).astype(BF16)

    @pl.when(c == pl.num_programs(1) - 1)
    def _():
        for h in range(HA):
            c_out[h] = ct_scr[h].T
            n_out[h] = n_scr[h]
            m_out[h] = m_scr[h]
            s_out[h] = s_scr[h]


def _retention_consts(L):
    log_gamma = jnp.log1p(-jnp.exp2(-5.0 - jnp.arange(HB, dtype=F32)))
    j = jnp.arange(L, dtype=F32)
    diff = j[:, None] - j[None, :]
    decay_mat = jnp.where(diff >= 0, jnp.exp(log_gamma[:, None, None] * jnp.maximum(diff, 0.0)), 0.0)
    q_decay = jnp.exp(log_gamma[:, None] * (j + 1.0))
    k_decay = jnp.exp(log_gamma[:, None] * (L - 1.0 - j))
    state_decay = jnp.exp(log_gamma * L)
    return decay_mat, q_decay, k_decay, state_decay


def _rope_tables(pos):
    half = DKB // 2
    freqs = ROPE_BASE ** (-jnp.arange(half, dtype=F32) / half)
    ang = pos.astype(F32)[:, None] * freqs[None, :]
    cos, sin = jnp.cos(ang), jnp.sin(ang)
    return jnp.concatenate([cos, cos], axis=1), jnp.concatenate([-sin, sin], axis=1)


def _ab_scan(ya, yg, yb, b_gate, g_a, g_b, batch, seq):
    L = CHUNK
    nc = seq // L
    cosf, sinf = _rope_tables(jnp.arange(seq))
    dmat, qdec, kdec, sdec = _retention_consts(L)
    qdec = jnp.broadcast_to(qdec[:, :, None], (HB, L, DVB))
    kdec = jnp.broadcast_to(kdec[:, :, None], (HB, L, DVB))
    smem = pl.BlockSpec(memory_space=pltpu.SMEM)
    rows = lambda b, c: (b * nc + c, 0)
    const3 = lambda b, c: (0, 0, 0)
    state = lambda b, c: (b, 0, 0, 0)
    m = batch * seq
    return pl.pallas_call(
        _ab_scan_body,
        grid=(batch, nc),
        in_specs=[
            smem, smem,
            pl.BlockSpec((L, AB_A_W), rows),
            pl.BlockSpec((L, LANE), rows),
            pl.BlockSpec((L, AB_B_W), rows),
            pl.BlockSpec((L, DKB), lambda b, c: (c, 0)),
            pl.BlockSpec((L, DKB), lambda b, c: (c, 0)),
            pl.BlockSpec((HB, L, L), const3),
            pl.BlockSpec((HB, L, DVB), const3),
            pl.BlockSpec((HB, L, DVB), const3),
            pl.BlockSpec((HA, 1, DVA), const3),
            pl.BlockSpec((HB, 1, DVB), const3),
        ],
        out_specs=[
            pl.BlockSpec((L, HA * DVA), rows),
            pl.BlockSpec((L, HB * DVB), rows),
            pl.BlockSpec((None, HA, DVA, DKA), state),
            pl.BlockSpec((None, HA, 1, DKA), state),
            pl.BlockSpec((None, HA, 1, LANE), state),
            pl.BlockSpec((None, HB, DKB, DVB), state),
        ],
        out_shape=[
            jax.ShapeDtypeStruct((m, HA * DVA), BF16),
            jax.ShapeDtypeStruct((m, HB * DVB), BF16),
            jax.ShapeDtypeStruct((batch, HA, DVA, DKA), F32),
            jax.ShapeDtypeStruct((batch, HA, 1, DKA), F32),
            jax.ShapeDtypeStruct((batch, HA, 1, LANE), F32),
            jax.ShapeDtypeStruct((batch, HB, DKB, DVB), F32),
        ],
        scratch_shapes=[
            pltpu.VMEM((HA, DKA, DVA), F32),
            pltpu.VMEM((HA, 1, DKA), F32),
            pltpu.VMEM((HA, 1, LANE), F32),
            pltpu.VMEM((HB, DKB, DVB), F32),
        ],
        compiler_params=_cp(("parallel", "arbitrary"), 40),
        name="ab_scan",
    )(sdec, b_gate, ya, yg, yb, cosf, sinf, dmat, qdec, kdec,
      g_a.reshape(HA, 1, DVA), g_b.reshape(HB, 1, DVB))


def _lane_bcast_col(row_vec):
    n = row_vec.shape[1]
    return jnp.broadcast_to(row_vec, (LANE, n)).T


def _ab_step_body(sdec_ref, bg_ref, ya_ref, yg_ref, yb_ref, cos_ref, sin_ref, ga_ref, gb_ref,
                  c_ref, n_ref, m_ref, s_ref, mix_ref, c_out, n_out, m_out, s_out):
    ya = ya_ref[...]
    yg = yg_ref[...]
    yb = yb_ref[...]
    cosf = cos_ref[...]
    sinf = sin_ref[...]

    for h in range(HA):
        q = ya[:, h * DKA:(h + 1) * DKA]
        k = ya[:, HA * DKA + h * DKA:HA * DKA + (h + 1) * DKA] * (DKA ** -0.5)
        v = ya[:, 2 * HA * DKA + h * DVA:2 * HA * DKA + (h + 1) * DVA]
        o = ya[:, 2 * HA * DKA + HA * DVA + h * DVA:2 * HA * DKA + HA * DVA + (h + 1) * DVA]
        i_pre = yg[:, h:h + 1] + bg_ref[h]
        log_f = _log_sigmoid(yg[:, HA + h:HA + h + 1] + bg_ref[HA + h])
        m_prev = m_ref[h][:, 0:1]
        g_in = log_f + m_prev
        m = jnp.maximum(g_in, i_pre)
        w_in = jnp.exp(i_pre - m)
        inter = jnp.exp(g_in - m)
        s = jnp.sum(q * k, axis=1, keepdims=True) * w_in
        cmat = c_ref[h]
        cq = jnp.sum(cmat * q, axis=1, keepdims=True)
        cq_row = jnp.broadcast_to(cq, (DVA, LANE)).T[0:1, :]
        num = s * v + inter * cq_row
        n_prev = n_ref[h]
        den = s + inter * jnp.sum(n_prev * q, axis=1, keepdims=True)
        hrow = num / jnp.maximum(jnp.abs(den), jnp.exp(-m))
        c_out[h] = inter * cmat + (w_in * _lane_bcast_col(v)) * k
        n_out[h] = inter * n_prev + w_in * k
        m_out[h] = jnp.broadcast_to(m, (1, LANE))
        ha = hrow * lax.rsqrt(jnp.mean(hrow * hrow, axis=1, keepdims=True) + EPS) * ga_ref[h]
        mix_ref[:, h * DVA:(h + 1) * DVA] = ha * _sigmoid(o)

    base = HA * DVA
    for h in range(HB):
        qr = _rope(yb[:, h * DKB:(h + 1) * DKB], cosf, sinf)
        kr = _rope(yb[:, HB * DKB + h * DKB:HB * DKB + (h + 1) * DKB], cosf, sinf) * (DKB ** -0.5)
        vb = yb[:, 2 * HB * DKB + h * DVB:2 * HB * DKB + (h + 1) * DVB]
        gate = yb[:, 2 * HB * DKB + HB * DVB + h * DVB:2 * HB * DKB + HB * DVB + (h + 1) * DVB]
        gamma = sdec_ref[h]
        smat = s_ref[h]
        q_col = _lane_bcast_col(qr)
        k_col = _lane_bcast_col(kr)
        qs = jnp.concatenate([jnp.sum(q_col * smat[:, t * LANE:(t + 1) * LANE], axis=0, keepdims=True)
                              for t in range(DVB // LANE)], axis=1)
        out = jnp.sum(qr * kr, axis=1, keepdims=True) * vb + gamma * qs
        s_out[h] = gamma * smat + jnp.concatenate([k_col * vb[:, t * LANE:(t + 1) * LANE]
                                                   for t in range(DVB // LANE)], axis=1)
        mu = jnp.mean(out, axis=1, keepdims=True)
        var = jnp.mean(jnp.square(out - mu), axis=1, keepdims=True)
        hb = (out - mu) * lax.rsqrt(var + EPS) * gb_ref[h]
        mix_ref[:, base + h * DVB:base + (h + 1) * DVB] = hb * (gate * _sigmoid(gate))


def _ab_step(ya, yg, yb, b_gate, g_a, g_b, c0, n0, m0, s0, pos0):
    bs = ya.shape[0]
    cosf, sinf = _rope_tables(pos0 + jnp.arange(1))
    _, _, _, sdec = _retention_consts(1)
    smem = pl.BlockSpec(memory_space=pltpu.SMEM)
    full = lambda shape: pl.BlockSpec(shape, lambda b: tuple(0 for _ in shape))
    row = lambda width: pl.BlockSpec((None, 1, width), lambda b: (b, 0, 0))
    state = lambda b: (b, 0, 0, 0)
    mix_w = HA * DVA + HB * DVB
    outs = pl.pallas_call(
        _ab_step_body,
        grid=(bs,),
        in_specs=[
            smem, smem,
            row(AB_A_W), row(LANE), row(AB_B_W),
            full((1, DKB)), full((1, DKB)),
            full((HA, 1, DVA)), full((HB, 1, DVB)),
            pl.BlockSpec((None, HA, DVA, DKA), state),
            pl.BlockSpec((None, HA, 1, DKA), state),
            pl.BlockSpec((None, HA, 1, LANE), state),
            pl.BlockSpec((None, HB, DKB, DVB), state),
        ],
        out_specs=[
            row(mix_w),
            pl.BlockSpec((None, HA, DVA, DKA), state),
            pl.BlockSpec((None, HA, 1, DKA), state),
            pl.BlockSpec((None, HA, 1, LANE), state),
            pl.BlockSpec((None, HB, DKB, DVB), state),
        ],
        out_shape=[
            jax.ShapeDtypeStruct((bs, 1, mix_w), F32),
            jax.ShapeDtypeStruct((bs, HA, DVA, DKA), F32),
            jax.ShapeDtypeStruct((bs, HA, 1, DKA), F32),
            jax.ShapeDtypeStruct((bs, HA, 1, LANE), F32),
            jax.ShapeDtypeStruct((bs, HB, DKB, DVB), F32),
        ],
        compiler_params=_cp(("parallel",), 32),
        name="ab_step",
    )(sdec, b_gate, ya.reshape(bs, 1, AB_A_W), yg.reshape(bs, 1, LANE), yb.reshape(bs, 1, AB_B_W), cosf, sinf,
      g_a.reshape(HA, 1, DVA), g_b.reshape(HB, 1, DVB),
      c0, n0.reshape(bs, HA, 1, DKA), jnp.broadcast_to(m0[:, :, None, None], (bs, HA, 1, LANE)), s0)
    return (outs[0].reshape(bs, mix_w),) + tuple(outs[1:])


def _bucket_np(dist):
    exact = N_BUCKETS // 2
    dist = np.maximum(dist, 0)
    ratio = np.maximum(dist, 1).astype(np.float32) / np.float32(exact)
    log_ratio = np.log(ratio) / np.float32(math.log(MAX_DISTANCE / exact))
    large = np.minimum(exact + (log_ratio * (N_BUCKETS - exact)).astype(np.int32), N_BUCKETS - 1)
    return np.where(dist < exact, dist, large).astype(np.int32)


def _bucket_thresholds():
    table = _bucket_np(np.arange(4 * MAX_DISTANCE))
    return [int(np.argmax(table >= k)) for k in range(1, N_BUCKETS)]


def _bias_table_body(rb_ref, idx_ref, o_ref):
    idx = idx_ref[...]
    for h in range(HC):
        acc = jnp.zeros(idx.shape, F32)
        for bkt in range(N_BUCKETS):
            acc = jnp.where(idx == bkt, rb_ref[bkt, h], acc)
        o_ref[h] = acc


def _bias_table(rel_bias):
    nw = ATT_W // LANE
    t = np.arange(QBLK)[None, :, None]
    w = (np.arange(nw)[:, None, None] * LANE + np.arange(LANE)[None, None, :])
    idx = _bucket_np(t + ATT_D0 - w).reshape(nw * QBLK, LANE)
    out = pl.pallas_call(
        _bias_table_body,
        in_specs=[pl.BlockSpec(memory_space=pltpu.SMEM), pl.BlockSpec((nw * QBLK, LANE), lambda: (0, 0))],
        out_specs=pl.BlockSpec((HC, nw * QBLK, LANE), lambda: (0, 0, 0)),
        out_shape=jax.ShapeDtypeStruct((HC, nw * QBLK, LANE), F32),
        name="bias_table",
    )(rel_bias, jnp.asarray(idx))
    return out.reshape(HC, nw, QBLK, LANE)


def _to_key(x):
    bits = lax.bitcast_convert_type(x, I32)
    return bits ^ ((bits >> 31) & jnp.int32(0x7FFFFFFF))


def _attn_body(q_ref, qi_ref, wi_ref, ki_ref, k_ref, v_ref, tab_ref, o_ref,
               key_scr, thr_scr, m_scr, l_scr, acc_scr, *, topk):
    qb = pl.program_id(1)
    kc = pl.program_id(2)
    kcs = ATT_KC
    last = (qb * QBLK) // kcs
    n_tiles = kcs // LANE

    @pl.when(kc == 0)
    def _():
        wi = wi_ref[...]
        t_idx = qb * QBLK + lax.broadcasted_iota(I32, (QBLK, kcs), 0)

        def score(c, carry):
            base = pl.multiple_of(c * kcs, kcs)
            ki = ki_ref[pl.ds(base, kcs), :][:, 0:IDX_DIM]
            acc = jnp.zeros((QBLK, kcs), F32)
            for h in range(IDX_HEADS):
                lg = _dot_nt(qi_ref[:, h * IDX_DIM:(h + 1) * IDX_DIM], ki)
                acc = acc + wi[:, IDX_DIM + h:IDX_DIM + h + 1] * jnp.maximum(lg, 0.0)
            s_idx = base + lax.broadcasted_iota(I32, (QBLK, kcs), 1)
            key_scr[c] = jnp.where(s_idx <= t_idx, _to_key(acc + 0.0), jnp.int32(INT_MIN))
            return carry

        lax.fori_loop(0, last + 1, score, 0)

        def bisect(i, thr):
            cand = thr + lax.shift_left(jnp.int32(1), 31 - i)

            def count(c, a):
                ge = jnp.where(key_scr[c] >= cand, 1.0, 0.0)
                part = ge[:, 0:LANE]
                for t in range(1, n_tiles):
                    part = part + ge[:, t * LANE:(t + 1) * LANE]
                return a + part

            a = lax.fori_loop(0, last + 1, count, jnp.zeros((QBLK, LANE), F32))
            n = jnp.sum(a, axis=1, keepdims=True)
            return jnp.where(n >= topk, cand, thr)

        thr = lax.fori_loop(0, 32, bisect, jnp.full((QBLK, 1), INT_MIN, I32))
        thr_scr[...] = jnp.maximum(thr, jnp.int32(INT_MIN + 1))
        m_scr[...] = jnp.full_like(m_scr, NEG)
        l_scr[...] = jnp.zeros_like(l_scr)
        acc_scr[...] = jnp.zeros_like(acc_scr)

    @pl.when(kc <= last)
    def _():
        mask_bias = jnp.where(key_scr[kc] >= thr_scr[...], 0.0, NEG)
        w0 = jnp.maximum(ATT_D0 - qb * QBLK + kc * kcs, 0) // LANE
        for h in range(HC):
            hs = slice(h * DHC, (h + 1) * DHC)
            bias = jnp.concatenate([tab_ref[h, w0 + t] for t in range(n_tiles)], axis=1)
            lg = _dot_nt(q_ref[:, hs], k_ref[:, hs]) * (DHC ** -0.5) + bias + mask_bias
            m_old = m_scr[h]
            m_new = jnp.maximum(m_old, jnp.max(lg, axis=1, keepdims=True))
            alpha = jnp.exp(m_old - m_new)
            p = jnp.exp(lg - m_new)
            l_scr[h] = alpha * l_scr[h] + jnp.sum(p, axis=1, keepdims=True)
            acc_scr[:, hs] = alpha * acc_scr[:, hs] + _dot(p.astype(BF16), v_ref[:, hs])
            m_scr[h] = m_new

    @pl.when(kc == last)
    def _():
        for h in range(HC):
            hs = slice(h * DHC, (h + 1) * DHC)
            o_ref[:, hs] = (acc_scr[:, hs] / l_scr[h]).astype(o_ref.dtype)


def _attn_prompt(ya16, yg, yg16, table, batch, seq):
    nq = seq // QBLK
    nkc = seq // ATT_KC
    topk = min(TOPK_MAX, seq // 4)
    kv_row = lambda b, qb, kc: b * nkc + jnp.minimum(kc, (qb * QBLK) // ATT_KC)
    q_row = lambda b, qb, kc: b * nq + qb
    hd = HC * DHC
    return pl.pallas_call(
        functools.partial(_attn_body, topk=topk),
        grid=(batch, nq, nkc),
        in_specs=[
            pl.BlockSpec((QBLK, hd), lambda b, qb, kc: (q_row(b, qb, kc), 0)),
            pl.BlockSpec((QBLK, IDX_HEADS * IDX_DIM), lambda b, qb, kc: (q_row(b, qb, kc), 3)),
            pl.BlockSpec((QBLK, LANE), lambda b, qb, kc: (q_row(b, qb, kc), 0)),
            pl.BlockSpec((seq, LANE), lambda b, qb, kc: (b, 0)),
            pl.BlockSpec((ATT_KC, hd), lambda b, qb, kc: (kv_row(b, qb, kc), 1)),
            pl.BlockSpec((ATT_KC, hd), lambda b, qb, kc: (kv_row(b, qb, kc), 2)),
            pl.BlockSpec((HC, ATT_W // LANE, QBLK, LANE), lambda b, qb, kc: (0, 0, 0, 0)),
        ],
        out_specs=pl.BlockSpec((QBLK, hd), lambda b, qb, kc: (q_row(b, qb, kc), 0)),
        out_shape=jax.ShapeDtypeStruct((batch * seq, hd), BF16),
        scratch_shapes=[
            pltpu.VMEM((nkc, QBLK, ATT_KC), I32),
            pltpu.VMEM((QBLK, 1), I32),
            pltpu.VMEM((HC, QBLK, 1), F32),
            pltpu.VMEM((HC, QBLK, 1), F32),
            pltpu.VMEM((QBLK, hd), F32),
        ],
        compiler_params=_cp(("parallel", "arbitrary", "arbitrary"), 48),
        name="attn_prompt",
    )(ya16, ya16, yg, yg16, ya16, ya16, table)


def _conv_body(yb_ref, cw_ref, cb_ref, gn_ref, bn_ref, o_ref, st_ref, ubuf, *, tt):
    t = pl.program_id(1)
    halo = CONV_HALO
    rb = 32

    @pl.when(t == 0)
    def _():
        ubuf[0:halo, :] = jnp.zeros((halo, D_CONV), F32)

    @pl.when(t > 0)
    def _():
        ubuf[0:halo, :] = ubuf[tt:tt + halo, :]

    ubuf[halo:halo + tt, :] = yb_ref[:, 0:D_CONV] * _sigmoid(yb_ref[:, D_CONV:2 * D_CONV])
    for r in range(tt // rb):
        acc = jnp.broadcast_to(cb_ref[...], (rb, D_CONV))
        for w in range(CONV_W):
            off = halo - (CONV_W - 1) + w + r * rb
            acc = acc + ubuf[off:off + rb, :] * cw_ref[w:w + 1, :]
        mu = jnp.mean(acc, axis=1, keepdims=True)
        var = jnp.mean(jnp.square(acc - mu), axis=1, keepdims=True)
        yn = (acc - mu) * lax.rsqrt(var + EPS) * gn_ref[...] + bn_ref[...]
        o_ref[r * rb:(r + 1) * rb, :] = (yn * _sigmoid(yn)).astype(o_ref.dtype)

    @pl.when(t == pl.num_programs(1) - 1)
    def _():
        st_ref[...] = ubuf[halo + tt - (CONV_W - 1):halo + tt, :]


def _conv_prompt(yb, cw, cb, gn, bn, batch, seq):
    tt = 256
    nt = seq // tt
    const = lambda b, t: (0, 0)
    return pl.pallas_call(
        functools.partial(_conv_body, tt=tt),
        grid=(batch, nt),
        in_specs=[
            pl.BlockSpec((tt, 2 * D_CONV), lambda b, t: (b * nt + t, 0)),
            pl.BlockSpec((CONV_W, D_CONV), const),
            pl.BlockSpec((1, D_CONV), const),
            pl.BlockSpec((1, D_CONV), const),
            pl.BlockSpec((1, D_CONV), const),
        ],
        out_specs=[
            pl.BlockSpec((tt, D_CONV), lambda b, t: (b * nt + t, 0)),
            pl.BlockSpec((None, CONV_W - 1, D_CONV), lambda b, t: (b, 0, 0)),
        ],
        out_shape=[
            jax.ShapeDtypeStruct((batch * seq, D_CONV), BF16),
            jax.ShapeDtypeStruct((batch, CONV_W - 1, D_CONV), F32),
        ],
        scratch_shapes=[pltpu.VMEM((CONV_HALO + tt, D_CONV), F32)],
        compiler_params=_cp(("parallel", "arbitrary"), 32),
        name="conv_prompt",
    )(yb, cw, cb.reshape(1, D_CONV), gn.reshape(1, D_CONV), bn.reshape(1, D_CONV))


def _rows_to_sublanes(row_vec, n, width, start=0):
    return jnp.concatenate([row_vec[:, start + i * width:start + (i + 1) * width] for i in range(n)], axis=0)


def _select_body(pt_ref, ya_ref, yg_ref, pool_ref, pos_ref, pbuf, sc_scr, slot_scr, sem, *, layer, n_pages, topk):
    b = pl.program_id(0)
    group = 16

    def page_copy(p):
        return pltpu.make_async_copy(pool_ref.at[layer, pt_ref[b, p]], pbuf.at[p], sem.at[0])

    def start(p, carry):
        page_copy(p).start()
        return carry

    def wait(p, carry):
        page_copy(p).wait()
        return carry

    lax.fori_loop(0, n_pages, start, 0)
    lax.fori_loop(0, n_pages, wait, 0)

    q_row = ya_ref[...]
    g_row = yg_ref[...]
    qi = _rows_to_sublanes(q_row, IDX_HEADS, IDX_DIM, start=3 * HC * DHC).astype(BF16)
    wi = _rows_to_sublanes(g_row, IDX_HEADS, 1, start=IDX_DIM)
    ki_new = g_row[:, 0:IDX_DIM].astype(BF16)

    for g in range(n_pages // group):
        keys = pbuf[g * group:(g + 1) * group].reshape(group * PAGE_SIZE, IDX_DIM).astype(BF16)
        lg = _dot_nt(qi, keys)
        sc = jnp.sum(jnp.maximum(lg, 0.0) * wi, axis=0, keepdims=True)
        for p in range(group):
            sc_scr[g * group + p:g * group + p + 1, :] = sc[:, p * PAGE_SIZE:(p + 1) * PAGE_SIZE]
    lg_new = jnp.sum(qi.astype(F32) * ki_new.astype(F32), axis=1, keepdims=True)
    sc_new = jnp.sum(jnp.maximum(lg_new, 0.0) * wi, axis=0, keepdims=True)
    key = _to_key(sc_scr[...] + 0.0)
    key_new = _to_key(sc_new + 0.0)

    def count(mask, mask_new):
        return jnp.sum(jnp.sum(jnp.where(mask, 1.0, 0.0), axis=1, keepdims=True), axis=0, keepdims=True) \
            + jnp.where(mask_new, 1.0, 0.0)

    def bisect(i, thr):
        cand = thr + lax.shift_left(jnp.int32(1), 31 - i)
        return jnp.where(count(key >= cand, key_new >= cand) >= topk, cand, thr)

    thr = lax.fori_loop(0, 32, bisect, jnp.full((1, 1), INT_MIN, I32))

    r_i = lax.broadcasted_iota(I32, (PAGE_SIZE, PAGE_SIZE), 0)
    c_i = lax.broadcasted_iota(I32, (PAGE_SIZE, PAGE_SIZE), 1)
    upper = jnp.where(r_i <= c_i, 1.0, 0.0).astype(BF16)
    pr = lax.broadcasted_iota(I32, (n_pages, n_pages), 0)
    pc = lax.broadcasted_iota(I32, (n_pages, n_pages), 1)
    before = jnp.where(pc < pr, 1.0, 0.0).astype(BF16)

    def rank(mask):
        inc = _dot(jnp.where(mask, 1.0, 0.0).astype(BF16), upper)
        tot = jnp.broadcast_to(inc[:, PAGE_SIZE - 1:PAGE_SIZE], (n_pages, PAGE_SIZE))
        return inc + _dot(before, tot.astype(BF16))

    gt = key > thr
    eq = key == thr
    n_gt = count(gt, key_new > thr)
    slot_eq = n_gt + rank(eq) - 1.0
    slot = jnp.where(gt, rank(gt) - 1.0, jnp.where(eq & (slot_eq < topk), slot_eq, -1.0))
    slot_scr[...] = slot

    r_iota = lax.broadcasted_iota(I32, (topk, PAGE_SIZE), 0).astype(F32)
    lane = lax.broadcasted_iota(I32, (1, PAGE_SIZE), 1).astype(F32)

    def compact(p, acc):
        val = lax.convert_element_type(p * PAGE_SIZE + 1, F32) + lane
        return acc + jnp.where(slot_scr[pl.ds(p, 1), :] == r_iota, val, 0.0)

    acc = lax.fori_loop(0, n_pages, compact, jnp.zeros((topk, PAGE_SIZE), F32))
    pos = jnp.sum(acc, axis=1, keepdims=True) - 1.0
    pos_ref[...] = jnp.broadcast_to(pos, (topk, PAGE_SIZE)).astype(I32)


def _select_sample(ya, yg, pool_idx, page_table, layer, topk):
    bs = ya.shape[0]
    n_pages = page_table.shape[1]
    row = lambda width: pl.BlockSpec((None, 1, width), lambda b, pt: (b, 0, 0))
    out = pl.pallas_call(
        functools.partial(_select_body, layer=layer, n_pages=n_pages, topk=topk),
        grid_spec=pltpu.PrefetchScalarGridSpec(
            num_scalar_prefetch=1,
            grid=(bs,),
            in_specs=[row(ya.shape[1]), row(yg.shape[1]), pl.BlockSpec(memory_space=pl.ANY)],
            out_specs=pl.BlockSpec((None, topk, PAGE_SIZE), lambda b, pt: (b, 0, 0)),
            scratch_shapes=[
                pltpu.VMEM((n_pages, PAGE_SIZE, IDX_DIM), F32),
                pltpu.VMEM((n_pages, PAGE_SIZE), F32),
                pltpu.VMEM((n_pages, PAGE_SIZE), F32),
                pltpu.SemaphoreType.DMA((1,)),
            ],
        ),
        out_shape=jax.ShapeDtypeStruct((bs, topk, PAGE_SIZE), I32),
        compiler_params=_cp(("arbitrary",), 40),
        name="select_sample",
    )(page_table, ya.reshape(bs, 1, -1), yg.reshape(bs, 1, -1), pool_idx)
    return out[:, :, 0]


def _sattn_body(pos_ref, pt_ref, ya_ref, posv_ref, rb_ref, kpool, vpool, o_ref, kbuf, vbuf, sem,
                *, layer, past, topk, thresholds):
    b = pl.program_id(0)

    def row_copies(r):
        pos = jnp.maximum(pos_ref[b, r], 0)
        page = pt_ref[b, pos // PAGE_SIZE]
        off = pos % PAGE_SIZE
        return (pltpu.make_async_copy(kpool.at[layer, page, off], kbuf.at[r], sem.at[0]),
                pltpu.make_async_copy(vpool.at[layer, page, off], vbuf.at[r], sem.at[1]))

    def start(r, carry):
        @pl.when(pos_ref[b, r] >= 0)
        def _():
            ck, cv = row_copies(r)
            ck.start()
            cv.start()
        return carry

    lax.fori_loop(0, topk, start, 0)

    y_row = ya_ref[...]
    k_new = _rows_to_sublanes(y_row, HC, DHC, start=HC * DHC)
    v_new = _rows_to_sublanes(y_row, HC, DHC, start=2 * HC * DHC)

    def wait(r, carry):
        @pl.when(pos_ref[b, r] >= 0)
        def _():
            ck, cv = row_copies(r)
            ck.wait()
            cv.wait()

        @pl.when(pos_ref[b, r] < 0)
        def _():
            kbuf[r] = k_new
            vbuf[r] = v_new
        return carry

    lax.fori_loop(0, topk, wait, 0)

    posv = posv_ref[...]
    dist = jnp.where(posv < 0, 0, past - posv)
    bucket = jnp.zeros(dist.shape, I32)
    for thr in thresholds:
        bucket = bucket + jnp.where(dist >= thr, 1, 0)
    for h in range(HC):
        hs = slice(h * DHC, (h + 1) * DHC)
        bias = jnp.zeros(dist.shape, F32)
        for bkt in range(N_BUCKETS):
            bias = jnp.where(bucket == bkt, rb_ref[bkt, h], bias)
        q16 = jnp.broadcast_to(y_row[:, hs], (BF16_ROWS, DHC)).astype(BF16)
        lg = _dot_nt(q16, kbuf[:, h, :].astype(BF16)) * (DHC ** -0.5) + bias
        p = jnp.exp(lg - jnp.max(lg, axis=1, keepdims=True))
        p = p / jnp.sum(p, axis=1, keepdims=True)
        out = _dot(p.astype(BF16), vbuf[:, h, :].astype(BF16))
        o_ref[:, hs] = out[0:1, :]


def _attn_sample(ya, pos, rel_bias, pool_k, pool_v, page_table, layer, topk):
    bs = ya.shape[0]
    past = page_table.shape[1] * PAGE_SIZE
    row = lambda width: pl.BlockSpec((None, 1, width), lambda b, p, pt: (b, 0, 0))
    out = pl.pallas_call(
        functools.partial(_sattn_body, layer=layer, past=past, topk=topk, thresholds=_bucket_thresholds()),
        grid_spec=pltpu.PrefetchScalarGridSpec(
            num_scalar_prefetch=2,
            grid=(bs,),
            in_specs=[row(ya.shape[1]), row(topk), pl.BlockSpec(memory_space=pltpu.SMEM),
                      pl.BlockSpec(memory_space=pl.ANY), pl.BlockSpec(memory_space=pl.ANY)],
            out_specs=row(HC * DHC),
            scratch_shapes=[
                pltpu.VMEM((topk, HC, DHC), F32),
                pltpu.VMEM((topk, HC, DHC), F32),
                pltpu.SemaphoreType.DMA((2,)),
            ],
        ),
        out_shape=jax.ShapeDtypeStruct((bs, 1, HC * DHC), F32),
        compiler_params=_cp(("arbitrary",), 32),
        name="attn_sample",
    )(pos, page_table, ya.reshape(bs, 1, -1), pos.reshape(bs, 1, topk), rel_bias, pool_k, pool_v)
    return out.reshape(bs, HC * DHC)


def _conv_step_body(yb_ref, st_ref, cw_ref, cb_ref, gn_ref, bn_ref, o_ref, st_out):
    y = yb_ref[...]
    u = y[:, 0:D_CONV] * _sigmoid(y[:, D_CONV:2 * D_CONV])
    nprev = CONV_W - 1
    acc = jnp.sum(st_ref[...] * cw_ref[0:nprev, :], axis=0, keepdims=True) + u * cw_ref[nprev:CONV_W, :] + cb_ref[...]
    mu = jnp.mean(acc, axis=1, keepdims=True)
    var = jnp.mean(jnp.square(acc - mu), axis=1, keepdims=True)
    yn = (acc - mu) * lax.rsqrt(var + EPS) * gn_ref[...] + bn_ref[...]
    o_ref[...] = yn * _sigmoid(yn)
    st_out[0:nprev - 1, :] = st_ref[1:nprev, :]
    st_out[nprev - 1:nprev, :] = u


def _conv_step(yb, state, cw, cb, gn, bn):
    bs = yb.shape[0]
    full = lambda shape: pl.BlockSpec(shape, lambda b: tuple(0 for _ in shape))
    st_spec = pl.BlockSpec((None, CONV_W - 1, D_CONV), lambda b: (b, 0, 0))
    row = lambda width: pl.BlockSpec((None, 1, width), lambda b: (b, 0, 0))
    out, st = pl.pallas_call(
        _conv_step_body,
        grid=(bs,),
        in_specs=[row(2 * D_CONV), st_spec, full((CONV_W, D_CONV)), full((1, D_CONV)), full((1, D_CONV)),
                  full((1, D_CONV))],
        out_specs=[row(D_CONV), st_spec],
        out_shape=[jax.ShapeDtypeStruct((bs, 1, D_CONV), F32),
                   jax.ShapeDtypeStruct((bs, CONV_W - 1, D_CONV), F32)],
        compiler_params=_cp(("parallel",), 32),
        name="conv_step",
    )(yb.reshape(bs, 1, 2 * D_CONV), state, cw, cb.reshape(1, D_CONV), gn.reshape(1, D_CONV),
      bn.reshape(1, D_CONV))
    return out.reshape(bs, D_CONV), st


def kernel(x_prompt, x_sample, state_mlstm_c, state_mlstm_n, state_mlstm_m, state_ret, cache_k, cache_v,
           cache_idx_k, state_conv, page_table, norm_ffn1, w_ffn1_in, w_ffn1_out, norm_mix, norm_ffn2,
           w_ffn2_in, w_ffn2_out, norm_final, w_ab_in, b_ab_gate, g_mlstm_norm, g_ret_norm, w_ab_out,
           w_cd_in, w_cd_out, rel_bias, conv_w, conv_b, g_conv_norm, b_conv_norm):
    batch, seq, d = x_prompt.shape
    bs = x_sample.shape[0]
    depth = norm_ffn1.shape[0]
    past = page_table.shape[1] * PAGE_SIZE
    hd = HC * DHC
    assert x_sample.shape[1] == 1 and seq % ATT_KC == 0 and d == D_MODEL
    table = _bias_table(rel_bias)

    def run(x, prompt):
        m = x.shape[0]
        act_dtype = BF16 if prompt else F32
        ab_states, cd_states = [], []
        for l in range(depth):
            j = l // 2
            x = _ffn(x, norm_ffn1[l], w_ffn1_in, w_ffn1_out, l)
            xn = _rmsnorm(x, norm_mix[l], act_dtype)
            if l % 2 == 0:
                (ya,) = _matmul([xn], w_ab_in, j, 0, AB_A_W, 512)
                (yg,) = _matmul([xn], w_ab_in, j, AB_A_W, LANE, LANE)
                (yb,) = _matmul([xn], w_ab_in, j, AB_B0, AB_B_W, 512)
                if prompt:
                    ma, mb, c, n, mm, s = _ab_scan(ya, yg, yb, b_ab_gate[j], g_mlstm_norm[j], g_ret_norm[j],
                                                   batch, seq)
                    mix = [ma, mb]
                else:
                    mix1, c, n, mm, s = _ab_step(ya, yg, yb, b_ab_gate[j], g_mlstm_norm[j], g_ret_norm[j],
                                                 state_mlstm_c[j], state_mlstm_n[j], state_mlstm_m[j],
                                                 state_ret[j], past)
                    mix = [mix1]
                ab_states.append((c, n[:, :, 0, :], mm[:, :, 0, 0], s))
                (x,) = _matmul(mix, w_ab_out, j, 0, d, 512, res=x)
            else:
                if prompt:
                    ya, ya16 = _matmul([xn], w_cd_in, j, 0, CD_A_W, 512, out_dtypes=(F32, BF16))
                    yg, yg16 = _matmul([xn], w_cd_in, j, CD_A_W, LANE, LANE, out_dtypes=(F32, BF16))
                else:
                    (ya,) = _matmul([xn], w_cd_in, j, 0, CD_A_W, 512)
                    (yg,) = _matmul([xn], w_cd_in, j, CD_A_W, LANE, LANE)
                (yb,) = _matmul([xn], w_cd_in, j, CD_B0, CD_B_W, 512)
                conv_args = (conv_w[j], conv_b[j], g_conv_norm[j], b_conv_norm[j])
                if prompt:
                    attn = _attn_prompt(ya16, yg, yg16, table, batch, seq)
                    conv, conv_state = _conv_prompt(yb, *conv_args, batch, seq)
                    lead = (batch, seq)
                else:
                    topk = min(TOPK_MAX, (past + 1) // 4)
                    pos = _select_sample(ya, yg, cache_idx_k, page_table, j, topk)
                    attn = _attn_sample(ya, pos, rel_bias, cache_k, cache_v, page_table, j, topk)
                    conv, conv_state = _conv_step(yb, state_conv[j], *conv_args)
                    lead = (bs, 1)
                cd_states.append((ya[:, hd:2 * hd].reshape(lead + (HC, DHC)),
                                  ya[:, 2 * hd:3 * hd].reshape(lead + (HC, DHC)),
                                  yg[:, 0:IDX_DIM].reshape(lead + (IDX_DIM,)),
                                  conv_state))
                (x,) = _matmul([attn, conv], w_cd_out, j, 0, d, 512, res=x)
            x = _ffn(x, norm_ffn2[l], w_ffn2_in, w_ffn2_out, l)
        x = _rmsnorm(x, norm_final, F32)
        return x, [jnp.stack(a) for a in zip(*ab_states)], [jnp.stack(a) for a in zip(*cd_states)]

    y_p, (pc, pn, pm, ps), (pk, pv, pik, pcv) = run(x_prompt.reshape(batch * seq, d), True)
    y_s, (sc, sn, sm, ss), (sk, sv, sik, scv) = run(x_sample.reshape(bs, d), False)
    return (y_p.reshape(batch, seq, d), y_s.reshape(bs, 1, d), pc, sc, pn, sn, pm, sm, ps, ss,
            pk, sk, pv, sv, pik, sik, pcv, scv)
```

```python
import functools
import math

import numpy as np
import jax
import jax.numpy as jnp
from jax import lax
from jax.experimental import pallas as pl
from jax.experimental.pallas import tpu as pltpu

F32 = jnp.float32
BF16 = jnp.bfloat16
I32 = jnp.int32

D_MODEL = 2048
PAGE_SIZE = 128
HA, DKA, DVA = 4, 128, 256
HB, DKB, DVB = 4, 128, 256
CHUNK = 128
ROPE_BASE = 10000.0
HC, DHC = 8, 128
IDX_HEADS, IDX_DIM = 16, 64
TOPK_MAX = 256
QBLK = 128
N_BUCKETS = 32
MAX_DISTANCE = 128
D_CONV = 1024
CONV_W = 31
EPS = 1e-6

LANE = 128
BF16_ROWS = 16
MIB = 2 ** 20
INT_MIN = -(2 ** 31)
NEG = -1e30
LOG2E = 1.0 / math.log(2.0)
ATT_QB = 256
ATT_KC = 512
ATT_D0 = ATT_KC + LANE
ATT_W = 2 * ATT_KC + LANE
CONV_HALO = 32

AB_A_W = 2 * HA * DKA + 2 * HA * DVA
AB_B0 = AB_A_W + 2 * HA
AB_B_W = 2 * HB * DKB + 2 * HB * DVB
CD_A_W = 3 * HC * DHC + IDX_HEADS * IDX_DIM
CD_B0 = CD_A_W + IDX_DIM + IDX_HEADS
CD_B_W = 2 * D_CONV


def _cp(sem, vmem_mib):
    return pltpu.CompilerParams(dimension_semantics=sem, vmem_limit_bytes=vmem_mib * MIB)


def _sigmoid(x):
    return 1.0 / (1.0 + jnp.exp(-x))


def _log_sigmoid(x):
    return jnp.minimum(x, 0.0) - jnp.log1p(jnp.exp(-jnp.abs(x)))


def _dot(a, b):
    return jnp.dot(a, b, preferred_element_type=F32)


def _dot_nt(a, b):
    return lax.dot_general(a, b, (((1,), (1,)), ((), ())), preferred_element_type=F32)


def _pad_rows(a, rows):
    if a.shape[0] >= rows:
        return a
    return jnp.concatenate([a, jnp.zeros((rows - a.shape[0], a.shape[1]), a.dtype)], axis=0)


def _rms_body(x_ref, g_ref, o_ref):
    x = x_ref[...]
    y = x * lax.rsqrt(jnp.mean(x * x, axis=-1, keepdims=True) + EPS) * g_ref[...]
    o_ref[...] = y.astype(o_ref.dtype)


def _rmsnorm(x, g, out_dtype):
    m, d = x.shape
    tm = min(m, 512)
    return pl.pallas_call(
        _rms_body,
        grid=(m // tm,),
        in_specs=[pl.BlockSpec((tm, d), lambda i: (i, 0)), pl.BlockSpec((1, d), lambda i: (0, 0))],
        out_specs=pl.BlockSpec((tm, d), lambda i: (i, 0)),
        out_shape=jax.ShapeDtypeStruct((m, d), out_dtype),
        compiler_params=_cp(("parallel",), 32),
        name="rmsnorm",
    )(x, g.reshape(1, d))


def _ffn_body(x_ref, g_ref, wa_ref, wb_ref, wo_ref, o_ref, xn_ref, *, tm):
    j = pl.program_id(1)

    @pl.when(j == 0)
    def _():
        x = x_ref[...]
        y = x * lax.rsqrt(jnp.mean(x * x, axis=-1, keepdims=True) + EPS) * g_ref[...]
        xn_ref[...] = _pad_rows(y, BF16_ROWS).astype(BF16)
        o_ref[...] = x

    xn = xn_ref[...]
    a = _dot(xn, wa_ref[...].astype(BF16))
    b = _dot(xn, wb_ref[...].astype(BF16))
    h = (a * _sigmoid(a) * b * 0.5).astype(BF16)
    o_ref[...] += _dot(h, wo_ref[...].astype(BF16))[:tm]


def _ffn(x, g, w_in, w_out, layer):
    m, d = x.shape
    f = w_out.shape[1]
    tm = min(m, 1024)
    tf = 512
    assert m % tm == 0 and f % tf == 0
    nf = f // tf
    return pl.pallas_call(
        functools.partial(_ffn_body, tm=tm),
        grid=(m // tm, nf),
        in_specs=[
            pl.BlockSpec((tm, d), lambda i, j: (i, 0), pipeline_mode=pl.Buffered(1)),
            pl.BlockSpec((1, d), lambda i, j: (0, 0)),
            pl.BlockSpec((None, d, tf), lambda i, j: (layer, 0, j)),
            pl.BlockSpec((None, d, tf), lambda i, j: (layer, 0, j + nf)),
            pl.BlockSpec((None, tf, d), lambda i, j: (layer, j, 0)),
        ],
        out_specs=pl.BlockSpec((tm, d), lambda i, j: (i, 0)),
        out_shape=jax.ShapeDtypeStruct((m, d), F32),
        scratch_shapes=[pltpu.VMEM((max(tm, BF16_ROWS), d), BF16)],
        compiler_params=_cp(("parallel", "arbitrary"), 60),
        name="ffn",
    )(x, g.reshape(1, d), w_in, w_in, w_out)


def _mm_body(*refs, k_sizes, w_transposed, has_res, n_out, tm):
    n_a = len(k_sizes)
    a_refs = refs[:n_a]
    w_ref = refs[n_a]
    pos = n_a + 1
    res_ref = None
    if has_res:
        res_ref = refs[pos]
        pos += 1
    out_refs = refs[pos:pos + n_out]
    wt_ref = refs[pos + n_out]
    rows_per_step = 64

    @pl.when(pl.program_id(1) == 0)
    def _():
        def cast_rows(c, carry):
            rows = pl.ds(pl.multiple_of(c * rows_per_step, rows_per_step), rows_per_step)
            if w_transposed:
                wt_ref[rows, :] = w_ref[0, rows, :].astype(BF16)
            else:
                wt_ref[rows, :] = w_ref[rows, :].astype(BF16)
            return carry

        lax.fori_loop(0, wt_ref.shape[0] // rows_per_step, cast_rows, 0)

    acc = None
    k0 = 0
    for a_ref, ks in zip(a_refs, k_sizes):
        a = _pad_rows(a_ref[...], BF16_ROWS).astype(BF16)
        d = _dot_nt(a, wt_ref[:, k0:k0 + ks]) if w_transposed else _dot(a, wt_ref[k0:k0 + ks, :])
        acc = d if acc is None else acc + d
        k0 += ks
    acc = acc[:tm]
    if has_res:
        acc = acc + res_ref[...]
    for o_ref in out_refs:
        o_ref[...] = acc.astype(o_ref.dtype)


def _matmul(a_parts, w, layer, col0, width, tn, res=None, out_dtypes=(F32,), w_transposed=False):
    m = a_parts[0].shape[0]
    k_sizes = tuple(a.shape[1] for a in a_parts)
    k_total = sum(k_sizes)
    assert w.shape[2 if w_transposed else 1] == k_total
    assert width % tn == 0 and tn % 64 == 0
    tm = min(m, 2048)
    assert m % tm == 0
    in_specs = [pl.BlockSpec((tm, ks), lambda j, i: (i, 0)) for ks in k_sizes]
    if w_transposed:
        assert col0 % 8 == 0
        in_specs.append(pl.BlockSpec((pl.Element(1), pl.Element(tn), pl.Element(k_total)),
                                     lambda j, i: (layer, (col0 // 8 + j * (tn // 8)) * 8, 0)))
        wt_shape = (tn, k_total)
    else:
        assert col0 % tn == 0 and tn % LANE == 0
        in_specs.append(pl.BlockSpec((None, k_total, tn), lambda j, i: (layer, 0, col0 // tn + j)))
        wt_shape = (k_total, tn)
    args = list(a_parts) + [w]
    if res is not None:
        in_specs.append(pl.BlockSpec((tm, tn), lambda j, i: (i, j)))
        args.append(res)
    outs = pl.pallas_call(
        functools.partial(_mm_body, k_sizes=k_sizes, w_transposed=w_transposed, has_res=res is not None,
                          n_out=len(out_dtypes), tm=tm),
        grid=(width // tn, m // tm),
        in_specs=in_specs,
        out_specs=[pl.BlockSpec((tm, tn), lambda j, i: (i, j)) for _ in out_dtypes],
        out_shape=[jax.ShapeDtypeStruct((m, width), dt) for dt in out_dtypes],
        scratch_shapes=[pltpu.VMEM(wt_shape, BF16)],
        compiler_params=_cp(("parallel", "arbitrary"), 56),
        name="matmul",
    )(*args)
    return outs


def _rope(x, cosf, sinf):
    return x * cosf + pltpu.roll(x, shift=x.shape[-1] // 2, axis=1) * sinf


def _ab_scan_body(sdec_ref, bg_ref, ya_ref, yg_ref, yb_ref, cos_ref, sin_ref, dmat_ref, qdec_ref, kdec_ref,
                  ga_ref, gb_ref, ma_ref, mb_ref, c_out, n_out, m_out, s_out, ct_scr, n_scr, m_scr, s_scr):
    c = pl.program_id(1)
    L = CHUNK

    @pl.when(c == 0)
    def _():
        ct_scr[...] = jnp.zeros_like(ct_scr)
        n_scr[...] = jnp.zeros_like(n_scr)
        m_scr[...] = jnp.zeros_like(m_scr)
        s_scr[...] = jnp.zeros_like(s_scr)

    g = yg_ref[...]
    g_t = g.T
    row = lax.broadcasted_iota(I32, (L, L), 0)
    col = lax.broadcasted_iota(I32, (L, L), 1)
    lower = col <= row
    cosf = cos_ref[...]
    sinf = sin_ref[...]

    for h in range(HA):
        q = ya_ref[:, h * DKA:(h + 1) * DKA]
        k = ya_ref[:, HA * DKA + h * DKA:HA * DKA + (h + 1) * DKA] * (DKA ** -0.5)
        v = ya_ref[:, 2 * HA * DKA + h * DVA:2 * HA * DKA + (h + 1) * DVA]
        o = ya_ref[:, 2 * HA * DKA + HA * DVA + h * DVA:2 * HA * DKA + HA * DVA + (h + 1) * DVA]
        b_i = bg_ref[h]
        b_f = bg_ref[HA + h]
        i_col = g[:, h:h + 1] + b_i
        f_col = _log_sigmoid(g[:, HA + h:HA + h + 1] + b_f)
        i_row = g_t[h:h + 1, :] + b_i
        f_row = _log_sigmoid(g_t[HA + h:HA + h + 1, :] + b_f)
        b_row = jnp.sum(jnp.where(row <= col, f_col, 0.0), axis=0, keepdims=True)
        b_col = jnp.sum(jnp.where(lower, f_row, 0.0), axis=1, keepdims=True)
        d_log = jnp.where(lower, b_col - b_row + i_row, -jnp.inf)
        m_prev = m_scr[h][:, 0:1]
        g_in = b_col + m_prev
        m = jnp.maximum(g_in, jnp.max(d_log, axis=1, keepdims=True))
        k_t = k.T.astype(BF16)
        q16 = q.astype(BF16)
        s = _dot(q16, k_t) * jnp.exp(d_log - m)
        inter = jnp.exp(g_in - m)
        ct = ct_scr[h]
        num = _dot(s.astype(BF16), v.astype(BF16)) + inter * _dot(q16, ct.astype(BF16))
        n_prev = n_scr[h]
        den = jnp.sum(s, axis=1, keepdims=True) + inter * jnp.sum(q * n_prev, axis=1, keepdims=True)
        hh = num / jnp.maximum(jnp.abs(den), jnp.exp(-m))
        m_new = m[L - 1:L, :]
        b_last = b_col[L - 1:L, :]
        decay = jnp.exp(b_last + m_prev - m_new)
        wk = jnp.exp(b_last - b_col + i_col - m_new)
        ct_scr[h] = decay * ct + _dot(k_t, (wk * v).astype(BF16))
        n_scr[h] = decay * n_prev + jnp.sum(wk * k, axis=0, keepdims=True)
        m_scr[h] = jnp.broadcast_to(m_new, (1, LANE))
        ha = hh * lax.rsqrt(jnp.mean(hh * hh, axis=1, keepdims=True) + EPS) * ga_ref[h]
        ma_ref[:, h * DVA:(h + 1) * DVA] = (ha * _sigmoid(o)).astype(BF16)

    for h in range(HB):
        qr = _rope(yb_ref[:, h * DKB:(h + 1) * DKB], cosf, sinf)
        kr = _rope(yb_ref[:, HB * DKB + h * DKB:HB * DKB + (h + 1) * DKB], cosf, sinf) * (DKB ** -0.5)
        vb = yb_ref[:, 2 * HB * DKB + h * DVB:2 * HB * DKB + (h + 1) * DVB]
        gate = yb_ref[:, 2 * HB * DKB + HB * DVB + h * DVB:2 * HB * DKB + HB * DVB + (h + 1) * DVB]
        kr_t = kr.T.astype(BF16)
        qr16 = qr.astype(BF16)
        a = _dot(qr16, kr_t) * dmat_ref[h]
        sp = s_scr[h]
        out = _dot(a.astype(BF16), vb.astype(BF16)) + qdec_ref[h] * _dot(qr16, sp.astype(BF16))
        s_scr[h] = sdec_ref[h] * sp + _dot(kr_t, (kdec_ref[h] * vb).astype(BF16))
        mu = jnp.mean(out, axis=1, keepdims=True)
        var = jnp.mean(jnp.square(out - mu), axis=1, keepdims=True)
        hb = (out - mu) * lax.rsqrt(var + EPS) * gb_ref[h]
        mb_ref[:, h * DVB:(h + 1) * DVB] = (hb * (gate * _sigmoid(gate))).astype(BF16)

    @pl.when(c == pl.num_programs(1) - 1)
    def _():
        for h in range(HA):
            c_out[h] = ct_scr[h].T
            n_out[h] = n_scr[h]
            m_out[h] = m_scr[h]
            s_out[h] = s_scr[h]


def _retention_consts(L):
    log_gamma = jnp.log1p(-jnp.exp2(-5.0 - jnp.arange(HB, dtype=F32)))
    j = jnp.arange(L, dtype=F32)
    diff = j[:, None] - j[None, :]
    decay_mat = jnp.where(diff >= 0, jnp.exp(log_gamma[:, None, None] * jnp.maximum(diff, 0.0)), 0.0)
    q_decay = jnp.exp(log_gamma[:, None] * (j + 1.0))
    k_decay = jnp.exp(log_gamma[:, None] * (L - 1.0 - j))
    state_decay = jnp.exp(log_gamma * L)
    return decay_mat, q_decay, k_decay, state_decay


def _rope_tables(pos):
    half = DKB // 2
    freqs = ROPE_BASE ** (-jnp.arange(half, dtype=F32) / half)
    ang = pos.astype(F32)[:, None] * freqs[None, :]
    cos, sin = jnp.cos(ang), jnp.sin(ang)
    return jnp.concatenate([cos, cos], axis=1), jnp.concatenate([-sin, sin], axis=1)


def _ab_scan(ya, yg, yb, b_gate, g_a, g_b, batch, seq):
    L = CHUNK
    nc = seq // L
    cosf, sinf = _rope_tables(jnp.arange(seq))
    dmat, qdec, kdec, sdec = _retention_consts(L)
    qdec = jnp.broadcast_to(qdec[:, :, None], (HB, L, DVB))
    kdec = jnp.broadcast_to(kdec[:, :, None], (HB, L, DVB))
    smem = pl.BlockSpec(memory_space=pltpu.SMEM)
    rows = lambda b, c: (b * nc + c, 0)
    const3 = lambda b, c: (0, 0, 0)
    state = lambda b, c: (b, 0, 0, 0)
    m = batch * seq
    return pl.pallas_call(
        _ab_scan_body,
        grid=(batch, nc),
        in_specs=[
            smem, smem,
            pl.BlockSpec((L, AB_A_W), rows),
            pl.BlockSpec((L, LANE), rows),
            pl.BlockSpec((L, AB_B_W), rows),
            pl.BlockSpec((L, DKB), lambda b, c: (c, 0)),
            pl.BlockSpec((L, DKB), lambda b, c: (c, 0)),
            pl.BlockSpec((HB, L, L), const3),
            pl.BlockSpec((HB, L, DVB), const3),
            pl.BlockSpec((HB, L, DVB), const3),
            pl.BlockSpec((HA, 1, DVA), const3),
            pl.BlockSpec((HB, 1, DVB), const3),
        ],
        out_specs=[
            pl.BlockSpec((L, HA * DVA), rows),
            pl.BlockSpec((L, HB * DVB), rows),
            pl.BlockSpec((None, HA, DVA, DKA), state),
            pl.BlockSpec((None, HA, 1, DKA), state),
            pl.BlockSpec((None, HA, 1, LANE), state),
            pl.BlockSpec((None, HB, DKB, DVB), state),
        ],
        out_shape=[
            jax.ShapeDtypeStruct((m, HA * DVA), BF16),
            jax.ShapeDtypeStruct((m, HB * DVB), BF16),
            jax.ShapeDtypeStruct((batch, HA, DVA, DKA), F32),
            jax.ShapeDtypeStruct((batch, HA, 1, DKA), F32),
            jax.ShapeDtypeStruct((batch, HA, 1, LANE), F32),
            jax.ShapeDtypeStruct((batch, HB, DKB, DVB), F32),
        ],
        scratch_shapes=[
            pltpu.VMEM((HA, DKA, DVA), F32),
            pltpu.VMEM((HA, 1, DKA), F32),
            pltpu.VMEM((HA, 1, LANE), F32),
            pltpu.VMEM((HB, DKB, DVB), F32),
        ],
        compiler_params=_cp(("parallel", "arbitrary"), 40),
        name="ab_scan",
    )(sdec, b_gate, ya, yg, yb, cosf, sinf, dmat, qdec, kdec,
      g_a.reshape(HA, 1, DVA), g_b.reshape(HB, 1, DVB))


def _lane_bcast_col(row_vec):
    n = row_vec.shape[1]
    return jnp.broadcast_to(row_vec, (LANE, n)).T


def _ab_step_body(sdec_ref, bg_ref, ya_ref, yg_ref, yb_ref, cos_ref, sin_ref, ga_ref, gb_ref,
                  c_ref, n_ref, m_ref, s_ref, mix_ref, c_out, n_out, m_out, s_out):
    ya = ya_ref[...]
    yg = yg_ref[...]
    yb = yb_ref[...]
    cosf = cos_ref[...]
    sinf = sin_ref[...]

    for h in range(HA):
        q = ya[:, h * DKA:(h + 1) * DKA]
        k = ya[:, HA * DKA + h * DKA:HA * DKA + (h + 1) * DKA] * (DKA ** -0.5)
        v = ya[:, 2 * HA * DKA + h * DVA:2 * HA * DKA + (h + 1) * DVA]
        o = ya[:, 2 * HA * DKA + HA * DVA + h * DVA:2 * HA * DKA + HA * DVA + (h + 1) * DVA]
        i_pre = yg[:, h:h + 1] + bg_ref[h]
        log_f = _log_sigmoid(yg[:, HA + h:HA + h + 1] + bg_ref[HA + h])
        m_prev = m_ref[h][:, 0:1]
        g_in = log_f + m_prev
        m = jnp.maximum(g_in, i_pre)
        w_in = jnp.exp(i_pre - m)
        inter = jnp.exp(g_in - m)
        s = jnp.sum(q * k, axis=1, keepdims=True) * w_in
        cmat = c_ref[h]
        cq = jnp.sum(cmat * q, axis=1, keepdims=True)
        cq_row = jnp.broadcast_to(cq, (DVA, LANE)).T[0:1, :]
        num = s * v + inter * cq_row
        n_prev = n_ref[h]
        den = s + inter * jnp.sum(n_prev * q, axis=1, keepdims=True)
        hrow = num / jnp.maximum(jnp.abs(den), jnp.exp(-m))
        c_out[h] = inter * cmat + (w_in * _lane_bcast_col(v)) * k
        n_out[h] = inter * n_prev + w_in * k
        m_out[h] = jnp.broadcast_to(m, (1, LANE))
        ha = hrow * lax.rsqrt(jnp.mean(hrow * hrow, axis=1, keepdims=True) + EPS) * ga_ref[h]
        mix_ref[:, h * DVA:(h + 1) * DVA] = ha * _sigmoid(o)

    base = HA * DVA
    for h in range(HB):
        qr = _rope(yb[:, h * DKB:(h + 1) * DKB], cosf, sinf)
        kr = _rope(yb[:, HB * DKB + h * DKB:HB * DKB + (h + 1) * DKB], cosf, sinf) * (DKB ** -0.5)
        vb = yb[:, 2 * HB * DKB + h * DVB:2 * HB * DKB + (h + 1) * DVB]
        gate = yb[:, 2 * HB * DKB + HB * DVB + h * DVB:2 * HB * DKB + HB * DVB + (h + 1) * DVB]
        gamma = sdec_ref[h]
        smat = s_ref[h]
        q_col = _lane_bcast_col(qr)
        k_col = _lane_bcast_col(kr)
        qs = jnp.concatenate([jnp.sum(q_col * smat[:, t * LANE:(t + 1) * LANE], axis=0, keepdims=True)
                              for t in range(DVB // LANE)], axis=1)
        out = jnp.sum(qr * kr, axis=1, keepdims=True) * vb + gamma * qs
        s_out[h] = gamma * smat + jnp.concatenate([k_col * vb[:, t * LANE:(t + 1) * LANE]
                                                   for t in range(DVB // LANE)], axis=1)
        mu = jnp.mean(out, axis=1, keepdims=True)
        var = jnp.mean(jnp.square(out - mu), axis=1, keepdims=True)
        hb = (out - mu) * lax.rsqrt(var + EPS) * gb_ref[h]
        mix_ref[:, base + h * DVB:base + (h + 1) * DVB] = hb * (gate * _sigmoid(gate))


def _ab_step(ya, yg, yb, b_gate, g_a, g_b, c0, n0, m0, s0, pos0):
    bs = ya.shape[0]
    cosf, sinf = _rope_tables(pos0 + jnp.arange(1))
    _, _, _, sdec = _retention_consts(1)
    smem = pl.BlockSpec(memory_space=pltpu.SMEM)
    full = lambda shape: pl.BlockSpec(shape, lambda b: tuple(0 for _ in shape))
    row = lambda width: pl.BlockSpec((None, 1, width), lambda b: (b, 0, 0))
    state = lambda b: (b, 0, 0, 0)
    mix_w = HA * DVA + HB * DVB
    outs = pl.pallas_call(
        _ab_step_body,
        grid=(bs,),
        in_specs=[
            smem, smem,
            row(AB_A_W), row(LANE), row(AB_B_W),
            full((1, DKB)), full((1, DKB)),
            full((HA, 1, DVA)), full((HB, 1, DVB)),
            pl.BlockSpec((None, HA, DVA, DKA), state),
            pl.BlockSpec((None, HA, 1, DKA), state),
            pl.BlockSpec((None, HA, 1, LANE), state),
            pl.BlockSpec((None, HB, DKB, DVB), state),
        ],
        out_specs=[
            row(mix_w),
            pl.BlockSpec((None, HA, DVA, DKA), state),
            pl.BlockSpec((None, HA, 1, DKA), state),
            pl.BlockSpec((None, HA, 1, LANE), state),
            pl.BlockSpec((None, HB, DKB, DVB), state),
        ],
        out_shape=[
            jax.ShapeDtypeStruct((bs, 1, mix_w), F32),
            jax.ShapeDtypeStruct((bs, HA, DVA, DKA), F32),
            jax.ShapeDtypeStruct((bs, HA, 1, DKA), F32),
            jax.ShapeDtypeStruct((bs, HA, 1, LANE), F32),
            jax.ShapeDtypeStruct((bs, HB, DKB, DVB), F32),
        ],
        compiler_params=_cp(("parallel",), 32),
        name="ab_step",
    )(sdec, b_gate, ya.reshape(bs, 1, AB_A_W), yg.reshape(bs, 1, LANE), yb.reshape(bs, 1, AB_B_W), cosf, sinf,
      g_a.reshape(HA, 1, DVA), g_b.reshape(HB, 1, DVB),
      c0, n0.reshape(bs, HA, 1, DKA), jnp.broadcast_to(m0[:, :, None, None], (bs, HA, 1, LANE)), s0)
    return (outs[0].reshape(bs, mix_w),) + tuple(outs[1:])


def _bucket_np(dist):
    exact = N_BUCKETS // 2
    dist = np.maximum(dist, 0)
    ratio = np.maximum(dist, 1).astype(np.float32) / np.float32(exact)
    log_ratio = np.log(ratio) / np.float32(math.log(MAX_DISTANCE / exact))
    large = np.minimum(exact + (log_ratio * (N_BUCKETS - exact)).astype(np.int32), N_BUCKETS - 1)
    return np.where(dist < exact, dist, large).astype(np.int32)


def _bucket_thresholds():
    table = _bucket_np(np.arange(4 * MAX_DISTANCE))
    return [int(np.argmax(table >= k)) for k in range(1, N_BUCKETS)]


def _bias_table_body(rb_ref, idx_ref, o_ref):
    idx = idx_ref[...]
    for h in range(HC):
        acc = jnp.zeros(idx.shape, F32)
        for bkt in range(N_BUCKETS):
            acc = jnp.where(idx == bkt, rb_ref[bkt, h] * LOG2E, acc)
        o_ref[h] = acc


def _bias_table(rel_bias):
    t = np.arange(ATT_QB)[None, :]
    w = np.arange(ATT_W)[:, None]
    idx = _bucket_np(t + ATT_D0 - w)
    return pl.pallas_call(
        _bias_table_body,
        in_specs=[pl.BlockSpec(memory_space=pltpu.SMEM), pl.BlockSpec((ATT_W, ATT_QB), lambda: (0, 0))],
        out_specs=pl.BlockSpec((HC, ATT_W, ATT_QB), lambda: (0, 0, 0)),
        out_shape=jax.ShapeDtypeStruct((HC, ATT_W, ATT_QB), F32),
        name="bias_table",
    )(rel_bias, jnp.asarray(idx))


def _to_key(x):
    bits = lax.bitcast_convert_type(x, I32)
    return bits ^ ((bits >> 31) & jnp.int32(0x7FFFFFFF))


def _fold_rows(x, op):
    chains = 4
    acc = [x[r * 8:(r + 1) * 8, :] for r in range(chains)]
    for r in range(chains, x.shape[0] // 8):
        acc[r % chains] = op(acc[r % chains], x[r * 8:(r + 1) * 8, :])
    return op(op(acc[0], acc[1]), op(acc[2], acc[3]))


def _attn_body(q_ref, qi_ref, wi_ref, ki_ref, k_ref, vt_ref, tab_ref, o_ref,
               key_scr, thr_scr, mask_scr, m_scr, l_scr, acc_scr, *, topk):
    qb = pl.program_id(1)
    kc = pl.program_id(2)
    nq, kcs = ATT_QB, ATT_KC
    last = (qb * nq) // kcs
    sub = LANE

    @pl.when(kc == 0)
    def _():
        wi_t = wi_ref[...].T
        t_idx = qb * nq + lax.broadcasted_iota(I32, (sub, nq), 1)

        def score(c, carry):
            base = pl.multiple_of(c * kcs, kcs)
            for sb in range(kcs // sub):
                ki = ki_ref[pl.ds(base + sb * sub, sub), :][:, 0:IDX_DIM]
                acc = jnp.zeros((sub, nq), F32)
                for h in range(IDX_HEADS):
                    lg = _dot_nt(ki, qi_ref[:, h * IDX_DIM:(h + 1) * IDX_DIM])
                    acc = acc + wi_t[IDX_DIM + h:IDX_DIM + h + 1, :] * jnp.maximum(lg, 0.0)
                s_idx = base + sb * sub + lax.broadcasted_iota(I32, (sub, nq), 0)
                key_scr[c, sb * sub:(sb + 1) * sub, :] = jnp.where(s_idx <= t_idx, _to_key(acc + 0.0),
                                                                   jnp.int32(INT_MIN))
            return carry

        lax.fori_loop(0, last + 1, score, 0)

        def bisect(i, thr):
            cand = thr + lax.shift_left(jnp.int32(1), 31 - i)

            def count(c, a):
                return a + _fold_rows(jnp.where(key_scr[c] >= cand, 1.0, 0.0), jnp.add)

            a = lax.fori_loop(0, last + 1, count, jnp.zeros((8, nq), F32))
            n = jnp.sum(a, axis=0, keepdims=True)
            return jnp.where(n >= topk, cand, thr)

        thr = lax.fori_loop(0, 32, bisect, jnp.full((1, nq), INT_MIN, I32))
        thr_scr[...] = jnp.maximum(thr, jnp.int32(INT_MIN + 1))
        m_scr[...] = jnp.full_like(m_scr, NEG)
        l_scr[...] = jnp.zeros_like(l_scr)
        acc_scr[...] = jnp.zeros_like(acc_scr)

    @pl.when(kc <= last)
    def _():
        mask_scr[...] = jnp.where(key_scr[kc] >= thr_scr[...], 0.0, NEG)
        w0 = pl.multiple_of(jnp.maximum(ATT_D0 - qb * nq + kc * kcs, 0), LANE)
        for h in range(HC):
            hs = slice(h * DHC, (h + 1) * DHC)
            lg = (_dot_nt(k_ref[:, hs], q_ref[:, hs]) * (DHC ** -0.5 * LOG2E) + tab_ref[h, pl.ds(w0, kcs), :]
                  + mask_scr[...])
            m_old = m_scr[h]
            m_new = jnp.maximum(m_old, jnp.max(_fold_rows(lg, jnp.maximum), axis=0, keepdims=True))
            alpha = jnp.exp2(m_old - m_new)
            p = jnp.exp2(lg - m_new)
            l_scr[h] = alpha * l_scr[h] + jnp.sum(_fold_rows(p, jnp.add), axis=0, keepdims=True)
            acc_scr[hs, :] = alpha * acc_scr[hs, :] + _dot(vt_ref[hs, :], p.astype(BF16))
            m_scr[h] = m_new

    @pl.when(kc == last)
    def _():
        for h in range(HC):
            hs = slice(h * DHC, (h + 1) * DHC)
            o_ref[:, hs] = (acc_scr[hs, :] / l_scr[h]).T.astype(o_ref.dtype)


def _transpose_body(x_ref, o_ref):
    o_ref[...] = x_ref[...].T


def _transpose_v(ya16, batch, seq):
    hd = HC * DHC
    tt = 512
    nt = seq // tt
    return pl.pallas_call(
        _transpose_body,
        grid=(batch, nt),
        in_specs=[pl.BlockSpec((tt, hd), lambda b, t: (b * nt + t, 2))],
        out_specs=pl.BlockSpec((None, hd, tt), lambda b, t: (b, 0, t)),
        out_shape=jax.ShapeDtypeStruct((batch, hd, seq), BF16),
        compiler_params=_cp(("parallel", "parallel"), 32),
        name="transpose_v",
    )(ya16)


def _attn_prompt(ya16, yg, yg16, vt16, table, batch, seq):
    nqb = seq // ATT_QB
    nkc = seq // ATT_KC
    topk = min(TOPK_MAX, seq // 4)
    kv_blk = lambda qb, kc: jnp.minimum(kc, (qb * ATT_QB) // ATT_KC)
    q_row = lambda b, qb, kc: b * nqb + qb
    hd = HC * DHC
    return pl.pallas_call(
        functools.partial(_attn_body, topk=topk),
        grid=(batch, nqb, nkc),
        in_specs=[
            pl.BlockSpec((ATT_QB, hd), lambda b, qb, kc: (q_row(b, qb, kc), 0)),
            pl.BlockSpec((ATT_QB, IDX_HEADS * IDX_DIM), lambda b, qb, kc: (q_row(b, qb, kc), 3)),
            pl.BlockSpec((ATT_QB, LANE), lambda b, qb, kc: (q_row(b, qb, kc), 0)),
            pl.BlockSpec((seq, LANE), lambda b, qb, kc: (b, 0)),
            pl.BlockSpec((ATT_KC, hd), lambda b, qb, kc: (b * nkc + kv_blk(qb, kc), 1)),
            pl.BlockSpec((None, hd, ATT_KC), lambda b, qb, kc: (b, 0, kv_blk(qb, kc))),
            pl.BlockSpec((HC, ATT_W, ATT_QB), lambda b, qb, kc: (0, 0, 0)),
        ],
        out_specs=pl.BlockSpec((ATT_QB, hd), lambda b, qb, kc: (q_row(b, qb, kc), 0)),
        out_shape=jax.ShapeDtypeStruct((batch * seq, hd), BF16),
        scratch_shapes=[
            pltpu.VMEM((nkc, ATT_KC, ATT_QB), I32),
            pltpu.VMEM((1, ATT_QB), I32),
            pltpu.VMEM((ATT_KC, ATT_QB), F32),
            pltpu.VMEM((HC, 1, ATT_QB), F32),
            pltpu.VMEM((HC, 1, ATT_QB), F32),
            pltpu.VMEM((hd, ATT_QB), F32),
        ],
        compiler_params=_cp(("parallel", "arbitrary", "arbitrary"), 56),
        name="attn_prompt",
    )(ya16, ya16, yg, yg16, ya16, vt16, table)


def _conv_body(yb_ref, cw_ref, cb_ref, gn_ref, bn_ref, o_ref, st_ref, ubuf, *, tt):
    t = pl.program_id(1)
    halo = CONV_HALO
    rb = 32

    @pl.when(t == 0)
    def _():
        ubuf[0:halo, :] = jnp.zeros((halo, D_CONV), F32)

    @pl.when(t > 0)
    def _():
        ubuf[0:halo, :] = ubuf[tt:tt + halo, :]

    ubuf[halo:halo + tt, :] = yb_ref[:, 0:D_CONV] * _sigmoid(yb_ref[:, D_CONV:2 * D_CONV])
    for r in range(tt // rb):
        acc = jnp.broadcast_to(cb_ref[...], (rb, D_CONV))
        for w in range(CONV_W):
            off = halo - (CONV_W - 1) + w + r * rb
            acc = acc + ubuf[off:off + rb, :] * cw_ref[w:w + 1, :]
        mu = jnp.mean(acc, axis=1, keepdims=True)
        var = jnp.mean(jnp.square(acc - mu), axis=1, keepdims=True)
        yn = (acc - mu) * lax.rsqrt(var + EPS) * gn_ref[...] + bn_ref[...]
        o_ref[r * rb:(r + 1) * rb, :] = (yn * _sigmoid(yn)).astype(o_ref.dtype)

    @pl.when(t == pl.num_programs(1) - 1)
    def _():
        st_ref[...] = ubuf[halo + tt - (CONV_W - 1):halo + tt, :]


def _conv_prompt(yb, cw, cb, gn, bn, batch, seq):
    tt = 256
    nt = seq // tt
    const = lambda b, t: (0, 0)
    return pl.pallas_call(
        functools.partial(_conv_body, tt=tt),
        grid=(batch, nt),
        in_specs=[
            pl.BlockSpec((tt, 2 * D_CONV), lambda b, t: (b * nt + t, 0)),
            pl.BlockSpec((CONV_W, D_CONV), const),
            pl.BlockSpec((1, D_CONV), const),
            pl.BlockSpec((1, D_CONV), const),
            pl.BlockSpec((1, D_CONV), const),
        ],
        out_specs=[
            pl.BlockSpec((tt, D_CONV), lambda b, t: (b * nt + t, 0)),
            pl.BlockSpec((None, CONV_W - 1, D_CONV), lambda b, t: (b, 0, 0)),
        ],
        out_shape=[
            jax.ShapeDtypeStruct((batch * seq, D_CONV), BF16),
            jax.ShapeDtypeStruct((batch, CONV_W - 1, D_CONV), F32),
        ],
        scratch_shapes=[pltpu.VMEM((CONV_HALO + tt, D_CONV), F32)],
        compiler_params=_cp(("parallel", "arbitrary"), 32),
        name="conv_prompt",
    )(yb, cw, cb.reshape(1, D_CONV), gn.reshape(1, D_CONV), bn.reshape(1, D_CONV))


def _rows_to_sublanes(row_vec, n, width, start=0):
    return jnp.concatenate([row_vec[:, start + i * width:start + (i + 1) * width] for i in range(n)], axis=0)


def _select_body(pt_ref, ya_ref, yg_ref, pool_ref, pos_ref, pbuf, sc_scr, slot_scr, sem, *, layer, n_pages, topk):
    b = pl.program_id(0)
    group = 16

    def page_copy(p):
        return pltpu.make_async_copy(pool_ref.at[layer, pt_ref[b, p]], pbuf.at[p], sem.at[0])

    def start(p, carry):
        page_copy(p).start()
        return carry

    def wait(p, carry):
        page_copy(p).wait()
        return carry

    lax.fori_loop(0, n_pages, start, 0)
    lax.fori_loop(0, n_pages, wait, 0)

    q_row = ya_ref[...]
    g_row = yg_ref[...]
    qi = _rows_to_sublanes(q_row, IDX_HEADS, IDX_DIM, start=3 * HC * DHC).astype(BF16)
    wi = _rows_to_sublanes(g_row, IDX_HEADS, 1, start=IDX_DIM)
    ki_new = g_row[:, 0:IDX_DIM].astype(BF16)

    for g in range(n_pages // group):
        keys = jnp.concatenate([pbuf[g * group + p] for p in range(group)], axis=1).astype(BF16)
        lg = _dot(qi, keys)
        sc = jnp.sum(jnp.maximum(lg, 0.0) * wi, axis=0, keepdims=True)
        for p in range(group):
            sc_scr[g * group + p:g * group + p + 1, :] = sc[:, p * PAGE_SIZE:(p + 1) * PAGE_SIZE]
    lg_new = jnp.sum(qi.astype(F32) * ki_new.astype(F32), axis=1, keepdims=True)
    sc_new = jnp.sum(jnp.maximum(lg_new, 0.0) * wi, axis=0, keepdims=True)
    key = _to_key(sc_scr[...] + 0.0)
    key_new = _to_key(sc_new + 0.0)

    def count(mask, mask_new):
        return jnp.sum(jnp.sum(jnp.where(mask, 1.0, 0.0), axis=1, keepdims=True), axis=0, keepdims=True) \
            + jnp.where(mask_new, 1.0, 0.0)

    def bisect(i, thr):
        cand = thr + lax.shift_left(jnp.int32(1), 31 - i)
        return jnp.where(count(key >= cand, key_new >= cand) >= topk, cand, thr)

    thr = lax.fori_loop(0, 32, bisect, jnp.full((1, 1), INT_MIN, I32))

    r_i = lax.broadcasted_iota(I32, (PAGE_SIZE, PAGE_SIZE), 0)
    c_i = lax.broadcasted_iota(I32, (PAGE_SIZE, PAGE_SIZE), 1)
    upper = jnp.where(r_i <= c_i, 1.0, 0.0).astype(BF16)
    pr = lax.broadcasted_iota(I32, (n_pages, n_pages), 0)
    pc = lax.broadcasted_iota(I32, (n_pages, n_pages), 1)
    before = jnp.where(pc < pr, 1.0, 0.0).astype(BF16)

    def rank(mask):
        inc = _dot(jnp.where(mask, 1.0, 0.0).astype(BF16), upper)
        tot = jnp.broadcast_to(inc[:, PAGE_SIZE - 1:PAGE_SIZE], (n_pages, PAGE_SIZE))
        return inc + _dot(before, tot.astype(BF16))

    gt = key > thr
    eq = key == thr
    n_gt = count(gt, key_new > thr)
    slot_eq = n_gt + rank(eq) - 1.0
    slot = jnp.where(gt, rank(gt) - 1.0, jnp.where(eq & (slot_eq < topk), slot_eq, -1.0))
    slot_scr[...] = slot

    r_iota = lax.broadcasted_iota(I32, (topk, PAGE_SIZE), 0).astype(F32)
    lane = lax.broadcasted_iota(I32, (1, PAGE_SIZE), 1).astype(F32)

    def compact(p, acc):
        val = lax.convert_element_type(p * PAGE_SIZE + 1, F32) + lane
        return acc + jnp.where(slot_scr[pl.ds(p, 1), :] == r_iota, val, 0.0)

    acc = lax.fori_loop(0, n_pages, compact, jnp.zeros((topk, PAGE_SIZE), F32))
    pos = jnp.sum(acc, axis=1, keepdims=True) - 1.0
    pos_ref[...] = jnp.broadcast_to(pos, (topk, PAGE_SIZE)).astype(I32)


def _select_sample(ya, yg, pool_idx, page_table, layer, topk):
    bs = ya.shape[0]
    n_pages = page_table.shape[1]
    row = lambda width: pl.BlockSpec((None, 1, width), lambda b, pt: (b, 0, 0))
    out = pl.pallas_call(
        functools.partial(_select_body, layer=layer, n_pages=n_pages, topk=topk),
        grid_spec=pltpu.PrefetchScalarGridSpec(
            num_scalar_prefetch=1,
            grid=(bs,),
            in_specs=[row(ya.shape[1]), row(yg.shape[1]), pl.BlockSpec(memory_space=pl.ANY)],
            out_specs=pl.BlockSpec((None, topk, PAGE_SIZE), lambda b, pt: (b, 0, 0)),
            scratch_shapes=[
                pltpu.VMEM((n_pages, IDX_DIM, PAGE_SIZE), F32),
                pltpu.VMEM((n_pages, PAGE_SIZE), F32),
                pltpu.VMEM((n_pages, PAGE_SIZE), F32),
                pltpu.SemaphoreType.DMA((1,)),
            ],
        ),
        out_shape=jax.ShapeDtypeStruct((bs, topk, PAGE_SIZE), I32),
        compiler_params=_cp(("arbitrary",), 40),
        name="select_sample",
    )(page_table, ya.reshape(bs, 1, -1), yg.reshape(bs, 1, -1), pool_idx)
    return out[:, :, 0]


def _sattn_body(pos_ref, pt_ref, ya_ref, posv_ref, rb_ref, kpool, vpool, o_ref, kbuf, vbuf, sem,
                *, layer, past, topk, thresholds):
    b = pl.program_id(0)

    def row_copies(r):
        pos = jnp.maximum(pos_ref[b, r], 0)
        page = pt_ref[b, pos // PAGE_SIZE]
        off = pos % PAGE_SIZE
        return (pltpu.make_async_copy(kpool.at[layer, page, off], kbuf.at[r], sem.at[0]),
                pltpu.make_async_copy(vpool.at[layer, page, off], vbuf.at[r], sem.at[1]))

    def start(r, carry):
        @pl.when(pos_ref[b, r] >= 0)
        def _():
            ck, cv = row_copies(r)
            ck.start()
            cv.start()
        return carry

    lax.fori_loop(0, topk, start, 0)

    y_row = ya_ref[...]
    k_new = _rows_to_sublanes(y_row, HC, DHC, start=HC * DHC)
    v_new = _rows_to_sublanes(y_row, HC, DHC, start=2 * HC * DHC)

    def wait(r, carry):
        @pl.when(pos_ref[b, r] >= 0)
        def _():
            ck, cv = row_copies(r)
            ck.wait()
            cv.wait()

        @pl.when(pos_ref[b, r] < 0)
        def _():
            kbuf[r] = k_new
            vbuf[r] = v_new
        return carry

    lax.fori_loop(0, topk, wait, 0)

    posv = posv_ref[...]
    dist = jnp.where(posv < 0, 0, past - posv)
    bucket = jnp.zeros(dist.shape, I32)
    for thr in thresholds:
        bucket = bucket + jnp.where(dist >= thr, 1, 0)
    for h in range(HC):
        hs = slice(h * DHC, (h + 1) * DHC)
        bias = jnp.zeros(dist.shape, F32)
        for bkt in range(N_BUCKETS):
            bias = jnp.where(bucket == bkt, rb_ref[bkt, h], bias)
        q16 = jnp.broadcast_to(y_row[:, hs], (BF16_ROWS, DHC)).astype(BF16)
        lg = _dot_nt(q16, kbuf[:, h, :].astype(BF16)) * (DHC ** -0.5) + bias
        p = jnp.exp(lg - jnp.max(lg, axis=1, keepdims=True))
        p = p / jnp.sum(p, axis=1, keepdims=True)
        out = _dot(p.astype(BF16), vbuf[:, h, :].astype(BF16))
        o_ref[:, hs] = out[0:1, :]


def _attn_sample(ya, pos, rel_bias, pool_k, pool_v, page_table, layer, topk):
    bs = ya.shape[0]
    past = page_table.shape[1] * PAGE_SIZE
    row = lambda width: pl.BlockSpec((None, 1, width), lambda b, p, pt: (b, 0, 0))
    out = pl.pallas_call(
        functools.partial(_sattn_body, layer=layer, past=past, topk=topk, thresholds=_bucket_thresholds()),
        grid_spec=pltpu.PrefetchScalarGridSpec(
            num_scalar_prefetch=2,
            grid=(bs,),
            in_specs=[row(ya.shape[1]), row(topk), pl.BlockSpec(memory_space=pltpu.SMEM),
                      pl.BlockSpec(memory_space=pl.ANY), pl.BlockSpec(memory_space=pl.ANY)],
            out_specs=row(HC * DHC),
            scratch_shapes=[
                pltpu.VMEM((topk, HC, DHC), F32),
                pltpu.VMEM((topk, HC, DHC), F32),
                pltpu.SemaphoreType.DMA((2,)),
            ],
        ),
        out_shape=jax.ShapeDtypeStruct((bs, 1, HC * DHC), F32),
        compiler_params=_cp(("arbitrary",), 32),
        name="attn_sample",
    )(pos, page_table, ya.reshape(bs, 1, -1), pos.reshape(bs, 1, topk), rel_bias, pool_k, pool_v)
    return out.reshape(bs, HC * DHC)


def _conv_step_body(yb_ref, st_ref, cw_ref, cb_ref, gn_ref, bn_ref, o_ref, st_out):
    y = yb_ref[...]
    u = y[:, 0:D_CONV] * _sigmoid(y[:, D_CONV:2 * D_CONV])
    nprev = CONV_W - 1
    acc = jnp.sum(st_ref[...] * cw_ref[0:nprev, :], axis=0, keepdims=True) + u * cw_ref[nprev:CONV_W, :] + cb_ref[...]
    mu = jnp.mean(acc, axis=1, keepdims=True)
    var = jnp.mean(jnp.square(acc - mu), axis=1, keepdims=True)
    yn = (acc - mu) * lax.rsqrt(var + EPS) * gn_ref[...] + bn_ref[...]
    o_ref[...] = yn * _sigmoid(yn)
    st_out[0:nprev - 1, :] = st_ref[1:nprev, :]
    st_out[nprev - 1:nprev, :] = u


def _conv_step(yb, state, cw, cb, gn, bn):
    bs = yb.shape[0]
    full = lambda shape: pl.BlockSpec(shape, lambda b: tuple(0 for _ in shape))
    st_spec = pl.BlockSpec((None, CONV_W - 1, D_CONV), lambda b: (b, 0, 0))
    row = lambda width: pl.BlockSpec((None, 1, width), lambda b: (b, 0, 0))
    out, st = pl.pallas_call(
        _conv_step_body,
        grid=(bs,),
        in_specs=[row(2 * D_CONV), st_spec, full((CONV_W, D_CONV)), full((1, D_CONV)), full((1, D_CONV)),
                  full((1, D_CONV))],
        out_specs=[row(D_CONV), st_spec],
        out_shape=[jax.ShapeDtypeStruct((bs, 1, D_CONV), F32),
                   jax.ShapeDtypeStruct((bs, CONV_W - 1, D_CONV), F32)],
        compiler_params=_cp(("parallel",), 32),
        name="conv_step",
    )(yb.reshape(bs, 1, 2 * D_CONV), state, cw, cb.reshape(1, D_CONV), gn.reshape(1, D_CONV),
      bn.reshape(1, D_CONV))
    return out.reshape(bs, D_CONV), st


def kernel(x_prompt, x_sample, state_mlstm_c, state_mlstm_n, state_mlstm_m, state_ret, cache_k, cache_v,
           cache_idx_k, state_conv, page_table, norm_ffn1, w_ffn1_in, w_ffn1_out, norm_mix, norm_ffn2,
           w_ffn2_in, w_ffn2_out, norm_final, w_ab_in, b_ab_gate, g_mlstm_norm, g_ret_norm, w_ab_out,
           w_cd_in, w_cd_out, rel_bias, conv_w, conv_b, g_conv_norm, b_conv_norm):
    batch, seq, d = x_prompt.shape
    bs = x_sample.shape[0]
    depth = norm_ffn1.shape[0]
    past = page_table.shape[1] * PAGE_SIZE
    hd = HC * DHC
    assert x_sample.shape[1] == 1 and seq % ATT_KC == 0 and d == D_MODEL
    table = _bias_table(rel_bias)
    w_ab_in_t = jnp.swapaxes(w_ab_in, 1, 2)
    w_cd_in_t = jnp.swapaxes(w_cd_in, 1, 2)
    pool_idx_t = jnp.swapaxes(cache_idx_k, 2, 3)
    in_proj = functools.partial(_matmul, w_transposed=True)

    def run(x, prompt):
        m = x.shape[0]
        act_dtype = BF16 if prompt else F32
        ab_states, cd_states = [], []
        for l in range(depth):
            j = l // 2
            x = _ffn(x, norm_ffn1[l], w_ffn1_in, w_ffn1_out, l)
            xn = _rmsnorm(x, norm_mix[l], act_dtype)
            if l % 2 == 0:
                (ya,) = in_proj([xn], w_ab_in_t, j, 0, AB_A_W, 512)
                (yg,) = in_proj([xn], w_ab_in_t, j, AB_A_W, LANE, LANE)
                (yb,) = in_proj([xn], w_ab_in_t, j, AB_B0, AB_B_W, 512)
                if prompt:
                    ma, mb, c, n, mm, s = _ab_scan(ya, yg, yb, b_ab_gate[j], g_mlstm_norm[j], g_ret_norm[j],
                                                   batch, seq)
                    mix = [ma, mb]
                else:
                    mix1, c, n, mm, s = _ab_step(ya, yg, yb, b_ab_gate[j], g_mlstm_norm[j], g_ret_norm[j],
                                                 state_mlstm_c[j], state_mlstm_n[j], state_mlstm_m[j],
                                                 state_ret[j], past)
                    mix = [mix1]
                ab_states.append((c, n[:, :, 0, :], mm[:, :, 0, 0], s))
                (x,) = _matmul(mix, w_ab_out, j, 0, d, 512, res=x)
            else:
                if prompt:
                    ya, ya16 = in_proj([xn], w_cd_in_t, j, 0, CD_A_W, 512, out_dtypes=(F32, BF16))
                    yg, yg16 = in_proj([xn], w_cd_in_t, j, CD_A_W, LANE, LANE, out_dtypes=(F32, BF16))
                else:
                    (ya,) = in_proj([xn], w_cd_in_t, j, 0, CD_A_W, 512)
                    (yg,) = in_proj([xn], w_cd_in_t, j, CD_A_W, LANE, LANE)
                (yb,) = in_proj([xn], w_cd_in_t, j, CD_B0, CD_B_W, 512)
                conv_args = (conv_w[j], conv_b[j], g_conv_norm[j], b_conv_norm[j])
                if prompt:
                    attn = _attn_prompt(ya16, yg, yg16, _transpose_v(ya16, batch, seq), table, batch, seq)
                    conv, conv_state = _conv_prompt(yb, *conv_args, batch, seq)
                    lead = (batch, seq)
                else:
                    topk = min(TOPK_MAX, (past + 1) // 4)
                    pos = _select_sample(ya, yg, pool_idx_t, page_table, j, topk)
                    attn = _attn_sample(ya, pos, rel_bias, cache_k, cache_v, page_table, j, topk)
                    conv, conv_state = _conv_step(yb, state_conv[j], *conv_args)
                    lead = (bs, 1)
                cd_states.append((ya[:, hd:2 * hd].reshape(lead + (HC, DHC)),
                                  ya[:, 2 * hd:3 * hd].reshape(lead + (HC, DHC)),
                                  yg[:, 0:IDX_DIM].reshape(lead + (IDX_DIM,)),
                                  conv_state))
                (x,) = _matmul([attn, conv], w_cd_out, j, 0, d, 512, res=x)
            x = _ffn(x, norm_ffn2[l], w_ffn2_in, w_ffn2_out, l)
        x = _rmsnorm(x, norm_final, F32)
        return x, [jnp.stack(a) for a in zip(*ab_states)], [jnp.stack(a) for a in zip(*cd_states)]

    y_p, (pc, pn, pm, ps), (pk, pv, pik, pcv) = run(x_prompt.reshape(batch * seq, d), True)
    y_s, (sc, sn, sm, ss), (sk, sv, sik, scv) = run(x_sample.reshape(bs, d), False)
    return (y_p.reshape(batch, seq, d), y_s.reshape(bs, 1, d), pc, sc, pn, sn, pm, sm, ps, ss,
            pk, sk, pv, sv, pik, sik, pcv, scv)
```

```python
import functools
import math

import numpy as np
import jax
import jax.numpy as jnp
from jax import lax
from jax.experimental import pallas as pl
from jax.experimental.pallas import tpu as pltpu

F32 = jnp.float32
BF16 = jnp.bfloat16
I32 = jnp.int32
I16 = jnp.int16

D_MODEL = 2048
PAGE_SIZE = 128
HA, DKA, DVA = 4, 128, 256
HB, DKB, DVB = 4, 128, 256
CHUNK = 128
ROPE_BASE = 10000.0
HC, DHC = 8, 128
IDX_HEADS, IDX_DIM = 16, 64
TOPK_MAX = 256
QBLK = 128
N_BUCKETS = 32
MAX_DISTANCE = 128
D_CONV = 1024
CONV_W = 31
EPS = 1e-6

LANE = 128
BF16_ROWS = 16
MIB = 2 ** 20
INT_MIN = -(2 ** 31)
NEG = -1e30
LOG2E = 1.0 / math.log(2.0)
ATT_QB = 256
ATT_KC = 512
ATT_D0 = ATT_KC + LANE
ATT_W = 2 * ATT_KC + LANE
CONV_HALO = 32

AB_A_W = 2 * HA * DKA + 2 * HA * DVA
AB_B0 = AB_A_W + 2 * HA
AB_B_W = 2 * HB * DKB + 2 * HB * DVB
CD_A_W = 3 * HC * DHC + IDX_HEADS * IDX_DIM
CD_B0 = CD_A_W + IDX_DIM + IDX_HEADS
CD_B_W = 2 * D_CONV


def _cp(sem, vmem_mib):
    return pltpu.CompilerParams(dimension_semantics=sem, vmem_limit_bytes=vmem_mib * MIB)


def _sigmoid(x):
    return 1.0 / (1.0 + jnp.exp(-x))


def _log_sigmoid(x):
    return jnp.minimum(x, 0.0) - jnp.log1p(jnp.exp(-jnp.abs(x)))


def _dot(a, b):
    return jnp.dot(a, b, preferred_element_type=F32)


def _dot_nt(a, b):
    return lax.dot_general(a, b, (((1,), (1,)), ((), ())), preferred_element_type=F32)


def _pad_rows(a, rows):
    if a.shape[0] >= rows:
        return a
    return jnp.concatenate([a, jnp.zeros((rows - a.shape[0], a.shape[1]), a.dtype)], axis=0)


def _rms_body(x_ref, g_ref, o_ref):
    x = x_ref[...]
    y = x * lax.rsqrt(jnp.mean(x * x, axis=-1, keepdims=True) + EPS) * g_ref[...]
    o_ref[...] = y.astype(o_ref.dtype)


def _rmsnorm(x, g, out_dtype):
    m, d = x.shape
    tm = min(m, 512)
    return pl.pallas_call(
        _rms_body,
        grid=(m // tm,),
        in_specs=[pl.BlockSpec((tm, d), lambda i: (i, 0)), pl.BlockSpec((1, d), lambda i: (0, 0))],
        out_specs=pl.BlockSpec((tm, d), lambda i: (i, 0)),
        out_shape=jax.ShapeDtypeStruct((m, d), out_dtype),
        compiler_params=_cp(("parallel",), 32),
        name="rmsnorm",
    )(x, g.reshape(1, d))


def _ffn_body(x_ref, xs_ref, g_ref, wa_ref, wb_ref, wo_ref, o_ref, os_ref, xn_ref, xsn_ref):
    i = pl.program_id(0)
    j = pl.program_id(1)

    def normed(x):
        return x * lax.rsqrt(jnp.mean(x * x, axis=-1, keepdims=True) + EPS) * g_ref[...]

    @pl.when(j == 0)
    def _():
        x = x_ref[...]
        xn_ref[...] = normed(x).astype(BF16)
        o_ref[...] = x

    def half_swiglu(xn):
        a = _dot(xn, wa_ref[...].astype(BF16))
        b = _dot(xn, wb_ref[...].astype(BF16))
        return _dot((a * _sigmoid(a) * b * 0.5).astype(BF16), wo_ref[...].astype(BF16))

    o_ref[...] += half_swiglu(xn_ref[...])

    @pl.when(i == 0)
    def _():
        @pl.when(j == 0)
        def _():
            xs = xs_ref[...]
            xsn_ref[...] = _pad_rows(normed(xs), BF16_ROWS).astype(BF16)
            os_ref[...] = xs

        os_ref[...] += half_swiglu(xsn_ref[...])[:os_ref.shape[0]]


def _ffn(x, xs, g, w_in, w_out, layer):
    m, d = x.shape
    ms = xs.shape[0]
    f = w_out.shape[1]
    tm = min(m, 1024)
    tf = 512
    assert m % tm == 0 and f % tf == 0 and ms <= BF16_ROWS
    nf = f // tf
    return pl.pallas_call(
        _ffn_body,
        grid=(m // tm, nf),
        in_specs=[
            pl.BlockSpec((tm, d), lambda i, j: (i, 0), pipeline_mode=pl.Buffered(1)),
            pl.BlockSpec((ms, d), lambda i, j: (0, 0)),
            pl.BlockSpec((1, d), lambda i, j: (0, 0)),
            pl.BlockSpec((None, d, tf), lambda i, j: (layer, 0, j)),
            pl.BlockSpec((None, d, tf), lambda i, j: (layer, 0, j + nf)),
            pl.BlockSpec((None, tf, d), lambda i, j: (layer, j, 0)),
        ],
        out_specs=[pl.BlockSpec((tm, d), lambda i, j: (i, 0)), pl.BlockSpec((ms, d), lambda i, j: (0, 0))],
        out_shape=[jax.ShapeDtypeStruct((m, d), F32), jax.ShapeDtypeStruct((ms, d), F32)],
        scratch_shapes=[pltpu.VMEM((tm, d), BF16), pltpu.VMEM((BF16_ROWS, d), BF16)],
        compiler_params=_cp(("arbitrary", "arbitrary"), 60),
        name="ffn",
    )(x, xs, g.reshape(1, d), w_in, w_in, w_out)


def _mm_body(*refs, k_sizes, ks_sizes, w_transposed, has_res, n_out):
    n_a, n_s = len(k_sizes), len(ks_sizes)
    a_refs = refs[:n_a]
    s_refs = refs[n_a:n_a + n_s]
    w_ref = refs[n_a + n_s]
    pos = n_a + n_s + 1
    res_ref = sres_ref = None
    if has_res:
        res_ref, sres_ref = refs[pos], refs[pos + 1]
        pos += 2
    out_refs = refs[pos:pos + n_out]
    os_ref = refs[pos + n_out]
    wt_ref = refs[pos + n_out + 1]
    rows_per_step = 64

    def product(parts, sizes):
        acc = None
        k0 = 0
        for p_ref, ks in zip(parts, sizes):
            a = _pad_rows(p_ref[...], BF16_ROWS).astype(BF16)
            d = _dot_nt(a, wt_ref[:, k0:k0 + ks]) if w_transposed else _dot(a, wt_ref[k0:k0 + ks, :])
            acc = d if acc is None else acc + d
            k0 += ks
        return acc

    @pl.when(pl.program_id(1) == 0)
    def _():
        def cast_rows(c, carry):
            rows = pl.ds(pl.multiple_of(c * rows_per_step, rows_per_step), rows_per_step)
            if w_transposed:
                wt_ref[rows, :] = w_ref[0, rows, :].astype(BF16)
            else:
                wt_ref[rows, :] = w_ref[rows, :].astype(BF16)
            return carry

        lax.fori_loop(0, wt_ref.shape[0] // rows_per_step, cast_rows, 0)
        acc_s = product(s_refs, ks_sizes)[:os_ref.shape[0]]
        if has_res:
            acc_s = acc_s + sres_ref[...]
        os_ref[...] = acc_s

    acc = product(a_refs, k_sizes)
    if has_res:
        acc = acc + res_ref[...]
    for o_ref in out_refs:
        o_ref[...] = acc.astype(o_ref.dtype)


def _matmul(a_parts, s_parts, w, layer, col0, width, tn, res=None, s_res=None, out_dtypes=(F32,),
            w_transposed=False):
    m, ms = a_parts[0].shape[0], s_parts[0].shape[0]
    k_sizes = tuple(a.shape[1] for a in a_parts)
    ks_sizes = tuple(a.shape[1] for a in s_parts)
    k_total = sum(k_sizes)
    assert w.shape[2 if w_transposed else 1] == k_total and sum(ks_sizes) == k_total
    assert width % tn == 0 and tn % 64 == 0 and ms <= BF16_ROWS and (res is None) == (s_res is None)
    tm = min(m, 2048)
    assert m % tm == 0
    in_specs = [pl.BlockSpec((tm, ks), lambda j, i: (i, 0)) for ks in k_sizes]
    in_specs += [pl.BlockSpec((ms, ks), lambda j, i: (0, 0)) for ks in ks_sizes]
    if w_transposed:
        assert col0 % 8 == 0
        in_specs.append(pl.BlockSpec((pl.Element(1), pl.Element(tn), pl.Element(k_total)),
                                     lambda j, i: (layer, (col0 // 8 + j * (tn // 8)) * 8, 0)))
        wt_shape = (tn, k_total)
    else:
        assert col0 % tn == 0 and tn % LANE == 0
        in_specs.append(pl.BlockSpec((None, k_total, tn), lambda j, i: (layer, 0, col0 // tn + j)))
        wt_shape = (k_total, tn)
    args = list(a_parts) + list(s_parts) + [w]
    if res is not None:
        in_specs += [pl.BlockSpec((tm, tn), lambda j, i: (i, j)), pl.BlockSpec((ms, tn), lambda j, i: (0, j))]
        args += [res, s_res]
    outs = pl.pallas_call(
        functools.partial(_mm_body, k_sizes=k_sizes, ks_sizes=ks_sizes, w_transposed=w_transposed,
                          has_res=res is not None, n_out=len(out_dtypes)),
        grid=(width // tn, m // tm),
        in_specs=in_specs,
        out_specs=[pl.BlockSpec((tm, tn), lambda j, i: (i, j)) for _ in out_dtypes]
        + [pl.BlockSpec((ms, tn), lambda j, i: (0, j))],
        out_shape=[jax.ShapeDtypeStruct((m, width), dt) for dt in out_dtypes]
        + [jax.ShapeDtypeStruct((ms, width), F32)],
        scratch_shapes=[pltpu.VMEM(wt_shape, BF16)],
        compiler_params=_cp(("parallel", "arbitrary"), 56),
        name="matmul",
    )(*args)
    return outs


def _rope(x, cosf, sinf):
    return x * cosf + pltpu.roll(x, shift=x.shape[-1] // 2, axis=1) * sinf


def _ab_scan_body(sdec_ref, bg_ref, ya_ref, yg_ref, yb_ref, cos_ref, sin_ref, dmat_ref, qdec_ref, kdec_ref,
                  ga_ref, gb_ref, ma_ref, mb_ref, c_out, n_out, m_out, s_out, ct_scr, n_scr, m_scr, s_scr):
    c = pl.program_id(1)
    L = CHUNK

    @pl.when(c == 0)
    def _():
        ct_scr[...] = jnp.zeros_like(ct_scr)
        n_scr[...] = jnp.zeros_like(n_scr)
        m_scr[...] = jnp.zeros_like(m_scr)
        s_scr[...] = jnp.zeros_like(s_scr)

    g = yg_ref[...]
    g_t = g.T
    row = lax.broadcasted_iota(I32, (L, L), 0)
    col = lax.broadcasted_iota(I32, (L, L), 1)
    lower = col <= row
    cosf = cos_ref[...]
    sinf = sin_ref[...]

    for h in range(HA):
        q = ya_ref[:, h * DKA:(h + 1) * DKA]
        k = ya_ref[:, HA * DKA + h * DKA:HA * DKA + (h + 1) * DKA] * (DKA ** -0.5)
        v = ya_ref[:, 2 * HA * DKA + h * DVA:2 * HA * DKA + (h + 1) * DVA]
        o = ya_ref[:, 2 * HA * DKA + HA * DVA + h * DVA:2 * HA * DKA + HA * DVA + (h + 1) * DVA]
        b_i = bg_ref[h]
        b_f = bg_ref[HA + h]
        i_col = g[:, h:h + 1] + b_i
        f_col = _log_sigmoid(g[:, HA + h:HA + h + 1] + b_f)
        i_row = g_t[h:h + 1, :] + b_i
        f_row = _log_sigmoid(g_t[HA + h:HA + h + 1, :] + b_f)
        b_row = jnp.sum(jnp.where(row <= col, f_col, 0.0), axis=0, keepdims=True)
        b_col = jnp.sum(jnp.where(lower, f_row, 0.0), axis=1, keepdims=True)
        d_log = jnp.where(lower, b_col - b_row + i_row, -jnp.inf)
        m_prev = m_scr[h][:, 0:1]
        g_in = b_col + m_prev
        m = jnp.maximum(g_in, jnp.max(d_log, axis=1, keepdims=True))
        k_t = k.T.astype(BF16)
        q16 = q.astype(BF16)
        s = _dot(q16, k_t) * jnp.exp(d_log - m)
        inter = jnp.exp(g_in - m)
        ct = ct_scr[h]
        num = _dot(s.astype(BF16), v.astype(BF16)) + inter * _dot(q16, ct.astype(BF16))
        n_prev = n_scr[h]
        den = jnp.sum(s, axis=1, keepdims=True) + inter * jnp.sum(q * n_prev, axis=1, keepdims=True)
        hh = num / jnp.maximum(jnp.abs(den), jnp.exp(-m))
        m_new = m[L - 1:L, :]
        b_last = b_col[L - 1:L, :]
        decay = jnp.exp(b_last + m_prev - m_new)
        wk = jnp.exp(b_last - b_col + i_col - m_new)
        ct_scr[h] = decay * ct + _dot(k_t, (wk * v).astype(BF16))
        n_scr[h] = decay * n_prev + jnp.sum(wk * k, axis=0, keepdims=True)
        m_scr[h] = jnp.broadcast_to(m_new, (1, LANE))
        ha = hh * lax.rsqrt(jnp.mean(hh * hh, axis=1, keepdims=True) + EPS) * ga_ref[h]
        ma_ref[:, h * DVA:(h + 1) * DVA] = (ha * _sigmoid(o)).astype(BF16)

    for h in range(HB):
        qr = _rope(yb_ref[:, h * DKB:(h + 1) * DKB], cosf, sinf)
        kr = _rope(yb_ref[:, HB * DKB + h * DKB:HB * DKB + (h + 1) * DKB], cosf, sinf) * (DKB ** -0.5)
        vb = yb_ref[:, 2 * HB * DKB + h * DVB:2 * HB * DKB + (h + 1) * DVB]
        gate = yb_ref[:, 2 * HB * DKB + HB * DVB + h * DVB:2 * HB * DKB + HB * DVB + (h + 1) * DVB]
        kr_t = kr.T.astype(BF16)
        qr16 = qr.astype(BF16)
        a = _dot(qr16, kr_t) * dmat_ref[h]
        sp = s_scr[h]
        out = _dot(a.astype(BF16), vb.astype(BF16)) + qdec_ref[h] * _dot(qr16, sp.astype(BF16))
        s_scr[h] = sdec_ref[h] * sp + _dot(kr_t, (kdec_ref[h] * vb).astype(BF16))
        mu = jnp.mean(out, axis=1, keepdims=True)
        var = jnp.mean(jnp.square(out - mu), axis=1, keepdims=True)
        hb = (out - mu) * lax.rsqrt(var + EPS) * gb_ref[h]
        mb_ref[:, h * DVB:(h + 1) * DVB] = (hb * (gate * _sigmoid(gate))).astype(BF16)

    @pl.when(c == pl.num_programs(1) - 1)
    def _():
        for h in range(HA):
            c_out[h] = ct_scr[h].T
            n_out[h] = n_scr[h]
            m_out[h] = m_scr[h]
            s_out[h] = s_scr[h]


def _retention_consts(L):
    log_gamma = jnp.log1p(-jnp.exp2(-5.0 - jnp.arange(HB, dtype=F32)))
    j = jnp.arange(L, dtype=F32)
    diff = j[:, None] - j[None, :]
    decay_mat = jnp.where(diff >= 0, jnp.exp(log_gamma[:, None, None] * jnp.maximum(diff, 0.0)), 0.0)
    q_decay = jnp.exp(log_gamma[:, None] * (j + 1.0))
    k_decay = jnp.exp(log_gamma[:, None] * (L - 1.0 - j))
    state_decay = jnp.exp(log_gamma * L)
    return decay_mat, q_decay, k_decay, state_decay


def _rope_tables(pos):
    half = DKB // 2
    freqs = ROPE_BASE ** (-jnp.arange(half, dtype=F32) / half)
    ang = pos.astype(F32)[:, None] * freqs[None, :]
    cos, sin = jnp.cos(ang), jnp.sin(ang)
    return jnp.concatenate([cos, cos], axis=1), jnp.concatenate([-sin, sin], axis=1)


def _ab_scan(ya, yg, yb, b_gate, g_a, g_b, batch, seq):
    L = CHUNK
    nc = seq // L
    cosf, sinf = _rope_tables(jnp.arange(seq))
    dmat, qdec, kdec, sdec = _retention_consts(L)
    qdec = jnp.broadcast_to(qdec[:, :, None], (HB, L, DVB))
    kdec = jnp.broadcast_to(kdec[:, :, None], (HB, L, DVB))
    smem = pl.BlockSpec(memory_space=pltpu.SMEM)
    rows = lambda b, c: (b * nc + c, 0)
    const3 = lambda b, c: (0, 0, 0)
    state = lambda b, c: (b, 0, 0, 0)
    m = batch * seq
    return pl.pallas_call(
        _ab_scan_body,
        grid=(batch, nc),
        in_specs=[
            smem, smem,
            pl.BlockSpec((L, AB_A_W), rows),
            pl.BlockSpec((L, LANE), rows),
            pl.BlockSpec((L, AB_B_W), rows),
            pl.BlockSpec((L, DKB), lambda b, c: (c, 0)),
            pl.BlockSpec((L, DKB), lambda b, c: (c, 0)),
            pl.BlockSpec((HB, L, L), const3),
            pl.BlockSpec((HB, L, DVB), const3),
            pl.BlockSpec((HB, L, DVB), const3),
            pl.BlockSpec((HA, 1, DVA), const3),
            pl.BlockSpec((HB, 1, DVB), const3),
        ],
        out_specs=[
            pl.BlockSpec((L, HA * DVA), rows),
            pl.BlockSpec((L, HB * DVB), rows),
            pl.BlockSpec((None, HA, DVA, DKA), state),
            pl.BlockSpec((None, HA, 1, DKA), state),
            pl.BlockSpec((None, HA, 1, LANE), state),
            pl.BlockSpec((None, HB, DKB, DVB), state),
        ],
        out_shape=[
            jax.ShapeDtypeStruct((m, HA * DVA), BF16),
            jax.ShapeDtypeStruct((m, HB * DVB), BF16),
            jax.ShapeDtypeStruct((batch, HA, DVA, DKA), F32),
            jax.ShapeDtypeStruct((batch, HA, 1, DKA), F32),
            jax.ShapeDtypeStruct((batch, HA, 1, LANE), F32),
            jax.ShapeDtypeStruct((batch, HB, DKB, DVB), F32),
        ],
        scratch_shapes=[
            pltpu.VMEM((HA, DKA, DVA), F32),
            pltpu.VMEM((HA, 1, DKA), F32),
            pltpu.VMEM((HA, 1, LANE), F32),
            pltpu.VMEM((HB, DKB, DVB), F32),
        ],
        compiler_params=_cp(("parallel", "arbitrary"), 40),
        name="ab_scan",
    )(sdec, b_gate, ya, yg, yb, cosf, sinf, dmat, qdec, kdec,
      g_a.reshape(HA, 1, DVA), g_b.reshape(HB, 1, DVB))


def _lane_bcast_col(row_vec):
    n = row_vec.shape[1]
    return jnp.broadcast_to(row_vec, (LANE, n)).T


def _ab_step_body(sdec_ref, bg_ref, ya_ref, yg_ref, yb_ref, cos_ref, sin_ref, ga_ref, gb_ref,
                  c_ref, n_ref, m_ref, s_ref, mix_ref, c_out, n_out, m_out, s_out):
    ya = ya_ref[...]
    yg = yg_ref[...]
    yb = yb_ref[...]
    cosf = cos_ref[...]
    sinf = sin_ref[...]

    for h in range(HA):
        q = ya[:, h * DKA:(h + 1) * DKA]
        k = ya[:, HA * DKA + h * DKA:HA * DKA + (h + 1) * DKA] * (DKA ** -0.5)
        v = ya[:, 2 * HA * DKA + h * DVA:2 * HA * DKA + (h + 1) * DVA]
        o = ya[:, 2 * HA * DKA + HA * DVA + h * DVA:2 * HA * DKA + HA * DVA + (h + 1) * DVA]
        i_pre = yg[:, h:h + 1] + bg_ref[h]
        log_f = _log_sigmoid(yg[:, HA + h:HA + h + 1] + bg_ref[HA + h])
        m_prev = m_ref[h][:, 0:1]
        g_in = log_f + m_prev
        m = jnp.maximum(g_in, i_pre)
        w_in = jnp.exp(i_pre - m)
        inter = jnp.exp(g_in - m)
        s = jnp.sum(q * k, axis=1, keepdims=True) * w_in
        cmat = c_ref[h]
        cq = jnp.sum(cmat * q, axis=1, keepdims=True)
        cq_row = jnp.broadcast_to(cq, (DVA, LANE)).T[0:1, :]
        num = s * v + inter * cq_row
        n_prev = n_ref[h]
        den = s + inter * jnp.sum(n_prev * q, axis=1, keepdims=True)
        hrow = num / jnp.maximum(jnp.abs(den), jnp.exp(-m))
        c_out[h] = inter * cmat + (w_in * _lane_bcast_col(v)) * k
        n_out[h] = inter * n_prev + w_in * k
        m_out[h] = jnp.broadcast_to(m, (1, LANE))
        ha = hrow * lax.rsqrt(jnp.mean(hrow * hrow, axis=1, keepdims=True) + EPS) * ga_ref[h]
        mix_ref[:, h * DVA:(h + 1) * DVA] = ha * _sigmoid(o)

    base = HA * DVA
    for h in range(HB):
        qr = _rope(yb[:, h * DKB:(h + 1) * DKB], cosf, sinf)
        kr = _rope(yb[:, HB * DKB + h * DKB:HB * DKB + (h + 1) * DKB], cosf, sinf) * (DKB ** -0.5)
        vb = yb[:, 2 * HB * DKB + h * DVB:2 * HB * DKB + (h + 1) * DVB]
        gate = yb[:, 2 * HB * DKB + HB * DVB + h * DVB:2 * HB * DKB + HB * DVB + (h + 1) * DVB]
        gamma = sdec_ref[h]
        smat = s_ref[h]
        q_col = _lane_bcast_col(qr)
        k_col = _lane_bcast_col(kr)
        qs = jnp.concatenate([jnp.sum(q_col * smat[:, t * LANE:(t + 1) * LANE], axis=0, keepdims=True)
                              for t in range(DVB // LANE)], axis=1)
        out = jnp.sum(qr * kr, axis=1, keepdims=True) * vb + gamma * qs
        s_out[h] = gamma * smat + jnp.concatenate([k_col * vb[:, t * LANE:(t + 1) * LANE]
                                                   for t in range(DVB // LANE)], axis=1)
        mu = jnp.mean(out, axis=1, keepdims=True)
        var = jnp.mean(jnp.square(out - mu), axis=1, keepdims=True)
        hb = (out - mu) * lax.rsqrt(var + EPS) * gb_ref[h]
        mix_ref[:, base + h * DVB:base + (h + 1) * DVB] = hb * (gate * _sigmoid(gate))


def _ab_step(ya, yg, yb, b_gate, g_a, g_b, c0, n0, m0, s0, pos0):
    bs = ya.shape[0]
    cosf, sinf = _rope_tables(pos0 + jnp.arange(1))
    _, _, _, sdec = _retention_consts(1)
    smem = pl.BlockSpec(memory_space=pltpu.SMEM)
    full = lambda shape: pl.BlockSpec(shape, lambda b: tuple(0 for _ in shape))
    row = lambda width: pl.BlockSpec((None, 1, width), lambda b: (b, 0, 0))
    state = lambda b: (b, 0, 0, 0)
    mix_w = HA * DVA + HB * DVB
    outs = pl.pallas_call(
        _ab_step_body,
        grid=(bs,),
        in_specs=[
            smem, smem,
            row(AB_A_W), row(LANE), row(AB_B_W),
            full((1, DKB)), full((1, DKB)),
            full((HA, 1, DVA)), full((HB, 1, DVB)),
            pl.BlockSpec((None, HA, DVA, DKA), state),
            pl.BlockSpec((None, HA, 1, DKA), state),
            pl.BlockSpec((None, HA, 1, LANE), state),
            pl.BlockSpec((None, HB, DKB, DVB), state),
        ],
        out_specs=[
            row(mix_w),
            pl.BlockSpec((None, HA, DVA, DKA), state),
            pl.BlockSpec((None, HA, 1, DKA), state),
            pl.BlockSpec((None, HA, 1, LANE), state),
            pl.BlockSpec((None, HB, DKB, DVB), state),
        ],
        out_shape=[
            jax.ShapeDtypeStruct((bs, 1, mix_w), F32),
            jax.ShapeDtypeStruct((bs, HA, DVA, DKA), F32),
            jax.ShapeDtypeStruct((bs, HA, 1, DKA), F32),
            jax.ShapeDtypeStruct((bs, HA, 1, LANE), F32),
            jax.ShapeDtypeStruct((bs, HB, DKB, DVB), F32),
        ],
        compiler_params=_cp(("parallel",), 32),
        name="ab_step",
    )(sdec, b_gate, ya.reshape(bs, 1, AB_A_W), yg.reshape(bs, 1, LANE), yb.reshape(bs, 1, AB_B_W), cosf, sinf,
      g_a.reshape(HA, 1, DVA), g_b.reshape(HB, 1, DVB),
      c0, n0.reshape(bs, HA, 1, DKA), jnp.broadcast_to(m0[:, :, None, None], (bs, HA, 1, LANE)), s0)
    return (outs[0].reshape(bs, mix_w),) + tuple(outs[1:])


def _bucket_np(dist):
    exact = N_BUCKETS // 2
    dist = np.maximum(dist, 0)
    ratio = np.maximum(dist, 1).astype(np.float32) / np.float32(exact)
    log_ratio = np.log(ratio) / np.float32(math.log(MAX_DISTANCE / exact))
    large = np.minimum(exact + (log_ratio * (N_BUCKETS - exact)).astype(np.int32), N_BUCKETS - 1)
    return np.where(dist < exact, dist, large).astype(np.int32)


def _bucket_thresholds():
    table = _bucket_np(np.arange(4 * MAX_DISTANCE))
    return [int(np.argmax(table >= k)) for k in range(1, N_BUCKETS)]


def _bias_table_body(rb_ref, idx_ref, o_ref):
    idx = idx_ref[...]
    for h in range(HC):
        acc = jnp.zeros(idx.shape, F32)
        for bkt in range(N_BUCKETS):
            acc = jnp.where(idx == bkt, rb_ref[bkt, h] * LOG2E, acc)
        o_ref[h] = acc


def _bias_table(rel_bias):
    t = np.arange(ATT_QB)[None, :]
    w = np.arange(ATT_W)[:, None]
    idx = _bucket_np(t + ATT_D0 - w)
    return pl.pallas_call(
        _bias_table_body,
        in_specs=[pl.BlockSpec(memory_space=pltpu.SMEM), pl.BlockSpec((ATT_W, ATT_QB), lambda: (0, 0))],
        out_specs=pl.BlockSpec((HC, ATT_W, ATT_QB), lambda: (0, 0, 0)),
        out_shape=jax.ShapeDtypeStruct((HC, ATT_W, ATT_QB), F32),
        name="bias_table",
    )(rel_bias, jnp.asarray(idx))


def _to_key(x):
    bits = lax.bitcast_convert_type(x, I32)
    return bits ^ ((bits >> 31) & jnp.int32(0x7FFFFFFF))


def _fold_rows(x, op):
    chains = 4
    acc = [x[r * 8:(r + 1) * 8, :] for r in range(chains)]
    for r in range(chains, x.shape[0] // 8):
        acc[r % chains] = op(acc[r % chains], x[r * 8:(r + 1) * 8, :])
    return op(op(acc[0], acc[1]), op(acc[2], acc[3]))


def _attn_body(q_ref, qi_ref, wi_ref, ki_ref, k_ref, vt_ref, tab_ref, o_ref,
               key_scr, hi_scr, lo_scr, thr_scr, mask_scr, lg_scr, p_scr, m_scr, l_scr, acc_scr, *, topk):
    qb = pl.program_id(1)
    kc = pl.program_id(2)
    nq, kcs = ATT_QB, ATT_KC
    last = (qb * nq) // kcs
    sub = LANE

    @pl.when(kc == 0)
    def _():
        wi_t = wi_ref[...].T
        t_idx = qb * nq + lax.broadcasted_iota(I32, (kcs, nq), 1)
        score_scr = lg_scr.at[0]

        def score(c, carry):
            base = pl.multiple_of(c * kcs, kcs)
            ki = ki_ref[pl.ds(base, kcs), :][:, 0:IDX_DIM]
            for h in range(0, IDX_HEADS, 2):
                part = None
                for hh in (h, h + 1):
                    lg = _dot_nt(ki, qi_ref[:, hh * IDX_DIM:(hh + 1) * IDX_DIM])
                    term = wi_t[IDX_DIM + hh:IDX_DIM + hh + 1, :] * jnp.maximum(lg, 0.0)
                    part = term if part is None else part + term
                score_scr[...] = part if h == 0 else score_scr[...] + part
            s_idx = base + lax.broadcasted_iota(I32, (kcs, nq), 0)
            key = jnp.where(s_idx <= t_idx, _to_key(score_scr[...] + 0.0), jnp.int32(INT_MIN))
            key_scr[c] = key
            hi_scr[c] = (key >> 16).astype(I16)
            lo_scr[c] = ((key & 0xFFFF) - 32768).astype(I16)
            return carry

        lax.fori_loop(0, last + 1, score, 0)

        def count_ge(plane_scr, cand):
            cand_tile = jnp.broadcast_to(cand, (BF16_ROWS, nq)).astype(I16)

            def count(c, a):
                x = plane_scr[c]
                ge = [jnp.where(x[r * BF16_ROWS:(r + 1) * BF16_ROWS, :] >= cand_tile, jnp.int16(1), jnp.int16(0))
                      for r in range(kcs // BF16_ROWS)]
                chains = 4
                for r in range(chains, len(ge)):
                    ge[r % chains] = ge[r % chains] + ge[r]
                return a + ((ge[0] + ge[1]) + (ge[2] + ge[3]))

            a = lax.fori_loop(0, last + 1, count, jnp.zeros((BF16_ROWS, nq), I16))
            return jnp.sum(a.astype(F32), axis=0, keepdims=True)

        def bisect16(plane_scr, n_above):
            def step(i, thr):
                cand = thr + lax.shift_left(jnp.int32(1), 15 - i)
                return jnp.where(n_above + count_ge(plane_scr, cand) >= topk, cand, thr)
            return lax.fori_loop(0, 16, step, jnp.full((1, nq), -32768, I32))

        thr_hi = bisect16(hi_scr, 0.0)
        n_gt = jnp.where(thr_hi < 32767, count_ge(hi_scr, jnp.minimum(thr_hi + 1, 32767)), 0.0)
        hi_tile = jnp.broadcast_to(thr_hi, (BF16_ROWS, nq)).astype(I16)

        def mask_low(c, carry):
            for r in range(kcs // BF16_ROWS):
                rows = slice(r * BF16_ROWS, (r + 1) * BF16_ROWS)
                lo_scr[c, rows, :] = jnp.where(hi_scr[c, rows, :] == hi_tile, lo_scr[c, rows, :], jnp.int16(-32768))
            return carry

        lax.fori_loop(0, last + 1, mask_low, 0)
        thr_lo = bisect16(lo_scr, n_gt)
        thr = thr_hi * 65536 + (thr_lo + 32768)
        thr_scr[...] = jnp.maximum(thr, jnp.int32(INT_MIN + 1))
        m_scr[...] = jnp.full_like(m_scr, NEG)
        l_scr[...] = jnp.zeros_like(l_scr)
        acc_scr[...] = jnp.zeros_like(acc_scr)

    @pl.when(kc <= last)
    def _():
        mask_scr[...] = jnp.where(key_scr[kc] >= thr_scr[...], 0.0, NEG)
        w0 = pl.multiple_of(jnp.maximum(ATT_D0 - qb * nq + kc * kcs, 0), LANE)
        m_new = []
        for h in range(HC):
            hs = slice(h * DHC, (h + 1) * DHC)
            lg = (_dot_nt(k_ref[:, hs], q_ref[:, hs]) * (DHC ** -0.5 * LOG2E) + tab_ref[h, pl.ds(w0, kcs), :]
                  + mask_scr[...])
            lg_scr[h] = lg
            m_new.append(jnp.maximum(m_scr[h], jnp.max(_fold_rows(lg, jnp.maximum), axis=0, keepdims=True)))
        alpha = []
        for h in range(HC):
            p = jnp.exp2(lg_scr[h] - m_new[h])
            p_scr[h] = p.astype(BF16)
            alpha.append(jnp.exp2(m_scr[h] - m_new[h]))
            l_scr[h] = alpha[h] * l_scr[h] + jnp.sum(_fold_rows(p, jnp.add), axis=0, keepdims=True)
            m_scr[h] = m_new[h]
        for h in range(HC):
            hs = slice(h * DHC, (h + 1) * DHC)
            acc_scr[hs, :] = alpha[h] * acc_scr[hs, :] + _dot(vt_ref[hs, :], p_scr[h])

    @pl.when(kc == last)
    def _():
        for h in range(HC):
            hs = slice(h * DHC, (h + 1) * DHC)
            o_ref[:, hs] = (acc_scr[hs, :] / l_scr[h]).T.astype(o_ref.dtype)


def _transpose_body(x_ref, o_ref):
    o_ref[...] = x_ref[...].T


def _transpose_v(ya16, batch, seq):
    hd = HC * DHC
    tt = 512
    nt = seq // tt
    return pl.pallas_call(
        _transpose_body,
        grid=(batch, nt),
        in_specs=[pl.BlockSpec((tt, hd), lambda b, t: (b * nt + t, 2))],
        out_specs=pl.BlockSpec((None, hd, tt), lambda b, t: (b, 0, t)),
        out_shape=jax.ShapeDtypeStruct((batch, hd, seq), BF16),
        compiler_params=_cp(("parallel", "parallel"), 32),
        name="transpose_v",
    )(ya16)


def _attn_prompt(ya16, yg, yg16, vt16, table, batch, seq):
    nqb = seq // ATT_QB
    nkc = seq // ATT_KC
    topk = min(TOPK_MAX, seq // 4)
    kv_blk = lambda qb, kc: jnp.minimum(kc, (qb * ATT_QB) // ATT_KC)
    q_row = lambda b, qb, kc: b * nqb + qb
    hd = HC * DHC
    return pl.pallas_call(
        functools.partial(_attn_body, topk=topk),
        grid=(batch, nqb, nkc),
        in_specs=[
            pl.BlockSpec((ATT_QB, hd), lambda b, qb, kc: (q_row(b, qb, kc), 0)),
            pl.BlockSpec((ATT_QB, IDX_HEADS * IDX_DIM), lambda b, qb, kc: (q_row(b, qb, kc), 3)),
            pl.BlockSpec((ATT_QB, LANE), lambda b, qb, kc: (q_row(b, qb, kc), 0)),
            pl.BlockSpec((seq, LANE), lambda b, qb, kc: (b, 0)),
            pl.BlockSpec((ATT_KC, hd), lambda b, qb, kc: (b * nkc + kv_blk(qb, kc), 1)),
            pl.BlockSpec((None, hd, ATT_KC), lambda b, qb, kc: (b, 0, kv_blk(qb, kc))),
            pl.BlockSpec((HC, ATT_W, ATT_QB), lambda b, qb, kc: (0, 0, 0)),
        ],
        out_specs=pl.BlockSpec((ATT_QB, hd), lambda b, qb, kc: (q_row(b, qb, kc), 0)),
        out_shape=jax.ShapeDtypeStruct((batch * seq, hd), BF16),
        scratch_shapes=[
            pltpu.VMEM((nkc, ATT_KC, ATT_QB), I32),
            pltpu.VMEM((nkc, ATT_KC, ATT_QB), I16),
            pltpu.VMEM((nkc, ATT_KC, ATT_QB), I16),
            pltpu.VMEM((1, ATT_QB), I32),
            pltpu.VMEM((ATT_KC, ATT_QB), F32),
            pltpu.VMEM((HC, ATT_KC, ATT_QB), F32),
            pltpu.VMEM((HC, ATT_KC, ATT_QB), BF16),
            pltpu.VMEM((HC, 1, ATT_QB), F32),
            pltpu.VMEM((HC, 1, ATT_QB), F32),
            pltpu.VMEM((hd, ATT_QB), F32),
        ],
        compiler_params=_cp(("parallel", "arbitrary", "arbitrary"), 56),
        name="attn_prompt",
    )(ya16, ya16, yg, yg16, ya16, vt16, table)


def _conv_body(yb_ref, cw_ref, cb_ref, gn_ref, bn_ref, o_ref, st_ref, ubuf, *, tt):
    t = pl.program_id(1)
    halo = CONV_HALO
    rb = 32

    @pl.when(t == 0)
    def _():
        ubuf[0:halo, :] = jnp.zeros((halo, D_CONV), F32)

    @pl.when(t > 0)
    def _():
        ubuf[0:halo, :] = ubuf[tt:tt + halo, :]

    ubuf[halo:halo + tt, :] = yb_ref[:, 0:D_CONV] * _sigmoid(yb_ref[:, D_CONV:2 * D_CONV])
    for r in range(tt // rb):
        acc = jnp.broadcast_to(cb_ref[...], (rb, D_CONV))
        for w in range(CONV_W):
            off = halo - (CONV_W - 1) + w + r * rb
            acc = acc + ubuf[off:off + rb, :] * cw_ref[w:w + 1, :]
        mu = jnp.mean(acc, axis=1, keepdims=True)
        var = jnp.mean(jnp.square(acc - mu), axis=1, keepdims=True)
        yn = (acc - mu) * lax.rsqrt(var + EPS) * gn_ref[...] + bn_ref[...]
        o_ref[r * rb:(r + 1) * rb, :] = (yn * _sigmoid(yn)).astype(o_ref.dtype)

    @pl.when(t == pl.num_programs(1) - 1)
    def _():
        st_ref[...] = ubuf[halo + tt - (CONV_W - 1):halo + tt, :]


def _conv_prompt(yb, cw, cb, gn, bn, batch, seq):
    tt = 256
    nt = seq // tt
    const = lambda b, t: (0, 0)
    return pl.pallas_call(
        functools.partial(_conv_body, tt=tt),
        grid=(batch, nt),
        in_specs=[
            pl.BlockSpec((tt, 2 * D_CONV), lambda b, t: (b * nt + t, 0)),
            pl.BlockSpec((CONV_W, D_CONV), const),
            pl.BlockSpec((1, D_CONV), const),
            pl.BlockSpec((1, D_CONV), const),
            pl.BlockSpec((1, D_CONV), const),
        ],
        out_specs=[
            pl.BlockSpec((tt, D_CONV), lambda b, t: (b * nt + t, 0)),
            pl.BlockSpec((None, CONV_W - 1, D_CONV), lambda b, t: (b, 0, 0)),
        ],
        out_shape=[
            jax.ShapeDtypeStruct((batch * seq, D_CONV), BF16),
            jax.ShapeDtypeStruct((batch, CONV_W - 1, D_CONV), F32),
        ],
        scratch_shapes=[pltpu.VMEM((CONV_HALO + tt, D_CONV), F32)],
        compiler_params=_cp(("parallel", "arbitrary"), 32),
        name="conv_prompt",
    )(yb, cw, cb.reshape(1, D_CONV), gn.reshape(1, D_CONV), bn.reshape(1, D_CONV))


def _rows_to_sublanes(row_vec, n, width, start=0):
    return jnp.concatenate([row_vec[:, start + i * width:start + (i + 1) * width] for i in range(n)], axis=0)


def _select_body(pt_ref, ya_ref, yg_ref, pool_ref, pos_ref, pbuf, sc_scr, slot_scr, sem, *, layer, n_pages, topk):
    b = pl.program_id(0)
    group = 16

    def page_copy(p):
        return pltpu.make_async_copy(pool_ref.at[layer, pt_ref[b, p]], pbuf.at[p], sem.at[0])

    def start(p, carry):
        page_copy(p).start()
        return carry

    def wait(p, carry):
        page_copy(p).wait()
        return carry

    lax.fori_loop(0, n_pages, start, 0)
    lax.fori_loop(0, n_pages, wait, 0)

    q_row = ya_ref[...]
    g_row = yg_ref[...]
    qi = _rows_to_sublanes(q_row, IDX_HEADS, IDX_DIM, start=3 * HC * DHC).astype(BF16)
    wi = _rows_to_sublanes(g_row, IDX_HEADS, 1, start=IDX_DIM)
    ki_new = g_row[:, 0:IDX_DIM].astype(BF16)

    for g in range(n_pages // group):
        keys = jnp.concatenate([pbuf[g * group + p] for p in range(group)], axis=1).astype(BF16)
        lg = _dot(qi, keys)
        sc = jnp.sum(jnp.maximum(lg, 0.0) * wi, axis=0, keepdims=True)
        for p in range(group):
            sc_scr[g * group + p:g * group + p + 1, :] = sc[:, p * PAGE_SIZE:(p + 1) * PAGE_SIZE]
    lg_new = jnp.sum(qi.astype(F32) * ki_new.astype(F32), axis=1, keepdims=True)
    sc_new = jnp.sum(jnp.maximum(lg_new, 0.0) * wi, axis=0, keepdims=True)
    key = _to_key(sc_scr[...] + 0.0)
    key_new = _to_key(sc_new + 0.0)

    def count(mask, mask_new):
        return jnp.sum(jnp.sum(jnp.where(mask, 1.0, 0.0), axis=1, keepdims=True), axis=0, keepdims=True) \
            + jnp.where(mask_new, 1.0, 0.0)

    def bisect(i, thr):
        cand = thr + lax.shift_left(jnp.int32(1), 31 - i)
        return jnp.where(count(key >= cand, key_new >= cand) >= topk, cand, thr)

    thr = lax.fori_loop(0, 32, bisect, jnp.full((1, 1), INT_MIN, I32))

    r_i = lax.broadcasted_iota(I32, (PAGE_SIZE, PAGE_SIZE), 0)
    c_i = lax.broadcasted_iota(I32, (PAGE_SIZE, PAGE_SIZE), 1)
    upper = jnp.where(r_i <= c_i, 1.0, 0.0).astype(BF16)
    pr = lax.broadcasted_iota(I32, (n_pages, n_pages), 0)
    pc = lax.broadcasted_iota(I32, (n_pages, n_pages), 1)
    before = jnp.where(pc < pr, 1.0, 0.0).astype(BF16)

    def rank(mask):
        inc = _dot(jnp.where(mask, 1.0, 0.0).astype(BF16), upper)
        tot = jnp.broadcast_to(inc[:, PAGE_SIZE - 1:PAGE_SIZE], (n_pages, PAGE_SIZE))
        return inc + _dot(before, tot.astype(BF16))

    gt = key > thr
    eq = key == thr
    n_gt = count(gt, key_new > thr)
    slot_eq = n_gt + rank(eq) - 1.0
    slot = jnp.where(gt, rank(gt) - 1.0, jnp.where(eq & (slot_eq < topk), slot_eq, -1.0))
    slot_scr[...] = slot

    r_iota = lax.broadcasted_iota(I32, (topk, PAGE_SIZE), 0).astype(F32)
    lane = lax.broadcasted_iota(I32, (1, PAGE_SIZE), 1).astype(F32)

    def compact(p, acc):
        val = lax.convert_element_type(p * PAGE_SIZE + 1, F32) + lane
        return acc + jnp.where(slot_scr[pl.ds(p, 1), :] == r_iota, val, 0.0)

    acc = lax.fori_loop(0, n_pages, compact, jnp.zeros((topk, PAGE_SIZE), F32))
    pos = jnp.sum(acc, axis=1, keepdims=True) - 1.0
    pos_ref[...] = jnp.broadcast_to(pos, (topk, PAGE_SIZE)).astype(I32)


def _select_sample(ya, yg, pool_idx, page_table, layer, topk):
    bs = ya.shape[0]
    n_pages = page_table.shape[1]
    row = lambda width: pl.BlockSpec((None, 1, width), lambda b, pt: (b, 0, 0))
    out = pl.pallas_call(
        functools.partial(_select_body, layer=layer, n_pages=n_pages, topk=topk),
        grid_spec=pltpu.PrefetchScalarGridSpec(
            num_scalar_prefetch=1,
            grid=(bs,),
            in_specs=[row(ya.shape[1]), row(yg.shape[1]), pl.BlockSpec(memory_space=pl.ANY)],
            out_specs=pl.BlockSpec((None, topk, PAGE_SIZE), lambda b, pt: (b, 0, 0)),
            scratch_shapes=[
                pltpu.VMEM((n_pages, IDX_DIM, PAGE_SIZE), F32),
                pltpu.VMEM((n_pages, PAGE_SIZE), F32),
                pltpu.VMEM((n_pages, PAGE_SIZE), F32),
                pltpu.SemaphoreType.DMA((1,)),
            ],
        ),
        out_shape=jax.ShapeDtypeStruct((bs, topk, PAGE_SIZE), I32),
        compiler_params=_cp(("arbitrary",), 40),
        name="select_sample",
    )(page_table, ya.reshape(bs, 1, -1), yg.reshape(bs, 1, -1), pool_idx)
    return out[:, :, 0]


def _sattn_body(pos_ref, pt_ref, ya_ref, posv_ref, rb_ref, kpool, vpool, o_ref, kbuf, vbuf, sem,
                *, layer, past, topk, thresholds):
    b = pl.program_id(0)

    def row_copies(r):
        pos = jnp.maximum(pos_ref[b, r], 0)
        page = pt_ref[b, pos // PAGE_SIZE]
        off = pos % PAGE_SIZE
        return (pltpu.make_async_copy(kpool.at[layer, page, off], kbuf.at[r], sem.at[0]),
                pltpu.make_async_copy(vpool.at[layer, page, off], vbuf.at[r], sem.at[1]))

    def start(r, carry):
        @pl.when(pos_ref[b, r] >= 0)
        def _():
            ck, cv = row_copies(r)
            ck.start()
            cv.start()
        return carry

    lax.fori_loop(0, topk, start, 0)

    y_row = ya_ref[...]
    k_new = _rows_to_sublanes(y_row, HC, DHC, start=HC * DHC)
    v_new = _rows_to_sublanes(y_row, HC, DHC, start=2 * HC * DHC)

    def wait(r, carry):
        @pl.when(pos_ref[b, r] >= 0)
        def _():
            ck, cv = row_copies(r)
            ck.wait()
            cv.wait()

        @pl.when(pos_ref[b, r] < 0)
        def _():
            kbuf[r] = k_new
            vbuf[r] = v_new
        return carry

    lax.fori_loop(0, topk, wait, 0)

    posv = posv_ref[...]
    dist = jnp.where(posv < 0, 0, past - posv)
    bucket = jnp.zeros(dist.shape, I32)
    for thr in thresholds:
        bucket = bucket + jnp.where(dist >= thr, 1, 0)
    for h in range(HC):
        hs = slice(h * DHC, (h + 1) * DHC)
        bias = jnp.zeros(dist.shape, F32)
        for bkt in range(N_BUCKETS):
            bias = jnp.where(bucket == bkt, rb_ref[bkt, h], bias)
        q16 = jnp.broadcast_to(y_row[:, hs], (BF16_ROWS, DHC)).astype(BF16)
        lg = _dot_nt(q16, kbuf[:, h, :].astype(BF16)) * (DHC ** -0.5) + bias
        p = jnp.exp(lg - jnp.max(lg, axis=1, keepdims=True))
        p = p / jnp.sum(p, axis=1, keepdims=True)
        out = _dot(p.astype(BF16), vbuf[:, h, :].astype(BF16))
        o_ref[:, hs] = out[0:1, :]


def _attn_sample(ya, pos, rel_bias, pool_k, pool_v, page_table, layer, topk):
    bs = ya.shape[0]
    past = page_table.shape[1] * PAGE_SIZE
    row = lambda width: pl.BlockSpec((None, 1, width), lambda b, p, pt: (b, 0, 0))
    out = pl.pallas_call(
        functools.partial(_sattn_body, layer=layer, past=past, topk=topk, thresholds=_bucket_thresholds()),
        grid_spec=pltpu.PrefetchScalarGridSpec(
            num_scalar_prefetch=2,
            grid=(bs,),
            in_specs=[row(ya.shape[1]), row(topk), pl.BlockSpec(memory_space=pltpu.SMEM),
                      pl.BlockSpec(memory_space=pl.ANY), pl.BlockSpec(memory_space=pl.ANY)],
            out_specs=row(HC * DHC),
            scratch_shapes=[
                pltpu.VMEM((topk, HC, DHC), F32),
                pltpu.VMEM((topk, HC, DHC), F32),
                pltpu.SemaphoreType.DMA((2,)),
            ],
        ),
        out_shape=jax.ShapeDtypeStruct((bs, 1, HC * DHC), F32),
        compiler_params=_cp(("arbitrary",), 32),
        name="attn_sample",
    )(pos, page_table, ya.reshape(bs, 1, -1), pos.reshape(bs, 1, topk), rel_bias, pool_k, pool_v)
    return out.reshape(bs, HC * DHC)


def _conv_step_body(yb_ref, st_ref, cw_ref, cb_ref, gn_ref, bn_ref, o_ref, st_out):
    y = yb_ref[...]
    u = y[:, 0:D_CONV] * _sigmoid(y[:, D_CONV:2 * D_CONV])
    nprev = CONV_W - 1
    acc = jnp.sum(st_ref[...] * cw_ref[0:nprev, :], axis=0, keepdims=True) + u * cw_ref[nprev:CONV_W, :] + cb_ref[...]
    mu = jnp.mean(acc, axis=1, keepdims=True)
    var = jnp.mean(jnp.square(acc - mu), axis=1, keepdims=True)
    yn = (acc - mu) * lax.rsqrt(var + EPS) * gn_ref[...] + bn_ref[...]
    o_ref[...] = yn * _sigmoid(yn)
    st_out[0:nprev - 1, :] = st_ref[1:nprev, :]
    st_out[nprev - 1:nprev, :] = u


def _conv_step(yb, state, cw, cb, gn, bn):
    bs = yb.shape[0]
    full = lambda shape: pl.BlockSpec(shape, lambda b: tuple(0 for _ in shape))
    st_spec = pl.BlockSpec((None, CONV_W - 1, D_CONV), lambda b: (b, 0, 0))
    row = lambda width: pl.BlockSpec((None, 1, width), lambda b: (b, 0, 0))
    out, st = pl.pallas_call(
        _conv_step_body,
        grid=(bs,),
        in_specs=[row(2 * D_CONV), st_spec, full((CONV_W, D_CONV)), full((1, D_CONV)), full((1, D_CONV)),
                  full((1, D_CONV))],
        out_specs=[row(D_CONV), st_spec],
        out_shape=[jax.ShapeDtypeStruct((bs, 1, D_CONV), F32),
                   jax.ShapeDtypeStruct((bs, CONV_W - 1, D_CONV), F32)],
        compiler_params=_cp(("parallel",), 32),
        name="conv_step",
    )(yb.reshape(bs, 1, 2 * D_CONV), state, cw, cb.reshape(1, D_CONV), gn.reshape(1, D_CONV),
      bn.reshape(1, D_CONV))
    return out.reshape(bs, D_CONV), st


def kernel(x_prompt, x_sample, state_mlstm_c, state_mlstm_n, state_mlstm_m, state_ret, cache_k, cache_v,
           cache_idx_k, state_conv, page_table, norm_ffn1, w_ffn1_in, w_ffn1_out, norm_mix, norm_ffn2,
           w_ffn2_in, w_ffn2_out, norm_final, w_ab_in, b_ab_gate, g_mlstm_norm, g_ret_norm, w_ab_out,
           w_cd_in, w_cd_out, rel_bias, conv_w, conv_b, g_conv_norm, b_conv_norm):
    batch, seq, d = x_prompt.shape
    bs = x_sample.shape[0]
    depth = norm_ffn1.shape[0]
    past = page_table.shape[1] * PAGE_SIZE
    hd = HC * DHC
    assert x_sample.shape[1] == 1 and seq % ATT_KC == 0 and d == D_MODEL
    table = _bias_table(rel_bias)
    w_ab_in_t = jnp.swapaxes(w_ab_in, 1, 2)
    w_cd_in_t = jnp.swapaxes(w_cd_in, 1, 2)
    pool_idx_t = jnp.swapaxes(cache_idx_k, 2, 3)
    in_proj = functools.partial(_matmul, w_transposed=True)

    x = x_prompt.reshape(batch * seq, d)
    xs = x_sample.reshape(bs, d)
    topk_s = min(TOPK_MAX, (past + 1) // 4)
    p_ab, s_ab, p_cd, s_cd = [], [], [], []

    def cd_state(ya, yg, conv_state, lead):
        return (ya[:, hd:2 * hd].reshape(lead + (HC, DHC)), ya[:, 2 * hd:3 * hd].reshape(lead + (HC, DHC)),
                yg[:, 0:IDX_DIM].reshape(lead + (IDX_DIM,)), conv_state)

    for l in range(depth):
        j = l // 2
        x, xs = _ffn(x, xs, norm_ffn1[l], w_ffn1_in, w_ffn1_out, l)
        xn = _rmsnorm(x, norm_mix[l], BF16)
        xsn = _rmsnorm(xs, norm_mix[l], F32)
        if l % 2 == 0:
            ya, ya_s = in_proj([xn], [xsn], w_ab_in_t, j, 0, AB_A_W, 512)
            yg, yg_s = in_proj([xn], [xsn], w_ab_in_t, j, AB_A_W, LANE, LANE)
            yb, yb_s = in_proj([xn], [xsn], w_ab_in_t, j, AB_B0, AB_B_W, 512)
            gates = (b_ab_gate[j], g_mlstm_norm[j], g_ret_norm[j])
            ma, mb, c, n, mm, s = _ab_scan(ya, yg, yb, *gates, batch, seq)
            p_ab.append((c, n[:, :, 0, :], mm[:, :, 0, 0], s))
            mix_s, c, n, mm, s = _ab_step(ya_s, yg_s, yb_s, *gates, state_mlstm_c[j], state_mlstm_n[j],
                                          state_mlstm_m[j], state_ret[j], past)
            s_ab.append((c, n[:, :, 0, :], mm[:, :, 0, 0], s))
            x, xs = _matmul([ma, mb], [mix_s], w_ab_out, j, 0, d, 512, res=x, s_res=xs)
        else:
            ya, ya16, ya_s = in_proj([xn], [xsn], w_cd_in_t, j, 0, CD_A_W, 512, out_dtypes=(F32, BF16))
            yg, yg16, yg_s = in_proj([xn], [xsn], w_cd_in_t, j, CD_A_W, LANE, LANE, out_dtypes=(F32, BF16))
            yb, yb_s = in_proj([xn], [xsn], w_cd_in_t, j, CD_B0, CD_B_W, 512)
            conv_args = (conv_w[j], conv_b[j], g_conv_norm[j], b_conv_norm[j])
            attn = _attn_prompt(ya16, yg, yg16, _transpose_v(ya16, batch, seq), table, batch, seq)
            conv, conv_state = _conv_prompt(yb, *conv_args, batch, seq)
            p_cd.append(cd_state(ya, yg, conv_state, (batch, seq)))
            pos = _select_sample(ya_s, yg_s, pool_idx_t, page_table, j, topk_s)
            attn_s = _attn_sample(ya_s, pos, rel_bias, cache_k, cache_v, page_table, j, topk_s)
            conv_s, conv_state_s = _conv_step(yb_s, state_conv[j], *conv_args)
            s_cd.append(cd_state(ya_s, yg_s, conv_state_s, (bs, 1)))
            x, xs = _matmul([attn, conv], [attn_s, conv_s], w_cd_out, j, 0, d, 512, res=x, s_res=xs)
        x, xs = _ffn(x, xs, norm_ffn2[l], w_ffn2_in, w_ffn2_out, l)
    y_p = _rmsnorm(x, norm_final, F32)
    y_s = _rmsnorm(xs, norm_final, F32)
    stack = lambda states: [jnp.stack(a) for a in zip(*states)]
    (pc, pn, pm, ps), (sc, sn, sm, ss) = stack(p_ab), stack(s_ab)
    (pk, pv, pik, pcv), (sk, sv, sik, scv) = stack(p_cd), stack(s_cd)
    return (y_p.reshape(batch, seq, d), y_s.reshape(bs, 1, d), pc, sc, pn, sn, pm, sm, ps, ss,
            pk, sk, pv, sv, pik, sik, pcv, scv)
```

```python
import functools
import math

import numpy as np
import jax
import jax.numpy as jnp
from jax import lax
from jax.experimental import pallas as pl
from jax.experimental.pallas import tpu as pltpu

F32 = jnp.float32
BF16 = jnp.bfloat16
I32 = jnp.int32
I16 = jnp.int16

D_MODEL = 2048
PAGE_SIZE = 128
HA, DKA, DVA = 4, 128, 256
HB, DKB, DVB = 4, 128, 256
CHUNK = 128
ROPE_BASE = 10000.0
HC, DHC = 8, 128
IDX_HEADS, IDX_DIM = 16, 64
TOPK_MAX = 256
QBLK = 128
N_BUCKETS = 32
MAX_DISTANCE = 128
D_CONV = 1024
CONV_W = 31
EPS = 1e-6

LANE = 128
BF16_ROWS = 16
MIB = 2 ** 20
INT_MIN = -(2 ** 31)
NEG = -1e30
LOG2E = 1.0 / math.log(2.0)
ATT_QB = 256
ATT_KC = 512
ATT_D0 = ATT_KC + LANE
ATT_W = 2 * ATT_KC + LANE
CONV_HALO = 32

AB_A_W = 2 * HA * DKA + 2 * HA * DVA
AB_B0 = AB_A_W + 2 * HA
AB_B_W = 2 * HB * DKB + 2 * HB * DVB
CD_A_W = 3 * HC * DHC + IDX_HEADS * IDX_DIM
CD_B0 = CD_A_W + IDX_DIM + IDX_HEADS
CD_B_W = 2 * D_CONV


def _cp(sem, vmem_mib):
    return pltpu.CompilerParams(dimension_semantics=sem, vmem_limit_bytes=vmem_mib * MIB)


def _sigmoid(x):
    return 1.0 / (1.0 + jnp.exp(-x))


def _log_sigmoid(x):
    return jnp.minimum(x, 0.0) - jnp.log1p(jnp.exp(-jnp.abs(x)))


def _dot(a, b):
    return jnp.dot(a, b, preferred_element_type=F32)


def _dot_nt(a, b):
    return lax.dot_general(a, b, (((1,), (1,)), ((), ())), preferred_element_type=F32)


def _pad_rows(a, rows):
    if a.shape[0] >= rows:
        return a
    return jnp.concatenate([a, jnp.zeros((rows - a.shape[0], a.shape[1]), a.dtype)], axis=0)


def _rms_body(x_ref, g_ref, o_ref):
    x = x_ref[...]
    y = x * lax.rsqrt(jnp.mean(x * x, axis=-1, keepdims=True) + EPS) * g_ref[...]
    o_ref[...] = y.astype(o_ref.dtype)


def _rmsnorm(x, g, out_dtype):
    m, d = x.shape
    tm = min(m, 512)
    return pl.pallas_call(
        _rms_body,
        grid=(m // tm,),
        in_specs=[pl.BlockSpec((tm, d), lambda i: (i, 0)), pl.BlockSpec((1, d), lambda i: (0, 0))],
        out_specs=pl.BlockSpec((tm, d), lambda i: (i, 0)),
        out_shape=jax.ShapeDtypeStruct((m, d), out_dtype),
        compiler_params=_cp(("parallel",), 32),
        name="rmsnorm",
    )(x, g.reshape(1, d))


def _ffn_body(x_ref, xs_ref, g_ref, wa_ref, wb_ref, wo_ref, o_ref, os_ref, xn_ref, xsn_ref):
    i = pl.program_id(0)
    j = pl.program_id(1)

    def normed(x):
        return x * lax.rsqrt(jnp.mean(x * x, axis=-1, keepdims=True) + EPS) * g_ref[...]

    @pl.when(j == 0)
    def _():
        x = x_ref[...]
        xn_ref[...] = normed(x).astype(BF16)
        o_ref[...] = x

    def half_swiglu(xn):
        a = _dot(xn, wa_ref[...].astype(BF16))
        b = _dot(xn, wb_ref[...].astype(BF16))
        return _dot((a * _sigmoid(a) * b * 0.5).astype(BF16), wo_ref[...].astype(BF16))

    o_ref[...] += half_swiglu(xn_ref[...])

    @pl.when(i == 0)
    def _():
        @pl.when(j == 0)
        def _():
            xs = xs_ref[...]
            xsn_ref[...] = _pad_rows(normed(xs), BF16_ROWS).astype(BF16)
            os_ref[...] = xs

        os_ref[...] += half_swiglu(xsn_ref[...])[:os_ref.shape[0]]


def _ffn(x, xs, g, w_in, w_out, layer):
    m, d = x.shape
    ms = xs.shape[0]
    f = w_out.shape[1]
    tm = min(m, 1024)
    tf = 512
    assert m % tm == 0 and f % tf == 0 and ms <= BF16_ROWS
    nf = f // tf
    return pl.pallas_call(
        _ffn_body,
        grid=(m // tm, nf),
        in_specs=[
            pl.BlockSpec((tm, d), lambda i, j: (i, 0), pipeline_mode=pl.Buffered(1)),
            pl.BlockSpec((ms, d), lambda i, j: (0, 0)),
            pl.BlockSpec((1, d), lambda i, j: (0, 0)),
            pl.BlockSpec((None, d, tf), lambda i, j: (layer, 0, j)),
            pl.BlockSpec((None, d, tf), lambda i, j: (layer, 0, j + nf)),
            pl.BlockSpec((None, tf, d), lambda i, j: (layer, j, 0)),
        ],
        out_specs=[pl.BlockSpec((tm, d), lambda i, j: (i, 0)), pl.BlockSpec((ms, d), lambda i, j: (0, 0))],
        out_shape=[jax.ShapeDtypeStruct((m, d), F32), jax.ShapeDtypeStruct((ms, d), F32)],
        scratch_shapes=[pltpu.VMEM((tm, d), BF16), pltpu.VMEM((BF16_ROWS, d), BF16)],
        compiler_params=_cp(("arbitrary", "arbitrary"), 60),
        name="ffn",
    )(x, xs, g.reshape(1, d), w_in, w_in, w_out)


def _mm_body(*refs, k_sizes, ks_sizes, w_transposed, has_res, n_out):
    n_a, n_s = len(k_sizes), len(ks_sizes)
    a_refs = refs[:n_a]
    s_refs = refs[n_a:n_a + n_s]
    w_ref = refs[n_a + n_s]
    pos = n_a + n_s + 1
    res_ref = sres_ref = None
    if has_res:
        res_ref, sres_ref = refs[pos], refs[pos + 1]
        pos += 2
    out_refs = refs[pos:pos + n_out]
    os_ref = refs[pos + n_out]
    wt_ref = refs[pos + n_out + 1]
    rows_per_step = 64

    def product(parts, sizes):
        acc = None
        k0 = 0
        for p_ref, ks in zip(parts, sizes):
            a = _pad_rows(p_ref[...], BF16_ROWS).astype(BF16)
            d = _dot_nt(a, wt_ref[:, k0:k0 + ks]) if w_transposed else _dot(a, wt_ref[k0:k0 + ks, :])
            acc = d if acc is None else acc + d
            k0 += ks
        return acc

    @pl.when(pl.program_id(1) == 0)
    def _():
        def cast_rows(c, carry):
            rows = pl.ds(pl.multiple_of(c * rows_per_step, rows_per_step), rows_per_step)
            if w_transposed:
                wt_ref[rows, :] = w_ref[0, rows, :].astype(BF16)
            else:
                wt_ref[rows, :] = w_ref[rows, :].astype(BF16)
            return carry

        lax.fori_loop(0, wt_ref.shape[0] // rows_per_step, cast_rows, 0)
        acc_s = product(s_refs, ks_sizes)[:os_ref.shape[0]]
        if has_res:
            acc_s = acc_s + sres_ref[...]
        os_ref[...] = acc_s

    acc = product(a_refs, k_sizes)
    if has_res:
        acc = acc + res_ref[...]
    for o_ref in out_refs:
        o_ref[...] = acc.astype(o_ref.dtype)


def _matmul(a_parts, s_parts, w, layer, col0, width, tn, res=None, s_res=None, out_dtypes=(F32,),
            w_transposed=False):
    m, ms = a_parts[0].shape[0], s_parts[0].shape[0]
    k_sizes = tuple(a.shape[1] for a in a_parts)
    ks_sizes = tuple(a.shape[1] for a in s_parts)
    k_total = sum(k_sizes)
    assert w.shape[2 if w_transposed else 1] == k_total and sum(ks_sizes) == k_total
    assert width % tn == 0 and tn % 64 == 0 and ms <= BF16_ROWS and (res is None) == (s_res is None)
    tm = min(m, 2048)
    assert m % tm == 0
    in_specs = [pl.BlockSpec((tm, ks), lambda j, i: (i, 0)) for ks in k_sizes]
    in_specs += [pl.BlockSpec((ms, ks), lambda j, i: (0, 0)) for ks in ks_sizes]
    if w_transposed:
        assert col0 % 8 == 0
        in_specs.append(pl.BlockSpec((pl.Element(1), pl.Element(tn), pl.Element(k_total)),
                                     lambda j, i: (layer, (col0 // 8 + j * (tn // 8)) * 8, 0)))
        wt_shape = (tn, k_total)
    else:
        assert col0 % tn == 0 and tn % LANE == 0
        in_specs.append(pl.BlockSpec((None, k_total, tn), lambda j, i: (layer, 0, col0 // tn + j)))
        wt_shape = (k_total, tn)
    args = list(a_parts) + list(s_parts) + [w]
    if res is not None:
        in_specs += [pl.BlockSpec((tm, tn), lambda j, i: (i, j)), pl.BlockSpec((ms, tn), lambda j, i: (0, j))]
        args += [res, s_res]
    outs = pl.pallas_call(
        functools.partial(_mm_body, k_sizes=k_sizes, ks_sizes=ks_sizes, w_transposed=w_transposed,
                          has_res=res is not None, n_out=len(out_dtypes)),
        grid=(width // tn, m // tm),
        in_specs=in_specs,
        out_specs=[pl.BlockSpec((tm, tn), lambda j, i: (i, j)) for _ in out_dtypes]
        + [pl.BlockSpec((ms, tn), lambda j, i: (0, j))],
        out_shape=[jax.ShapeDtypeStruct((m, width), dt) for dt in out_dtypes]
        + [jax.ShapeDtypeStruct((ms, width), F32)],
        scratch_shapes=[pltpu.VMEM(wt_shape, BF16)],
        compiler_params=_cp(("parallel", "arbitrary"), 56),
        name="matmul",
    )(*args)
    return outs


def _rope(x, cosf, sinf):
    return x * cosf + pltpu.roll(x, shift=x.shape[-1] // 2, axis=1) * sinf


def _ab_scan_body(sdec_ref, bg_ref, ya_ref, yg_ref, yb_ref, cos_ref, sin_ref, dmat_ref, qdec_ref, kdec_ref,
                  ga_ref, gb_ref, ma_ref, mb_ref, c_out, n_out, m_out, s_out, ct_scr, n_scr, m_scr, s_scr):
    c = pl.program_id(1)
    L = CHUNK

    @pl.when(c == 0)
    def _():
        ct_scr[...] = jnp.zeros_like(ct_scr)
        n_scr[...] = jnp.zeros_like(n_scr)
        m_scr[...] = jnp.zeros_like(m_scr)
        s_scr[...] = jnp.zeros_like(s_scr)

    g = yg_ref[...]
    g_t = g.T
    row = lax.broadcasted_iota(I32, (L, L), 0)
    col = lax.broadcasted_iota(I32, (L, L), 1)
    lower = col <= row
    cosf = cos_ref[...]
    sinf = sin_ref[...]

    for h in range(HA):
        q = ya_ref[:, h * DKA:(h + 1) * DKA]
        k = ya_ref[:, HA * DKA + h * DKA:HA * DKA + (h + 1) * DKA] * (DKA ** -0.5)
        v = ya_ref[:, 2 * HA * DKA + h * DVA:2 * HA * DKA + (h + 1) * DVA]
        o = ya_ref[:, 2 * HA * DKA + HA * DVA + h * DVA:2 * HA * DKA + HA * DVA + (h + 1) * DVA]
        b_i = bg_ref[h]
        b_f = bg_ref[HA + h]
        i_col = g[:, h:h + 1] + b_i
        f_col = _log_sigmoid(g[:, HA + h:HA + h + 1] + b_f)
        i_row = g_t[h:h + 1, :] + b_i
        f_row = _log_sigmoid(g_t[HA + h:HA + h + 1, :] + b_f)
        b_row = jnp.sum(jnp.where(row <= col, f_col, 0.0), axis=0, keepdims=True)
        b_col = jnp.sum(jnp.where(lower, f_row, 0.0), axis=1, keepdims=True)
        d_log = jnp.where(lower, b_col - b_row + i_row, -jnp.inf)
        m_prev = m_scr[h][:, 0:1]
        g_in = b_col + m_prev
        m = jnp.maximum(g_in, jnp.max(d_log, axis=1, keepdims=True))
        k_t = k.T.astype(BF16)
        q16 = q.astype(BF16)
        s = _dot(q16, k_t) * jnp.exp(d_log - m)
        inter = jnp.exp(g_in - m)
        ct = ct_scr[h]
        num = _dot(s.astype(BF16), v.astype(BF16)) + inter * _dot(q16, ct.astype(BF16))
        n_prev = n_scr[h]
        den = jnp.sum(s, axis=1, keepdims=True) + inter * jnp.sum(q * n_prev, axis=1, keepdims=True)
        hh = num / jnp.maximum(jnp.abs(den), jnp.exp(-m))
        m_new = m[L - 1:L, :]
        b_last = b_col[L - 1:L, :]
        decay = jnp.exp(b_last + m_prev - m_new)
        wk = jnp.exp(b_last - b_col + i_col - m_new)
        ct_scr[h] = decay * ct + _dot(k_t, (wk * v).astype(BF16))
        n_scr[h] = decay * n_prev + jnp.sum(wk * k, axis=0, keepdims=True)
        m_scr[h] = jnp.broadcast_to(m_new, (1, LANE))
        ha = hh * lax.rsqrt(jnp.mean(hh * hh, axis=1, keepdims=True) + EPS) * ga_ref[h]
        ma_ref[:, h * DVA:(h + 1) * DVA] = (ha * _sigmoid(o)).astype(BF16)

    for h in range(HB):
        qr = _rope(yb_ref[:, h * DKB:(h + 1) * DKB], cosf, sinf)
        kr = _rope(yb_ref[:, HB * DKB + h * DKB:HB * DKB + (h + 1) * DKB], cosf, sinf) * (DKB ** -0.5)
        vb = yb_ref[:, 2 * HB * DKB + h * DVB:2 * HB * DKB + (h + 1) * DVB]
        gate = yb_ref[:, 2 * HB * DKB + HB * DVB + h * DVB:2 * HB * DKB + HB * DVB + (h + 1) * DVB]
        kr_t = kr.T.astype(BF16)
        qr16 = qr.astype(BF16)
        a = _dot(qr16, kr_t) * dmat_ref[h]
        sp = s_scr[h]
        out = _dot(a.astype(BF16), vb.astype(BF16)) + qdec_ref[h] * _dot(qr16, sp.astype(BF16))
        s_scr[h] = sdec_ref[h] * sp + _dot(kr_t, (kdec_ref[h] * vb).astype(BF16))
        mu = jnp.mean(out, axis=1, keepdims=True)
        var = jnp.mean(jnp.square(out - mu), axis=1, keepdims=True)
        hb = (out - mu) * lax.rsqrt(var + EPS) * gb_ref[h]
        mb_ref[:, h * DVB:(h + 1) * DVB] = (hb * (gate * _sigmoid(gate))).astype(BF16)

    @pl.when(c == pl.num_programs(1) - 1)
    def _():
        for h in range(HA):
            c_out[h] = ct_scr[h].T
            n_out[h] = n_scr[h]
            m_out[h] = m_scr[h]
            s_out[h] = s_scr[h]


def _retention_consts(L):
    log_gamma = jnp.log1p(-jnp.exp2(-5.0 - jnp.arange(HB, dtype=F32)))
    j = jnp.arange(L, dtype=F32)
    diff = j[:, None] - j[None, :]
    decay_mat = jnp.where(diff >= 0, jnp.exp(log_gamma[:, None, None] * jnp.maximum(diff, 0.0)), 0.0)
    q_decay = jnp.exp(log_gamma[:, None] * (j + 1.0))
    k_decay = jnp.exp(log_gamma[:, None] * (L - 1.0 - j))
    state_decay = jnp.exp(log_gamma * L)
    return decay_mat, q_decay, k_decay, state_decay


def _rope_tables(pos):
    half = DKB // 2
    freqs = ROPE_BASE ** (-jnp.arange(half, dtype=F32) / half)
    ang = pos.astype(F32)[:, None] * freqs[None, :]
    cos, sin = jnp.cos(ang), jnp.sin(ang)
    return jnp.concatenate([cos, cos], axis=1), jnp.concatenate([-sin, sin], axis=1)


def _ab_scan(ya, yg, yb, b_gate, g_a, g_b, batch, seq):
    L = CHUNK
    nc = seq // L
    cosf, sinf = _rope_tables(jnp.arange(seq))
    dmat, qdec, kdec, sdec = _retention_consts(L)
    qdec = jnp.broadcast_to(qdec[:, :, None], (HB, L, DVB))
    kdec = jnp.broadcast_to(kdec[:, :, None], (HB, L, DVB))
    smem = pl.BlockSpec(memory_space=pltpu.SMEM)
    rows = lambda b, c: (b * nc + c, 0)
    const3 = lambda b, c: (0, 0, 0)
    state = lambda b, c: (b, 0, 0, 0)
    m = batch * seq
    return pl.pallas_call(
        _ab_scan_body,
        grid=(batch, nc),
        in_specs=[
            smem, smem,
            pl.BlockSpec((L, AB_A_W), rows),
            pl.BlockSpec((L, LANE), rows),
            pl.BlockSpec((L, AB_B_W), rows),
            pl.BlockSpec((L, DKB), lambda b, c: (c, 0)),
            pl.BlockSpec((L, DKB), lambda b, c: (c, 0)),
            pl.BlockSpec((HB, L, L), const3),
            pl.BlockSpec((HB, L, DVB), const3),
            pl.BlockSpec((HB, L, DVB), const3),
            pl.BlockSpec((HA, 1, DVA), const3),
            pl.BlockSpec((HB, 1, DVB), const3),
        ],
        out_specs=[
            pl.BlockSpec((L, HA * DVA), rows),
            pl.BlockSpec((L, HB * DVB), rows),
            pl.BlockSpec((None, HA, DVA, DKA), state),
            pl.BlockSpec((None, HA, 1, DKA), state),
            pl.BlockSpec((None, HA, 1, LANE), state),
            pl.BlockSpec((None, HB, DKB, DVB), state),
        ],
        out_shape=[
            jax.ShapeDtypeStruct((m, HA * DVA), BF16),
            jax.ShapeDtypeStruct((m, HB * DVB), BF16),
            jax.ShapeDtypeStruct((batch, HA, DVA, DKA), F32),
            jax.ShapeDtypeStruct((batch, HA, 1, DKA), F32),
            jax.ShapeDtypeStruct((batch, HA, 1, LANE), F32),
            jax.ShapeDtypeStruct((batch, HB, DKB, DVB), F32),
        ],
        scratch_shapes=[
            pltpu.VMEM((HA, DKA, DVA), F32),
            pltpu.VMEM((HA, 1, DKA), F32),
            pltpu.VMEM((HA, 1, LANE), F32),
            pltpu.VMEM((HB, DKB, DVB), F32),
        ],
        compiler_params=_cp(("parallel", "arbitrary"), 40),
        name="ab_scan",
    )(sdec, b_gate, ya, yg, yb, cosf, sinf, dmat, qdec, kdec,
      g_a.reshape(HA, 1, DVA), g_b.reshape(HB, 1, DVB))


def _lane_bcast_col(row_vec):
    n = row_vec.shape[1]
    return jnp.broadcast_to(row_vec, (LANE, n)).T


def _ab_step_body(sdec_ref, bg_ref, ya_ref, yg_ref, yb_ref, cos_ref, sin_ref, ga_ref, gb_ref,
                  c_ref, n_ref, m_ref, s_ref, mix_ref, c_out, n_out, m_out, s_out):
    ya = ya_ref[...]
    yg = yg_ref[...]
    yb = yb_ref[...]
    cosf = cos_ref[...]
    sinf = sin_ref[...]

    for h in range(HA):
        q = ya[:, h * DKA:(h + 1) * DKA]
        k = ya[:, HA * DKA + h * DKA:HA * DKA + (h + 1) * DKA] * (DKA ** -0.5)
        v = ya[:, 2 * HA * DKA + h * DVA:2 * HA * DKA + (h + 1) * DVA]
        o = ya[:, 2 * HA * DKA + HA * DVA + h * DVA:2 * HA * DKA + HA * DVA + (h + 1) * DVA]
        i_pre = yg[:, h:h + 1] + bg_ref[h]
        log_f = _log_sigmoid(yg[:, HA + h:HA + h + 1] + bg_ref[HA + h])
        m_prev = m_ref[h][:, 0:1]
        g_in = log_f + m_prev
        m = jnp.maximum(g_in, i_pre)
        w_in = jnp.exp(i_pre - m)
        inter = jnp.exp(g_in - m)
        s = jnp.sum(q * k, axis=1, keepdims=True) * w_in
        cmat = c_ref[h]
        cq = jnp.sum(cmat * q, axis=1, keepdims=True)
        cq_row = jnp.broadcast_to(cq, (DVA, LANE)).T[0:1, :]
        num = s * v + inter * cq_row
        n_prev = n_ref[h]
        den = s + inter * jnp.sum(n_prev * q, axis=1, keepdims=True)
        hrow = num / jnp.maximum(jnp.abs(den), jnp.exp(-m))
        c_out[h] = inter * cmat + (w_in * _lane_bcast_col(v)) * k
        n_out[h] = inter * n_prev + w_in * k
        m_out[h] = jnp.broadcast_to(m, (1, LANE))
        ha = hrow * lax.rsqrt(jnp.mean(hrow * hrow, axis=1, keepdims=True) + EPS) * ga_ref[h]
        mix_ref[:, h * DVA:(h + 1) * DVA] = ha * _sigmoid(o)

    base = HA * DVA
    for h in range(HB):
        qr = _rope(yb[:, h * DKB:(h + 1) * DKB], cosf, sinf)
        kr = _rope(yb[:, HB * DKB + h * DKB:HB * DKB + (h + 1) * DKB], cosf, sinf) * (DKB ** -0.5)
        vb = yb[:, 2 * HB * DKB + h * DVB:2 * HB * DKB + (h + 1) * DVB]
        gate = yb[:, 2 * HB * DKB + HB * DVB + h * DVB:2 * HB * DKB + HB * DVB + (h + 1) * DVB]
        gamma = sdec_ref[h]
        smat = s_ref[h]
        q_col = _lane_bcast_col(qr)
        k_col = _lane_bcast_col(kr)
        qs = jnp.concatenate([jnp.sum(q_col * smat[:, t * LANE:(t + 1) * LANE], axis=0, keepdims=True)
                              for t in range(DVB // LANE)], axis=1)
        out = jnp.sum(qr * kr, axis=1, keepdims=True) * vb + gamma * qs
        s_out[h] = gamma * smat + jnp.concatenate([k_col * vb[:, t * LANE:(t + 1) * LANE]
                                                   for t in range(DVB // LANE)], axis=1)
        mu = jnp.mean(out, axis=1, keepdims=True)
        var = jnp.mean(jnp.square(out - mu), axis=1, keepdims=True)
        hb = (out - mu) * lax.rsqrt(var + EPS) * gb_ref[h]
        mix_ref[:, base + h * DVB:base + (h + 1) * DVB] = hb * (gate * _sigmoid(gate))


def _ab_step(ya, yg, yb, b_gate, g_a, g_b, c0, n0, m0, s0, pos0):
    bs = ya.shape[0]
    cosf, sinf = _rope_tables(pos0 + jnp.arange(1))
    _, _, _, sdec = _retention_consts(1)
    smem = pl.BlockSpec(memory_space=pltpu.SMEM)
    full = lambda shape: pl.BlockSpec(shape, lambda b: tuple(0 for _ in shape))
    row = lambda width: pl.BlockSpec((None, 1, width), lambda b: (b, 0, 0))
    state = lambda b: (b, 0, 0, 0)
    mix_w = HA * DVA + HB * DVB
    outs = pl.pallas_call(
        _ab_step_body,
        grid=(bs,),
        in_specs=[
            smem, smem,
            row(AB_A_W), row(LANE), row(AB_B_W),
            full((1, DKB)), full((1, DKB)),
            full((HA, 1, DVA)), full((HB, 1, DVB)),
            pl.BlockSpec((None, HA, DVA, DKA), state),
            pl.BlockSpec((None, HA, 1, DKA), state),
            pl.BlockSpec((None, HA, 1, LANE), state),
            pl.BlockSpec((None, HB, DKB, DVB), state),
        ],
        out_specs=[
            row(mix_w),
            pl.BlockSpec((None, HA, DVA, DKA), state),
            pl.BlockSpec((None, HA, 1, DKA), state),
            pl.BlockSpec((None, HA, 1, LANE), state),
            pl.BlockSpec((None, HB, DKB, DVB), state),
        ],
        out_shape=[
            jax.ShapeDtypeStruct((bs, 1, mix_w), F32),
            jax.ShapeDtypeStruct((bs, HA, DVA, DKA), F32),
            jax.ShapeDtypeStruct((bs, HA, 1, DKA), F32),
            jax.ShapeDtypeStruct((bs, HA, 1, LANE), F32),
            jax.ShapeDtypeStruct((bs, HB, DKB, DVB), F32),
        ],
        compiler_params=_cp(("parallel",), 32),
        name="ab_step",
    )(sdec, b_gate, ya.reshape(bs, 1, AB_A_W), yg.reshape(bs, 1, LANE), yb.reshape(bs, 1, AB_B_W), cosf, sinf,
      g_a.reshape(HA, 1, DVA), g_b.reshape(HB, 1, DVB),
      c0, n0.reshape(bs, HA, 1, DKA), jnp.broadcast_to(m0[:, :, None, None], (bs, HA, 1, LANE)), s0)
    return (outs[0].reshape(bs, mix_w),) + tuple(outs[1:])


def _bucket_np(dist):
    exact = N_BUCKETS // 2
    dist = np.maximum(dist, 0)
    ratio = np.maximum(dist, 1).astype(np.float32) / np.float32(exact)
    log_ratio = np.log(ratio) / np.float32(math.log(MAX_DISTANCE / exact))
    large = np.minimum(exact + (log_ratio * (N_BUCKETS - exact)).astype(np.int32), N_BUCKETS - 1)
    return np.where(dist < exact, dist, large).astype(np.int32)


def _bucket_thresholds():
    table = _bucket_np(np.arange(4 * MAX_DISTANCE))
    return [int(np.argmax(table >= k)) for k in range(1, N_BUCKETS)]


def _bias_table_body(rb_ref, idx_ref, o_ref):
    idx = idx_ref[...]
    for h in range(HC):
        acc = jnp.zeros(idx.shape, F32)
        for bkt in range(N_BUCKETS):
            acc = jnp.where(idx == bkt, rb_ref[bkt, h] * LOG2E, acc)
        o_ref[h] = acc


def _bias_table(rel_bias):
    t = np.arange(ATT_QB)[None, :]
    w = np.arange(ATT_W)[:, None]
    idx = _bucket_np(t + ATT_D0 - w)
    return pl.pallas_call(
        _bias_table_body,
        in_specs=[pl.BlockSpec(memory_space=pltpu.SMEM), pl.BlockSpec((ATT_W, ATT_QB), lambda: (0, 0))],
        out_specs=pl.BlockSpec((HC, ATT_W, ATT_QB), lambda: (0, 0, 0)),
        out_shape=jax.ShapeDtypeStruct((HC, ATT_W, ATT_QB), F32),
        name="bias_table",
    )(rel_bias, jnp.asarray(idx))


def _to_key(x):
    bits = lax.bitcast_convert_type(x, I32)
    return bits ^ ((bits >> 31) & jnp.int32(0x7FFFFFFF))


def _fold_rows(x, op):
    chains = 4
    acc = [x[r * 8:(r + 1) * 8, :] for r in range(chains)]
    for r in range(chains, x.shape[0] // 8):
        acc[r % chains] = op(acc[r % chains], x[r * 8:(r + 1) * 8, :])
    return op(op(acc[0], acc[1]), op(acc[2], acc[3]))


def _attn_body(q_ref, qi_ref, wi_ref, ki_ref, k_ref, vt_ref, tab_ref, o_ref,
               key_scr, hi_scr, lo_scr, thr_scr, mask_scr, lg_scr, p_scr, m_scr, l_scr, acc_scr, *, topk):
    qb = pl.program_id(1)
    kc = pl.program_id(2)
    nq, kcs = ATT_QB, ATT_KC
    last = (qb * nq) // kcs
    sub = LANE

    @pl.when(kc == 0)
    def _():
        wi_t = wi_ref[...].T
        t_idx = qb * nq + lax.broadcasted_iota(I32, (kcs, nq), 1)
        score_scr = lg_scr.at[0]

        def score(c, carry):
            base = pl.multiple_of(c * kcs, kcs)
            ki = ki_ref[pl.ds(base, kcs), :][:, 0:IDX_DIM]
            for h in range(0, IDX_HEADS, 2):
                part = None
                for hh in (h, h + 1):
                    lg = _dot_nt(ki, qi_ref[:, hh * IDX_DIM:(hh + 1) * IDX_DIM])
                    term = wi_t[IDX_DIM + hh:IDX_DIM + hh + 1, :] * jnp.maximum(lg, 0.0)
                    part = term if part is None else part + term
                score_scr[...] = part if h == 0 else score_scr[...] + part
            s_idx = base + lax.broadcasted_iota(I32, (kcs, nq), 0)
            key = jnp.where(s_idx <= t_idx, _to_key(score_scr[...] + 0.0), jnp.int32(INT_MIN))
            key_scr[c] = key
            hi_scr[c] = (key >> 16).astype(I16)
            lo_scr[c] = ((key & 0xFFFF) - 32768).astype(I16)
            return carry

        lax.fori_loop(0, last + 1, score, 0)

        def count_ge(plane_scr, cand):
            cand_tile = jnp.broadcast_to(cand, (BF16_ROWS, nq)).astype(I16)

            def count(c, a):
                x = plane_scr[c]
                ge = [jnp.where(x[r * BF16_ROWS:(r + 1) * BF16_ROWS, :] >= cand_tile, jnp.int16(1), jnp.int16(0))
                      for r in range(kcs // BF16_ROWS)]
                chains = 4
                for r in range(chains, len(ge)):
                    ge[r % chains] = ge[r % chains] + ge[r]
                return a + ((ge[0] + ge[1]) + (ge[2] + ge[3]))

            a = lax.fori_loop(0, last + 1, count, jnp.zeros((BF16_ROWS, nq), I16))
            return jnp.sum(a.astype(F32), axis=0, keepdims=True)

        def bisect16(plane_scr, n_above):
            def step(i, thr):
                cand = thr + lax.shift_left(jnp.int32(1), 15 - i)
                return jnp.where(n_above + count_ge(plane_scr, cand) >= topk, cand, thr)
            return lax.fori_loop(0, 16, step, jnp.full((1, nq), -32768, I32))

        thr_hi = bisect16(hi_scr, 0.0)
        n_gt = jnp.where(thr_hi < 32767, count_ge(hi_scr, jnp.minimum(thr_hi + 1, 32767)), 0.0)
        hi_tile = jnp.broadcast_to(thr_hi, (BF16_ROWS, nq)).astype(I16)

        def mask_low(c, carry):
            for r in range(kcs // BF16_ROWS):
                rows = slice(r * BF16_ROWS, (r + 1) * BF16_ROWS)
                lo_scr[c, rows, :] = jnp.where(hi_scr[c, rows, :] == hi_tile, lo_scr[c, rows, :], jnp.int16(-32768))
            return carry

        lax.fori_loop(0, last + 1, mask_low, 0)
        thr_lo = bisect16(lo_scr, n_gt)
        thr = jnp.maximum(thr_hi * 65536 + (thr_lo + 32768), jnp.int32(INT_MIN + 1))
        thr_scr[...] = thr

        def count32(pred):
            def count(c, a):
                return a + _fold_rows(jnp.where(pred(key_scr[c]), 1.0, 0.0), jnp.add)
            a = lax.fori_loop(0, last + 1, count, jnp.zeros((8, nq), F32))
            return jnp.sum(a, axis=0, keepdims=True)

        surplus = jnp.max(count32(lambda k: k >= thr)) > topk

        @pl.when(surplus)
        def _():
            n_gt = count32(lambda k: k > thr)
            earlier = jnp.where(lax.broadcasted_iota(I32, (kcs, kcs), 1) < lax.broadcasted_iota(I32, (kcs, kcs), 0),
                                1.0, 0.0).astype(BF16)

            def drop_surplus(c, seen):
                key = key_scr[c]
                tied = key == thr
                tied_f = jnp.where(tied, 1.0, 0.0)
                rank = _dot(earlier, tied_f.astype(BF16)) + seen
                key_scr[c] = jnp.where(tied & (n_gt + rank >= topk), jnp.int32(INT_MIN), key)
                return seen + jnp.sum(_fold_rows(tied_f, jnp.add), axis=0, keepdims=True)

            lax.fori_loop(0, last + 1, drop_surplus, jnp.zeros((1, nq), F32))
        m_scr[...] = jnp.full_like(m_scr, NEG)
        l_scr[...] = jnp.zeros_like(l_scr)
        acc_scr[...] = jnp.zeros_like(acc_scr)

    @pl.when(kc <= last)
    def _():
        mask_scr[...] = jnp.where(key_scr[kc] >= thr_scr[...], 0.0, NEG)
        w0 = pl.multiple_of(jnp.maximum(ATT_D0 - qb * nq + kc * kcs, 0), LANE)
        m_new = []
        for h in range(HC):
            hs = slice(h * DHC, (h + 1) * DHC)
            lg = (_dot_nt(k_ref[:, hs], q_ref[:, hs]) * (DHC ** -0.5 * LOG2E) + tab_ref[h, pl.ds(w0, kcs), :]
                  + mask_scr[...])
            lg_scr[h] = lg
            m_new.append(jnp.maximum(m_scr[h], jnp.max(_fold_rows(lg, jnp.maximum), axis=0, keepdims=True)))
        alpha = []
        for h in range(HC):
            p = jnp.exp2(lg_scr[h] - m_new[h])
            p_scr[h] = p.astype(BF16)
            alpha.append(jnp.exp2(m_scr[h] - m_new[h]))
            l_scr[h] = alpha[h] * l_scr[h] + jnp.sum(_fold_rows(p, jnp.add), axis=0, keepdims=True)
            m_scr[h] = m_new[h]
        for h in range(HC):
            hs = slice(h * DHC, (h + 1) * DHC)
            acc_scr[hs, :] = alpha[h] * acc_scr[hs, :] + _dot(vt_ref[hs, :], p_scr[h])

    @pl.when(kc == last)
    def _():
        for h in range(HC):
            hs = slice(h * DHC, (h + 1) * DHC)
            o_ref[:, hs] = (acc_scr[hs, :] / l_scr[h]).T.astype(o_ref.dtype)


def _transpose_body(x_ref, o_ref):
    o_ref[...] = x_ref[...].T


def _transpose_v(ya16, batch, seq):
    hd = HC * DHC
    tt = 512
    nt = seq // tt
    return pl.pallas_call(
        _transpose_body,
        grid=(batch, nt),
        in_specs=[pl.BlockSpec((tt, hd), lambda b, t: (b * nt + t, 2))],
        out_specs=pl.BlockSpec((None, hd, tt), lambda b, t: (b, 0, t)),
        out_shape=jax.ShapeDtypeStruct((batch, hd, seq), BF16),
        compiler_params=_cp(("parallel", "parallel"), 32),
        name="transpose_v",
    )(ya16)


def _attn_prompt(ya16, yg, yg16, vt16, table, batch, seq):
    nqb = seq // ATT_QB
    nkc = seq // ATT_KC
    topk = min(TOPK_MAX, seq // 4)
    kv_blk = lambda qb, kc: jnp.minimum(kc, (qb * ATT_QB) // ATT_KC)
    q_row = lambda b, qb, kc: b * nqb + qb
    hd = HC * DHC
    return pl.pallas_call(
        functools.partial(_attn_body, topk=topk),
        grid=(batch, nqb, nkc),
        in_specs=[
            pl.BlockSpec((ATT_QB, hd), lambda b, qb, kc: (q_row(b, qb, kc), 0)),
            pl.BlockSpec((ATT_QB, IDX_HEADS * IDX_DIM), lambda b, qb, kc: (q_row(b, qb, kc), 3)),
            pl.BlockSpec((ATT_QB, LANE), lambda b, qb, kc: (q_row(b, qb, kc), 0)),
            pl.BlockSpec((seq, LANE), lambda b, qb, kc: (b, 0)),
            pl.BlockSpec((ATT_KC, hd), lambda b, qb, kc: (b * nkc + kv_blk(qb, kc), 1)),
            pl.BlockSpec((None, hd, ATT_KC), lambda b, qb, kc: (b, 0, kv_blk(qb, kc))),
            pl.BlockSpec((HC, ATT_W, ATT_QB), lambda b, qb, kc: (0, 0, 0)),
        ],
        out_specs=pl.BlockSpec((ATT_QB, hd), lambda b, qb, kc: (q_row(b, qb, kc), 0)),
        out_shape=jax.ShapeDtypeStruct((batch * seq, hd), BF16),
        scratch_shapes=[
            pltpu.VMEM((nkc, ATT_KC, ATT_QB), I32),
            pltpu.VMEM((nkc, ATT_KC, ATT_QB), I16),
            pltpu.VMEM((nkc, ATT_KC, ATT_QB), I16),
            pltpu.VMEM((1, ATT_QB), I32),
            pltpu.VMEM((ATT_KC, ATT_QB), F32),
            pltpu.VMEM((HC, ATT_KC, ATT_QB), F32),
            pltpu.VMEM((HC, ATT_KC, ATT_QB), BF16),
            pltpu.VMEM((HC, 1, ATT_QB), F32),
            pltpu.VMEM((HC, 1, ATT_QB), F32),
            pltpu.VMEM((hd, ATT_QB), F32),
        ],
        compiler_params=_cp(("parallel", "arbitrary", "arbitrary"), 56),
        name="attn_prompt",
    )(ya16, ya16, yg, yg16, ya16, vt16, table)


def _conv_body(yb_ref, cw_ref, cb_ref, gn_ref, bn_ref, o_ref, st_ref, ubuf, vbuf, *, tt):
    t = pl.program_id(1)
    halo = CONV_HALO
    rb = 32
    sl = 8
    first = halo - (CONV_W - 1)
    span = tt + halo - sl

    @pl.when(t == 0)
    def _():
        ubuf[0:halo, :] = jnp.zeros((halo, D_CONV), F32)

    @pl.when(t > 0)
    def _():
        ubuf[0:halo, :] = ubuf[tt:tt + halo, :]

    ubuf[halo:halo + tt, :] = yb_ref[:, 0:D_CONV] * _sigmoid(yb_ref[:, D_CONV:2 * D_CONV])
    step = 40
    for r in range(1, sl):
        for c0 in range(0, span, step):
            vbuf[r - 1, c0:c0 + step, :] = ubuf[c0 + r:c0 + r + step, :]
    for r in range(tt // rb):
        acc = jnp.broadcast_to(cb_ref[...], (rb, D_CONV))
        for w in range(CONV_W):
            a, res = divmod(first + w, sl)
            off = a * sl + r * rb
            src = ubuf[off:off + rb, :] if res == 0 else vbuf[res - 1, off:off + rb, :]
            acc = acc + src * cw_ref[w:w + 1, :]
        mu = jnp.mean(acc, axis=1, keepdims=True)
        var = jnp.mean(jnp.square(acc - mu), axis=1, keepdims=True)
        yn = (acc - mu) * lax.rsqrt(var + EPS) * gn_ref[...] + bn_ref[...]
        o_ref[r * rb:(r + 1) * rb, :] = (yn * _sigmoid(yn)).astype(o_ref.dtype)

    @pl.when(t == pl.num_programs(1) - 1)
    def _():
        st_ref[...] = ubuf[halo + tt - (CONV_W - 1):halo + tt, :]


def _conv_prompt(yb, cw, cb, gn, bn, batch, seq):
    tt = 256
    nt = seq // tt
    const = lambda b, t: (0, 0)
    return pl.pallas_call(
        functools.partial(_conv_body, tt=tt),
        grid=(batch, nt),
        in_specs=[
            pl.BlockSpec((tt, 2 * D_CONV), lambda b, t: (b * nt + t, 0)),
            pl.BlockSpec((CONV_W, D_CONV), const),
            pl.BlockSpec((1, D_CONV), const),
            pl.BlockSpec((1, D_CONV), const),
            pl.BlockSpec((1, D_CONV), const),
        ],
        out_specs=[
            pl.BlockSpec((tt, D_CONV), lambda b, t: (b * nt + t, 0)),
            pl.BlockSpec((None, CONV_W - 1, D_CONV), lambda b, t: (b, 0, 0)),
        ],
        out_shape=[
            jax.ShapeDtypeStruct((batch * seq, D_CONV), BF16),
            jax.ShapeDtypeStruct((batch, CONV_W - 1, D_CONV), F32),
        ],
        scratch_shapes=[pltpu.VMEM((CONV_HALO + tt, D_CONV), F32),
                        pltpu.VMEM((7, tt + CONV_HALO - 8, D_CONV), F32)],
        compiler_params=_cp(("parallel", "arbitrary"), 32),
        name="conv_prompt",
    )(yb, cw, cb.reshape(1, D_CONV), gn.reshape(1, D_CONV), bn.reshape(1, D_CONV))


def _rows_to_sublanes(row_vec, n, width, start=0):
    return jnp.concatenate([row_vec[:, start + i * width:start + (i + 1) * width] for i in range(n)], axis=0)


def _select_body(pt_ref, ya_ref, yg_ref, pool_ref, pos_ref, pbuf, sc_scr, slot_scr, sem, *, layer, n_pages, topk):
    b = pl.program_id(0)
    group = 16

    def page_copy(p):
        return pltpu.make_async_copy(pool_ref.at[layer, pt_ref[b, p]], pbuf.at[p], sem.at[0])

    def start(p, carry):
        page_copy(p).start()
        return carry

    def wait(p, carry):
        page_copy(p).wait()
        return carry

    lax.fori_loop(0, n_pages, start, 0)
    lax.fori_loop(0, n_pages, wait, 0)

    q_row = ya_ref[...]
    g_row = yg_ref[...]
    qi = _rows_to_sublanes(q_row, IDX_HEADS, IDX_DIM, start=3 * HC * DHC).astype(BF16)
    wi = _rows_to_sublanes(g_row, IDX_HEADS, 1, start=IDX_DIM)
    ki_new = g_row[:, 0:IDX_DIM].astype(BF16)

    for g in range(n_pages // group):
        keys = jnp.concatenate([pbuf[g * group + p] for p in range(group)], axis=1).astype(BF16)
        lg = _dot(qi, keys)
        sc = jnp.sum(jnp.maximum(lg, 0.0) * wi, axis=0, keepdims=True)
        for p in range(group):
            sc_scr[g * group + p:g * group + p + 1, :] = sc[:, p * PAGE_SIZE:(p + 1) * PAGE_SIZE]
    lg_new = jnp.sum(qi.astype(F32) * ki_new.astype(F32), axis=1, keepdims=True)
    sc_new = jnp.sum(jnp.maximum(lg_new, 0.0) * wi, axis=0, keepdims=True)
    key = _to_key(sc_scr[...] + 0.0)
    key_new = _to_key(sc_new + 0.0)

    def count(mask, mask_new):
        return jnp.sum(jnp.sum(jnp.where(mask, 1.0, 0.0), axis=1, keepdims=True), axis=0, keepdims=True) \
            + jnp.where(mask_new, 1.0, 0.0)

    def bisect(i, thr):
        cand = thr + lax.shift_left(jnp.int32(1), 31 - i)
        return jnp.where(count(key >= cand, key_new >= cand) >= topk, cand, thr)

    thr = lax.fori_loop(0, 32, bisect, jnp.full((1, 1), INT_MIN, I32))

    r_i = lax.broadcasted_iota(I32, (PAGE_SIZE, PAGE_SIZE), 0)
    c_i = lax.broadcasted_iota(I32, (PAGE_SIZE, PAGE_SIZE), 1)
    upper = jnp.where(r_i <= c_i, 1.0, 0.0).astype(BF16)
    pr = lax.broadcasted_iota(I32, (n_pages, n_pages), 0)
    pc = lax.broadcasted_iota(I32, (n_pages, n_pages), 1)
    before = jnp.where(pc < pr, 1.0, 0.0).astype(BF16)

    def rank(mask):
        inc = _dot(jnp.where(mask, 1.0, 0.0).astype(BF16), upper)
        tot = jnp.broadcast_to(inc[:, PAGE_SIZE - 1:PAGE_SIZE], (n_pages, PAGE_SIZE))
        return inc + _dot(before, tot.astype(BF16))

    gt = key > thr
    eq = key == thr
    n_gt = count(gt, key_new > thr)
    slot_eq = n_gt + rank(eq) - 1.0
    slot = jnp.where(gt, rank(gt) - 1.0, jnp.where(eq & (slot_eq < topk), slot_eq, -1.0))
    slot_scr[...] = slot

    r_iota = lax.broadcasted_iota(I32, (topk, PAGE_SIZE), 0).astype(F32)
    lane = lax.broadcasted_iota(I32, (1, PAGE_SIZE), 1).astype(F32)

    def compact(p, acc):
        val = lax.convert_element_type(p * PAGE_SIZE + 1, F32) + lane
        return acc + jnp.where(slot_scr[pl.ds(p, 1), :] == r_iota, val, 0.0)

    acc = lax.fori_loop(0, n_pages, compact, jnp.zeros((topk, PAGE_SIZE), F32))
    pos = jnp.sum(acc, axis=1, keepdims=True) - 1.0
    pos_ref[...] = jnp.broadcast_to(pos, (topk, PAGE_SIZE)).astype(I32)


def _select_sample(ya, yg, pool_idx, page_table, layer, topk):
    bs = ya.shape[0]
    n_pages = page_table.shape[1]
    row = lambda width: pl.BlockSpec((None, 1, width), lambda b, pt: (b, 0, 0))
    out = pl.pallas_call(
        functools.partial(_select_body, layer=layer, n_pages=n_pages, topk=topk),
        grid_spec=pltpu.PrefetchScalarGridSpec(
            num_scalar_prefetch=1,
            grid=(bs,),
            in_specs=[row(ya.shape[1]), row(yg.shape[1]), pl.BlockSpec(memory_space=pl.ANY)],
            out_specs=pl.BlockSpec((None, topk, PAGE_SIZE), lambda b, pt: (b, 0, 0)),
            scratch_shapes=[
                pltpu.VMEM((n_pages, IDX_DIM, PAGE_SIZE), F32),
                pltpu.VMEM((n_pages, PAGE_SIZE), F32),
                pltpu.VMEM((n_pages, PAGE_SIZE), F32),
                pltpu.SemaphoreType.DMA((1,)),
            ],
        ),
        out_shape=jax.ShapeDtypeStruct((bs, topk, PAGE_SIZE), I32),
        compiler_params=_cp(("arbitrary",), 40),
        name="select_sample",
    )(page_table, ya.reshape(bs, 1, -1), yg.reshape(bs, 1, -1), pool_idx)
    return out[:, :, 0]


def _sattn_body(pos_ref, pt_ref, ya_ref, posv_ref, rb_ref, kpool, vpool, o_ref, kbuf, vbuf, sem,
                *, layer, past, topk, thresholds):
    b = pl.program_id(0)

    def row_copies(r):
        pos = jnp.maximum(pos_ref[b, r], 0)
        page = pt_ref[b, pos // PAGE_SIZE]
        off = pos % PAGE_SIZE
        return (pltpu.make_async_copy(kpool.at[layer, page, off], kbuf.at[r], sem.at[0]),
                pltpu.make_async_copy(vpool.at[layer, page, off], vbuf.at[r], sem.at[1]))

    def start(r, carry):
        @pl.when(pos_ref[b, r] >= 0)
        def _():
            ck, cv = row_copies(r)
            ck.start()
            cv.start()
        return carry

    lax.fori_loop(0, topk, start, 0)

    y_row = ya_ref[...]
    k_new = _rows_to_sublanes(y_row, HC, DHC, start=HC * DHC)
    v_new = _rows_to_sublanes(y_row, HC, DHC, start=2 * HC * DHC)

    def wait(r, carry):
        @pl.when(pos_ref[b, r] >= 0)
        def _():
            ck, cv = row_copies(r)
            ck.wait()
            cv.wait()

        @pl.when(pos_ref[b, r] < 0)
        def _():
            kbuf[r] = k_new
            vbuf[r] = v_new
        return carry

    lax.fori_loop(0, topk, wait, 0)

    posv = posv_ref[...]
    dist = jnp.where(posv < 0, 0, past - posv)
    bucket = jnp.zeros(dist.shape, I32)
    for thr in thresholds:
        bucket = bucket + jnp.where(dist >= thr, 1, 0)
    for h in range(HC):
        hs = slice(h * DHC, (h + 1) * DHC)
        bias = jnp.zeros(dist.shape, F32)
        for bkt in range(N_BUCKETS):
            bias = jnp.where(bucket == bkt, rb_ref[bkt, h], bias)
        q16 = jnp.broadcast_to(y_row[:, hs], (BF16_ROWS, DHC)).astype(BF16)
        lg = _dot_nt(q16, kbuf[:, h, :].astype(BF16)) * (DHC ** -0.5) + bias
        p = jnp.exp(lg - jnp.max(lg, axis=1, keepdims=True))
        p = p / jnp.sum(p, axis=1, keepdims=True)
        out = _dot(p.astype(BF16), vbuf[:, h, :].astype(BF16))
        o_ref[:, hs] = out[0:1, :]


def _attn_sample(ya, pos, rel_bias, pool_k, pool_v, page_table, layer, topk):
    bs = ya.shape[0]
    past = page_table.shape[1] * PAGE_SIZE
    row = lambda width: pl.BlockSpec((None, 1, width), lambda b, p, pt: (b, 0, 0))
    out = pl.pallas_call(
        functools.partial(_sattn_body, layer=layer, past=past, topk=topk, thresholds=_bucket_thresholds()),
        grid_spec=pltpu.PrefetchScalarGridSpec(
            num_scalar_prefetch=2,
            grid=(bs,),
            in_specs=[row(ya.shape[1]), row(topk), pl.BlockSpec(memory_space=pltpu.SMEM),
                      pl.BlockSpec(memory_space=pl.ANY), pl.BlockSpec(memory_space=pl.ANY)],
            out_specs=row(HC * DHC),
            scratch_shapes=[
                pltpu.VMEM((topk, HC, DHC), F32),
                pltpu.VMEM((topk, HC, DHC), F32),
                pltpu.SemaphoreType.DMA((2,)),
            ],
        ),
        out_shape=jax.ShapeDtypeStruct((bs, 1, HC * DHC), F32),
        compiler_params=_cp(("arbitrary",), 32),
        name="attn_sample",
    )(pos, page_table, ya.reshape(bs, 1, -1), pos.reshape(bs, 1, topk), rel_bias, pool_k, pool_v)
    return out.reshape(bs, HC * DHC)


def _conv_step_body(yb_ref, st_ref, cw_ref, cb_ref, gn_ref, bn_ref, o_ref, st_out):
    y = yb_ref[...]
    u = y[:, 0:D_CONV] * _sigmoid(y[:, D_CONV:2 * D_CONV])
    nprev = CONV_W - 1
    acc = jnp.sum(st_ref[...] * cw_ref[0:nprev, :], axis=0, keepdims=True) + u * cw_ref[nprev:CONV_W, :] + cb_ref[...]
    mu = jnp.mean(acc, axis=1, keepdims=True)
    var = jnp.mean(jnp.square(acc - mu), axis=1, keepdims=True)
    yn = (acc - mu) * lax.rsqrt(var + EPS) * gn_ref[...] + bn_ref[...]
    o_ref[...] = yn * _sigmoid(yn)
    st_out[0:nprev - 1, :] = st_ref[1:nprev, :]
    st_out[nprev - 1:nprev, :] = u


def _conv_step(yb, state, cw, cb, gn, bn):
    bs = yb.shape[0]
    full = lambda shape: pl.BlockSpec(shape, lambda b: tuple(0 for _ in shape))
    st_spec = pl.BlockSpec((None, CONV_W - 1, D_CONV), lambda b: (b, 0, 0))
    row = lambda width: pl.BlockSpec((None, 1, width), lambda b: (b, 0, 0))
    out, st = pl.pallas_call(
        _conv_step_body,
        grid=(bs,),
        in_specs=[row(2 * D_CONV), st_spec, full((CONV_W, D_CONV)), full((1, D_CONV)), full((1, D_CONV)),
                  full((1, D_CONV))],
        out_specs=[row(D_CONV), st_spec],
        out_shape=[jax.ShapeDtypeStruct((bs, 1, D_CONV), F32),
                   jax.ShapeDtypeStruct((bs, CONV_W - 1, D_CONV), F32)],
        compiler_params=_cp(("parallel",), 32),
        name="conv_step",
    )(yb.reshape(bs, 1, 2 * D_CONV), state, cw, cb.reshape(1, D_CONV), gn.reshape(1, D_CONV),
      bn.reshape(1, D_CONV))
    return out.reshape(bs, D_CONV), st


def kernel(x_prompt, x_sample, state_mlstm_c, state_mlstm_n, state_mlstm_m, state_ret, cache_k, cache_v,
           cache_idx_k, state_conv, page_table, norm_ffn1, w_ffn1_in, w_ffn1_out, norm_mix, norm_ffn2,
           w_ffn2_in, w_ffn2_out, norm_final, w_ab_in, b_ab_gate, g_mlstm_norm, g_ret_norm, w_ab_out,
           w_cd_in, w_cd_out, rel_bias, conv_w, conv_b, g_conv_norm, b_conv_norm):
    batch, seq, d = x_prompt.shape
    bs = x_sample.shape[0]
    depth = norm_ffn1.shape[0]
    past = page_table.shape[1] * PAGE_SIZE
    hd = HC * DHC
    assert x_sample.shape[1] == 1 and seq % ATT_KC == 0 and d == D_MODEL
    table = _bias_table(rel_bias)
    w_ab_in_t = jnp.swapaxes(w_ab_in, 1, 2)
    w_cd_in_t = jnp.swapaxes(w_cd_in, 1, 2)
    pool_idx_t = jnp.swapaxes(cache_idx_k, 2, 3)
    in_proj = functools.partial(_matmul, w_transposed=True)

    x = x_prompt.reshape(batch * seq, d)
    xs = x_sample.reshape(bs, d)
    topk_s = min(TOPK_MAX, (past + 1) // 4)
    p_ab, s_ab, p_cd, s_cd = [], [], [], []

    def cd_state(ya, yg, conv_state, lead):
        return (ya[:, hd:2 * hd].reshape(lead + (HC, DHC)), ya[:, 2 * hd:3 * hd].reshape(lead + (HC, DHC)),
                yg[:, 0:IDX_DIM].reshape(lead + (IDX_DIM,)), conv_state)

    for l in range(depth):
        j = l // 2
        x, xs = _ffn(x, xs, norm_ffn1[l], w_ffn1_in, w_ffn1_out, l)
        xn = _rmsnorm(x, norm_mix[l], BF16)
        xsn = _rmsnorm(xs, norm_mix[l], F32)
        if l % 2 == 0:
            ya, ya_s = in_proj([xn], [xsn], w_ab_in_t, j, 0, AB_A_W, 512)
            yg, yg_s = in_proj([xn], [xsn], w_ab_in_t, j, AB_A_W, LANE, LANE)
            yb, yb_s = in_proj([xn], [xsn], w_ab_in_t, j, AB_B0, AB_B_W, 512)
            gates = (b_ab_gate[j], g_mlstm_norm[j], g_ret_norm[j])
            ma, mb, c, n, mm, s = _ab_scan(ya, yg, yb, *gates, batch, seq)
            p_ab.append((c, n[:, :, 0, :], mm[:, :, 0, 0], s))
            mix_s, c, n, mm, s = _ab_step(ya_s, yg_s, yb_s, *gates, state_mlstm_c[j], state_mlstm_n[j],
                                          state_mlstm_m[j], state_ret[j], past)
            s_ab.append((c, n[:, :, 0, :], mm[:, :, 0, 0], s))
            x, xs = _matmul([ma, mb], [mix_s], w_ab_out, j, 0, d, 512, res=x, s_res=xs)
        else:
            ya, ya16, ya_s = in_proj([xn], [xsn], w_cd_in_t, j, 0, CD_A_W, 512, out_dtypes=(F32, BF16))
            yg, yg16, yg_s = in_proj([xn], [xsn], w_cd_in_t, j, CD_A_W, LANE, LANE, out_dtypes=(F32, BF16))
            yb, yb_s = in_proj([xn], [xsn], w_cd_in_t, j, CD_B0, CD_B_W, 512)
            conv_args = (conv_w[j], conv_b[j], g_conv_norm[j], b_conv_norm[j])
            attn = _attn_prompt(ya16, yg, yg16, _transpose_v(ya16, batch, seq), table, batch, seq)
            conv, conv_state = _conv_prompt(yb, *conv_args, batch, seq)
            p_cd.append(cd_state(ya, yg, conv_state, (batch, seq)))
            pos = _select_sample(ya_s, yg_s, pool_idx_t, page_table, j, topk_s)
            attn_s = _attn_sample(ya_s, pos, rel_bias, cache_k, cache_v, page_table, j, topk_s)
            conv_s, conv_state_s = _conv_step(yb_s, state_conv[j], *conv_args)
            s_cd.append(cd_state(ya_s, yg_s, conv_state_s, (bs, 1)))
            x, xs = _matmul([attn, conv], [attn_s, conv_s], w_cd_out, j, 0, d, 512, res=x, s_res=xs)
        x, xs = _ffn(x, xs, norm_ffn2[l], w_ffn2_in, w_ffn2_out, l)
    y_p = _rmsnorm(x, norm_final, F32)
    y_s = _rmsnorm(xs, norm_final, F32)
    stack = lambda states: [jnp.stack(a) for a in zip(*states)]
    (pc, pn, pm, ps), (sc, sn, sm, ss) = stack(p_ab), stack(s_ab)
    (pk, pv, pik, pcv), (sk, sv, sik, scv) = stack(p_cd), stack(s_cd)
    return (y_p.reshape(batch, seq, d), y_s.reshape(bs, 1, d), pc, sc, pn, sn, pm, sm, ps, ss,
            pk, sk, pv, sv, pik, sik, pcv, scv)
```

```python
import functools
import math

import numpy as np
import jax
import jax.numpy as jnp
from jax import lax
from jax.experimental import pallas as pl
from jax.experimental.pallas import tpu as pltpu

F32 = jnp.float32
BF16 = jnp.bfloat16
I32 = jnp.int32
I16 = jnp.int16

D_MODEL = 2048
PAGE_SIZE = 128
HA, DKA, DVA = 4, 128, 256
HB, DKB, DVB = 4, 128, 256
CHUNK = 128
ROPE_BASE = 10000.0
HC, DHC = 8, 128
IDX_HEADS, IDX_DIM = 16, 64
TOPK_MAX = 256
QBLK = 128
N_BUCKETS = 32
MAX_DISTANCE = 128
D_CONV = 1024
CONV_W = 31
EPS = 1e-6

LANE = 128
BF16_ROWS = 16
MIB = 2 ** 20
INT_MIN = -(2 ** 31)
NEG = -1e30
LOG2E = 1.0 / math.log(2.0)
ATT_QB = 256
ATT_KC = 512
ATT_D0 = ATT_KC + LANE
ATT_W = 2 * ATT_KC + LANE
CONV_HALO = 32

AB_A_W = 2 * HA * DKA + 2 * HA * DVA
AB_B0 = AB_A_W + 2 * HA
AB_B_W = 2 * HB * DKB + 2 * HB * DVB
CD_A_W = 3 * HC * DHC + IDX_HEADS * IDX_DIM
CD_B0 = CD_A_W + IDX_DIM + IDX_HEADS
CD_B_W = 2 * D_CONV


def _cp(sem, vmem_mib):
    return pltpu.CompilerParams(dimension_semantics=sem, vmem_limit_bytes=vmem_mib * MIB)


def _sigmoid(x):
    return 1.0 / (1.0 + jnp.exp(-x))


def _log_sigmoid(x):
    return jnp.minimum(x, 0.0) - jnp.log1p(jnp.exp(-jnp.abs(x)))


def _dot(a, b):
    return jnp.dot(a, b, preferred_element_type=F32)


def _dot_nt(a, b):
    return lax.dot_general(a, b, (((1,), (1,)), ((), ())), preferred_element_type=F32)


def _pad_rows(a, rows):
    if a.shape[0] >= rows:
        return a
    return jnp.concatenate([a, jnp.zeros((rows - a.shape[0], a.shape[1]), a.dtype)], axis=0)


def _rms_body(x_ref, g_ref, o_ref):
    x = x_ref[...]
    y = x * lax.rsqrt(jnp.mean(x * x, axis=-1, keepdims=True) + EPS) * g_ref[...]
    o_ref[...] = y.astype(o_ref.dtype)


def _rmsnorm(x, g, out_dtype):
    m, d = x.shape
    tm = min(m, 512)
    return pl.pallas_call(
        _rms_body,
        grid=(m // tm,),
        in_specs=[pl.BlockSpec((tm, d), lambda i: (i, 0)), pl.BlockSpec((1, d), lambda i: (0, 0))],
        out_specs=pl.BlockSpec((tm, d), lambda i: (i, 0)),
        out_shape=jax.ShapeDtypeStruct((m, d), out_dtype),
        compiler_params=_cp(("parallel",), 32),
        name="rmsnorm",
    )(x, g.reshape(1, d))


def _ffn_body(x_ref, xs_ref, g_ref, wa_ref, wb_ref, wo_ref, o_ref, os_ref, xn_ref, xsn_ref):
    i = pl.program_id(0)
    j = pl.program_id(1)

    def normed(x):
        return x * lax.rsqrt(jnp.mean(x * x, axis=-1, keepdims=True) + EPS) * g_ref[...]

    @pl.when(j == 0)
    def _():
        x = x_ref[...]
        xn_ref[...] = normed(x).astype(BF16)
        o_ref[...] = x

    def half_swiglu(xn):
        a = _dot(xn, wa_ref[...].astype(BF16))
        b = _dot(xn, wb_ref[...].astype(BF16))
        return _dot((a * _sigmoid(a) * b * 0.5).astype(BF16), wo_ref[...].astype(BF16))

    o_ref[...] += half_swiglu(xn_ref[...])

    @pl.when(i == 0)
    def _():
        @pl.when(j == 0)
        def _():
            xs = xs_ref[...]
            xsn_ref[...] = _pad_rows(normed(xs), BF16_ROWS).astype(BF16)
            os_ref[...] = xs

        os_ref[...] += half_swiglu(xsn_ref[...])[:os_ref.shape[0]]


def _ffn(x, xs, g, w_in, w_out, layer):
    m, d = x.shape
    ms = xs.shape[0]
    f = w_out.shape[1]
    tm = min(m, 1024)
    tf = 512
    assert m % tm == 0 and f % tf == 0 and ms <= BF16_ROWS
    nf = f // tf
    return pl.pallas_call(
        _ffn_body,
        grid=(m // tm, nf),
        in_specs=[
            pl.BlockSpec((tm, d), lambda i, j: (i, 0), pipeline_mode=pl.Buffered(1)),
            pl.BlockSpec((ms, d), lambda i, j: (0, 0)),
            pl.BlockSpec((1, d), lambda i, j: (0, 0)),
            pl.BlockSpec((None, d, tf), lambda i, j: (layer, 0, j)),
            pl.BlockSpec((None, d, tf), lambda i, j: (layer, 0, j + nf)),
            pl.BlockSpec((None, tf, d), lambda i, j: (layer, j, 0)),
        ],
        out_specs=[pl.BlockSpec((tm, d), lambda i, j: (i, 0)), pl.BlockSpec((ms, d), lambda i, j: (0, 0))],
        out_shape=[jax.ShapeDtypeStruct((m, d), F32), jax.ShapeDtypeStruct((ms, d), F32)],
        scratch_shapes=[pltpu.VMEM((tm, d), BF16), pltpu.VMEM((BF16_ROWS, d), BF16)],
        compiler_params=_cp(("arbitrary", "arbitrary"), 60),
        name="ffn",
    )(x, xs, g.reshape(1, d), w_in, w_in, w_out)


def _mm_body(*refs, k_sizes, ks_sizes, w_transposed, has_res, n_out):
    n_a, n_s = len(k_sizes), len(ks_sizes)
    a_refs = refs[:n_a]
    s_refs = refs[n_a:n_a + n_s]
    w_ref = refs[n_a + n_s]
    pos = n_a + n_s + 1
    res_ref = sres_ref = None
    if has_res:
        res_ref, sres_ref = refs[pos], refs[pos + 1]
        pos += 2
    out_refs = refs[pos:pos + n_out]
    os_ref = refs[pos + n_out]
    wt_ref = refs[pos + n_out + 1]
    rows_per_step = 64

    def product(parts, sizes):
        acc = None
        k0 = 0
        for p_ref, ks in zip(parts, sizes):
            a = _pad_rows(p_ref[...], BF16_ROWS).astype(BF16)
            d = _dot_nt(a, wt_ref[:, k0:k0 + ks]) if w_transposed else _dot(a, wt_ref[k0:k0 + ks, :])
            acc = d if acc is None else acc + d
            k0 += ks
        return acc

    @pl.when(pl.program_id(1) == 0)
    def _():
        def cast_rows(c, carry):
            rows = pl.ds(pl.multiple_of(c * rows_per_step, rows_per_step), rows_per_step)
            if w_transposed:
                wt_ref[rows, :] = w_ref[0, rows, :].astype(BF16)
            else:
                wt_ref[rows, :] = w_ref[rows, :].astype(BF16)
            return carry

        lax.fori_loop(0, wt_ref.shape[0] // rows_per_step, cast_rows, 0)
        acc_s = product(s_refs, ks_sizes)[:os_ref.shape[0]]
        if has_res:
            acc_s = acc_s + sres_ref[...]
        os_ref[...] = acc_s

    acc = product(a_refs, k_sizes)
    if has_res:
        acc = acc + res_ref[...]
    for o_ref in out_refs:
        o_ref[...] = acc.astype(o_ref.dtype)


def _matmul(a_parts, s_parts, w, layer, col0, width, tn, res=None, s_res=None, out_dtypes=(F32,),
            w_transposed=False):
    m, ms = a_parts[0].shape[0], s_parts[0].shape[0]
    k_sizes = tuple(a.shape[1] for a in a_parts)
    ks_sizes = tuple(a.shape[1] for a in s_parts)
    k_total = sum(k_sizes)
    assert w.shape[2 if w_transposed else 1] == k_total and sum(ks_sizes) == k_total
    assert width % tn == 0 and tn % 64 == 0 and ms <= BF16_ROWS and (res is None) == (s_res is None)
    tm = min(m, 2048 if tn <= 512 else 1024)
    assert m % tm == 0
    in_specs = [pl.BlockSpec((tm, ks), lambda j, i: (i, 0)) for ks in k_sizes]
    in_specs += [pl.BlockSpec((ms, ks), lambda j, i: (0, 0)) for ks in ks_sizes]
    if w_transposed:
        assert col0 % 8 == 0
        in_specs.append(pl.BlockSpec((pl.Element(1), pl.Element(tn), pl.Element(k_total)),
                                     lambda j, i: (layer, (col0 // 8 + j * (tn // 8)) * 8, 0)))
        wt_shape = (tn, k_total)
    else:
        assert col0 % tn == 0 and tn % LANE == 0
        in_specs.append(pl.BlockSpec((None, k_total, tn), lambda j, i: (layer, 0, col0 // tn + j)))
        wt_shape = (k_total, tn)
    args = list(a_parts) + list(s_parts) + [w]
    if res is not None:
        in_specs += [pl.BlockSpec((tm, tn), lambda j, i: (i, j)), pl.BlockSpec((ms, tn), lambda j, i: (0, j))]
        args += [res, s_res]
    outs = pl.pallas_call(
        functools.partial(_mm_body, k_sizes=k_sizes, ks_sizes=ks_sizes, w_transposed=w_transposed,
                          has_res=res is not None, n_out=len(out_dtypes)),
        grid=(width // tn, m // tm),
        in_specs=in_specs,
        out_specs=[pl.BlockSpec((tm, tn), lambda j, i: (i, j)) for _ in out_dtypes]
        + [pl.BlockSpec((ms, tn), lambda j, i: (0, j))],
        out_shape=[jax.ShapeDtypeStruct((m, width), dt) for dt in out_dtypes]
        + [jax.ShapeDtypeStruct((ms, width), F32)],
        scratch_shapes=[pltpu.VMEM(wt_shape, BF16)],
        compiler_params=_cp(("parallel", "arbitrary"), 56),
        name="matmul",
    )(*args)
    return outs


def _rope(x, cosf, sinf):
    return x * cosf + pltpu.roll(x, shift=x.shape[-1] // 2, axis=1) * sinf


def _ab_scan_body(sdec_ref, bg_ref, ya_ref, yg_ref, yb_ref, cos_ref, sin_ref, dmat_ref, qdec_ref, kdec_ref,
                  ga_ref, gb_ref, ma_ref, mb_ref, c_out, n_out, m_out, s_out, ct_scr, n_scr, m_scr, s_scr):
    c = pl.program_id(1)
    L = CHUNK

    @pl.when(c == 0)
    def _():
        ct_scr[...] = jnp.zeros_like(ct_scr)
        n_scr[...] = jnp.zeros_like(n_scr)
        m_scr[...] = jnp.zeros_like(m_scr)
        s_scr[...] = jnp.zeros_like(s_scr)

    g = yg_ref[...]
    g_t = g.T
    row = lax.broadcasted_iota(I32, (L, L), 0)
    col = lax.broadcasted_iota(I32, (L, L), 1)
    lower = col <= row
    cosf = cos_ref[...]
    sinf = sin_ref[...]

    for h in range(HA):
        q = ya_ref[:, h * DKA:(h + 1) * DKA]
        k = ya_ref[:, HA * DKA + h * DKA:HA * DKA + (h + 1) * DKA] * (DKA ** -0.5)
        v = ya_ref[:, 2 * HA * DKA + h * DVA:2 * HA * DKA + (h + 1) * DVA]
        o = ya_ref[:, 2 * HA * DKA + HA * DVA + h * DVA:2 * HA * DKA + HA * DVA + (h + 1) * DVA]
        b_i = bg_ref[h]
        b_f = bg_ref[HA + h]
        i_col = g[:, h:h + 1] + b_i
        f_col = _log_sigmoid(g[:, HA + h:HA + h + 1] + b_f)
        i_row = g_t[h:h + 1, :] + b_i
        f_row = _log_sigmoid(g_t[HA + h:HA + h + 1, :] + b_f)
        b_row = jnp.sum(jnp.where(row <= col, f_col, 0.0), axis=0, keepdims=True)
        b_col = jnp.sum(jnp.where(lower, f_row, 0.0), axis=1, keepdims=True)
        d_log = jnp.where(lower, b_col - b_row + i_row, -jnp.inf)
        m_prev = m_scr[h][:, 0:1]
        g_in = b_col + m_prev
        m = jnp.maximum(g_in, jnp.max(d_log, axis=1, keepdims=True))
        k_t = k.T.astype(BF16)
        q16 = q.astype(BF16)
        s = _dot(q16, k_t) * jnp.exp(d_log - m)
        inter = jnp.exp(g_in - m)
        ct = ct_scr[h]
        num = _dot(s.astype(BF16), v.astype(BF16)) + inter * _dot(q16, ct.astype(BF16))
        n_prev = n_scr[h]
        den = jnp.sum(s, axis=1, keepdims=True) + inter * jnp.sum(q * n_prev, axis=1, keepdims=True)
        hh = num / jnp.maximum(jnp.abs(den), jnp.exp(-m))
        m_new = m[L - 1:L, :]
        b_last = b_col[L - 1:L, :]
        decay = jnp.exp(b_last + m_prev - m_new)
        wk = jnp.exp(b_last - b_col + i_col - m_new)
        ct_scr[h] = decay * ct + _dot(k_t, (wk * v).astype(BF16))
        n_scr[h] = decay * n_prev + jnp.sum(wk * k, axis=0, keepdims=True)
        m_scr[h] = jnp.broadcast_to(m_new, (1, LANE))
        ha = hh * lax.rsqrt(jnp.mean(hh * hh, axis=1, keepdims=True) + EPS) * ga_ref[h]
        ma_ref[:, h * DVA:(h + 1) * DVA] = (ha * _sigmoid(o)).astype(BF16)

    for h in range(HB):
        qr = _rope(yb_ref[:, h * DKB:(h + 1) * DKB], cosf, sinf)
        kr = _rope(yb_ref[:, HB * DKB + h * DKB:HB * DKB + (h + 1) * DKB], cosf, sinf) * (DKB ** -0.5)
        vb = yb_ref[:, 2 * HB * DKB + h * DVB:2 * HB * DKB + (h + 1) * DVB]
        gate = yb_ref[:, 2 * HB * DKB + HB * DVB + h * DVB:2 * HB * DKB + HB * DVB + (h + 1) * DVB]
        kr_t = kr.T.astype(BF16)
        qr16 = qr.astype(BF16)
        a = _dot(qr16, kr_t) * dmat_ref[h]
        sp = s_scr[h]
        out = _dot(a.astype(BF16), vb.astype(BF16)) + qdec_ref[h] * _dot(qr16, sp.astype(BF16))
        s_scr[h] = sdec_ref[h] * sp + _dot(kr_t, (kdec_ref[h] * vb).astype(BF16))
        mu = jnp.mean(out, axis=1, keepdims=True)
        var = jnp.mean(jnp.square(out - mu), axis=1, keepdims=True)
        hb = (out - mu) * lax.rsqrt(var + EPS) * gb_ref[h]
        mb_ref[:, h * DVB:(h + 1) * DVB] = (hb * (gate * _sigmoid(gate))).astype(BF16)

    @pl.when(c == pl.num_programs(1) - 1)
    def _():
        for h in range(HA):
            c_out[h] = ct_scr[h].T
            n_out[h] = n_scr[h]
            m_out[h] = m_scr[h]
            s_out[h] = s_scr[h]


def _retention_consts(L):
    log_gamma = jnp.log1p(-jnp.exp2(-5.0 - jnp.arange(HB, dtype=F32)))
    j = jnp.arange(L, dtype=F32)
    diff = j[:, None] - j[None, :]
    decay_mat = jnp.where(diff >= 0, jnp.exp(log_gamma[:, None, None] * jnp.maximum(diff, 0.0)), 0.0)
    q_decay = jnp.exp(log_gamma[:, None] * (j + 1.0))
    k_decay = jnp.exp(log_gamma[:, None] * (L - 1.0 - j))
    state_decay = jnp.exp(log_gamma * L)
    return decay_mat, q_decay, k_decay, state_decay


def _rope_tables(pos):
    half = DKB // 2
    freqs = ROPE_BASE ** (-jnp.arange(half, dtype=F32) / half)
    ang = pos.astype(F32)[:, None] * freqs[None, :]
    cos, sin = jnp.cos(ang), jnp.sin(ang)
    return jnp.concatenate([cos, cos], axis=1), jnp.concatenate([-sin, sin], axis=1)


def _ab_scan(ya, yg, yb, b_gate, g_a, g_b, batch, seq):
    L = CHUNK
    nc = seq // L
    cosf, sinf = _rope_tables(jnp.arange(seq))
    dmat, qdec, kdec, sdec = _retention_consts(L)
    qdec = jnp.broadcast_to(qdec[:, :, None], (HB, L, DVB))
    kdec = jnp.broadcast_to(kdec[:, :, None], (HB, L, DVB))
    smem = pl.BlockSpec(memory_space=pltpu.SMEM)
    rows = lambda b, c: (b * nc + c, 0)
    const3 = lambda b, c: (0, 0, 0)
    state = lambda b, c: (b, 0, 0, 0)
    m = batch * seq
    return pl.pallas_call(
        _ab_scan_body,
        grid=(batch, nc),
        in_specs=[
            smem, smem,
            pl.BlockSpec((L, AB_A_W), rows),
            pl.BlockSpec((L, LANE), rows),
            pl.BlockSpec((L, AB_B_W), rows),
            pl.BlockSpec((L, DKB), lambda b, c: (c, 0)),
            pl.BlockSpec((L, DKB), lambda b, c: (c, 0)),
            pl.BlockSpec((HB, L, L), const3),
            pl.BlockSpec((HB, L, DVB), const3),
            pl.BlockSpec((HB, L, DVB), const3),
            pl.BlockSpec((HA, 1, DVA), const3),
            pl.BlockSpec((HB, 1, DVB), const3),
        ],
        out_specs=[
            pl.BlockSpec((L, HA * DVA), rows),
            pl.BlockSpec((L, HB * DVB), rows),
            pl.BlockSpec((None, HA, DVA, DKA), state),
            pl.BlockSpec((None, HA, 1, DKA), state),
            pl.BlockSpec((None, HA, 1, LANE), state),
            pl.BlockSpec((None, HB, DKB, DVB), state),
        ],
        out_shape=[
            jax.ShapeDtypeStruct((m, HA * DVA), BF16),
            jax.ShapeDtypeStruct((m, HB * DVB), BF16),
            jax.ShapeDtypeStruct((batch, HA, DVA, DKA), F32),
            jax.ShapeDtypeStruct((batch, HA, 1, DKA), F32),
            jax.ShapeDtypeStruct((batch, HA, 1, LANE), F32),
            jax.ShapeDtypeStruct((batch, HB, DKB, DVB), F32),
        ],
        scratch_shapes=[
            pltpu.VMEM((HA, DKA, DVA), F32),
            pltpu.VMEM((HA, 1, DKA), F32),
            pltpu.VMEM((HA, 1, LANE), F32),
            pltpu.VMEM((HB, DKB, DVB), F32),
        ],
        compiler_params=_cp(("parallel", "arbitrary"), 40),
        name="ab_scan",
    )(sdec, b_gate, ya, yg, yb, cosf, sinf, dmat, qdec, kdec,
      g_a.reshape(HA, 1, DVA), g_b.reshape(HB, 1, DVB))


def _lane_bcast_col(row_vec):
    n = row_vec.shape[1]
    return jnp.broadcast_to(row_vec, (LANE, n)).T


def _ab_step_body(sdec_ref, bg_ref, ya_ref, yg_ref, yb_ref, cos_ref, sin_ref, ga_ref, gb_ref,
                  c_ref, n_ref, m_ref, s_ref, mix_ref, c_out, n_out, m_out, s_out):
    ya = ya_ref[...]
    yg = yg_ref[...]
    yb = yb_ref[...]
    cosf = cos_ref[...]
    sinf = sin_ref[...]

    for h in range(HA):
        q = ya[:, h * DKA:(h + 1) * DKA]
        k = ya[:, HA * DKA + h * DKA:HA * DKA + (h + 1) * DKA] * (DKA ** -0.5)
        v = ya[:, 2 * HA * DKA + h * DVA:2 * HA * DKA + (h + 1) * DVA]
        o = ya[:, 2 * HA * DKA + HA * DVA + h * DVA:2 * HA * DKA + HA * DVA + (h + 1) * DVA]
        i_pre = yg[:, h:h + 1] + bg_ref[h]
        log_f = _log_sigmoid(yg[:, HA + h:HA + h + 1] + bg_ref[HA + h])
        m_prev = m_ref[h][:, 0:1]
        g_in = log_f + m_prev
        m = jnp.maximum(g_in, i_pre)
        w_in = jnp.exp(i_pre - m)
        inter = jnp.exp(g_in - m)
        s = jnp.sum(q * k, axis=1, keepdims=True) * w_in
        cmat = c_ref[h]
        cq = jnp.sum(cmat * q, axis=1, keepdims=True)
        cq_row = jnp.broadcast_to(cq, (DVA, LANE)).T[0:1, :]
        num = s * v + inter * cq_row
        n_prev = n_ref[h]
        den = s + inter * jnp.sum(n_prev * q, axis=1, keepdims=True)
        hrow = num / jnp.maximum(jnp.abs(den), jnp.exp(-m))
        c_out[h] = inter * cmat + (w_in * _lane_bcast_col(v)) * k
        n_out[h] = inter * n_prev + w_in * k
        m_out[h] = jnp.broadcast_to(m, (1, LANE))
        ha = hrow * lax.rsqrt(jnp.mean(hrow * hrow, axis=1, keepdims=True) + EPS) * ga_ref[h]
        mix_ref[:, h * DVA:(h + 1) * DVA] = ha * _sigmoid(o)

    base = HA * DVA
    for h in range(HB):
        qr = _rope(yb[:, h * DKB:(h + 1) * DKB], cosf, sinf)
        kr = _rope(yb[:, HB * DKB + h * DKB:HB * DKB + (h + 1) * DKB], cosf, sinf) * (DKB ** -0.5)
        vb = yb[:, 2 * HB * DKB + h * DVB:2 * HB * DKB + (h + 1) * DVB]
        gate = yb[:, 2 * HB * DKB + HB * DVB + h * DVB:2 * HB * DKB + HB * DVB + (h + 1) * DVB]
        gamma = sdec_ref[h]
        smat = s_ref[h]
        q_col = _lane_bcast_col(qr)
        k_col = _lane_bcast_col(kr)
        qs = jnp.concatenate([jnp.sum(q_col * smat[:, t * LANE:(t + 1) * LANE], axis=0, keepdims=True)
                              for t in range(DVB // LANE)], axis=1)
        out = jnp.sum(qr * kr, axis=1, keepdims=True) * vb + gamma * qs
        s_out[h] = gamma * smat + jnp.concatenate([k_col * vb[:, t * LANE:(t + 1) * LANE]
                                                   for t in range(DVB // LANE)], axis=1)
        mu = jnp.mean(out, axis=1, keepdims=True)
        var = jnp.mean(jnp.square(out - mu), axis=1, keepdims=True)
        hb = (out - mu) * lax.rsqrt(var + EPS) * gb_ref[h]
        mix_ref[:, base + h * DVB:base + (h + 1) * DVB] = hb * (gate * _sigmoid(gate))


def _ab_step(ya, yg, yb, b_gate, g_a, g_b, c0, n0, m0, s0, pos0):
    bs = ya.shape[0]
    cosf, sinf = _rope_tables(pos0 + jnp.arange(1))
    _, _, _, sdec = _retention_consts(1)
    smem = pl.BlockSpec(memory_space=pltpu.SMEM)
    full = lambda shape: pl.BlockSpec(shape, lambda b: tuple(0 for _ in shape))
    row = lambda width: pl.BlockSpec((None, 1, width), lambda b: (b, 0, 0))
    state = lambda b: (b, 0, 0, 0)
    mix_w = HA * DVA + HB * DVB
    outs = pl.pallas_call(
        _ab_step_body,
        grid=(bs,),
        in_specs=[
            smem, smem,
            row(AB_A_W), row(LANE), row(AB_B_W),
            full((1, DKB)), full((1, DKB)),
            full((HA, 1, DVA)), full((HB, 1, DVB)),
            pl.BlockSpec((None, HA, DVA, DKA), state),
            pl.BlockSpec((None, HA, 1, DKA), state),
            pl.BlockSpec((None, HA, 1, LANE), state),
            pl.BlockSpec((None, HB, DKB, DVB), state),
        ],
        out_specs=[
            row(mix_w),
            pl.BlockSpec((None, HA, DVA, DKA), state),
            pl.BlockSpec((None, HA, 1, DKA), state),
            pl.BlockSpec((None, HA, 1, LANE), state),
            pl.BlockSpec((None, HB, DKB, DVB), state),
        ],
        out_shape=[
            jax.ShapeDtypeStruct((bs, 1, mix_w), F32),
            jax.ShapeDtypeStruct((bs, HA, DVA, DKA), F32),
            jax.ShapeDtypeStruct((bs, HA, 1, DKA), F32),
            jax.ShapeDtypeStruct((bs, HA, 1, LANE), F32),
            jax.ShapeDtypeStruct((bs, HB, DKB, DVB), F32),
        ],
        compiler_params=_cp(("parallel",), 32),
        name="ab_step",
    )(sdec, b_gate, ya.reshape(bs, 1, AB_A_W), yg.reshape(bs, 1, LANE), yb.reshape(bs, 1, AB_B_W), cosf, sinf,
      g_a.reshape(HA, 1, DVA), g_b.reshape(HB, 1, DVB),
      c0, n0.reshape(bs, HA, 1, DKA), jnp.broadcast_to(m0[:, :, None, None], (bs, HA, 1, LANE)), s0)
    return (outs[0].reshape(bs, mix_w),) + tuple(outs[1:])


def _bucket_np(dist):
    exact = N_BUCKETS // 2
    dist = np.maximum(dist, 0)
    ratio = np.maximum(dist, 1).astype(np.float32) / np.float32(exact)
    log_ratio = np.log(ratio) / np.float32(math.log(MAX_DISTANCE / exact))
    large = np.minimum(exact + (log_ratio * (N_BUCKETS - exact)).astype(np.int32), N_BUCKETS - 1)
    return np.where(dist < exact, dist, large).astype(np.int32)


def _bucket_thresholds():
    table = _bucket_np(np.arange(4 * MAX_DISTANCE))
    return [int(np.argmax(table >= k)) for k in range(1, N_BUCKETS)]


def _bias_table_body(rb_ref, idx_ref, o_ref):
    idx = idx_ref[...]
    for h in range(HC):
        acc = jnp.zeros(idx.shape, F32)
        for bkt in range(N_BUCKETS):
            acc = jnp.where(idx == bkt, rb_ref[bkt, h] * LOG2E, acc)
        o_ref[h] = acc


def _bias_table(rel_bias):
    t = np.arange(ATT_QB)[None, :]
    w = np.arange(ATT_W)[:, None]
    idx = _bucket_np(t + ATT_D0 - w)
    return pl.pallas_call(
        _bias_table_body,
        in_specs=[pl.BlockSpec(memory_space=pltpu.SMEM), pl.BlockSpec((ATT_W, ATT_QB), lambda: (0, 0))],
        out_specs=pl.BlockSpec((HC, ATT_W, ATT_QB), lambda: (0, 0, 0)),
        out_shape=jax.ShapeDtypeStruct((HC, ATT_W, ATT_QB), F32),
        name="bias_table",
    )(rel_bias, jnp.asarray(idx))


def _to_key(x):
    bits = lax.bitcast_convert_type(x, I32)
    return bits ^ ((bits >> 31) & jnp.int32(0x7FFFFFFF))


def _fold_rows(x, op):
    slabs = x.shape[0] // 8
    chains = min(4, slabs)
    acc = [x[r * 8:(r + 1) * 8, :] for r in range(chains)]
    for r in range(chains, slabs):
        acc[r % chains] = op(acc[r % chains], x[r * 8:(r + 1) * 8, :])
    while len(acc) > 1:
        acc = [op(acc[i], acc[i + 1]) for i in range(0, len(acc) - 1, 2)] + ([acc[-1]] if len(acc) % 2 else [])
    return acc[0]


def _attn_body(q_ref, qi_ref, wi_ref, ki_ref, k_ref, vt_ref, tab_ref, o_ref,
               key_scr, hi_scr, lo_scr, thr_scr, mask_scr, lg_scr, p_scr, m_scr, l_scr, acc_scr, *, topk):
    qb = pl.program_id(1)
    kc = pl.program_id(2)
    nq, kcs = ATT_QB, ATT_KC
    last = (qb * nq) // kcs
    sub = LANE

    @pl.when(kc == 0)
    def _():
        wi_t = wi_ref[...].T
        t_idx = qb * nq + lax.broadcasted_iota(I32, (kcs, nq), 1)
        score_scr = lg_scr.at[0]

        def score(c, carry):
            base = pl.multiple_of(c * kcs, kcs)
            ki = ki_ref[pl.ds(base, kcs), :][:, 0:IDX_DIM]
            for h in range(0, IDX_HEADS, 2):
                part = None
                for hh in (h, h + 1):
                    lg = _dot_nt(ki, qi_ref[:, hh * IDX_DIM:(hh + 1) * IDX_DIM])
                    term = wi_t[IDX_DIM + hh:IDX_DIM + hh + 1, :] * jnp.maximum(lg, 0.0)
                    part = term if part is None else part + term
                score_scr[...] = part if h == 0 else score_scr[...] + part
            s_idx = base + lax.broadcasted_iota(I32, (kcs, nq), 0)
            key = jnp.where(s_idx <= t_idx, _to_key(score_scr[...] + 0.0), jnp.int32(INT_MIN))
            key_scr[c] = key
            hi_scr[c] = (key >> 16).astype(I16)
            lo_scr[c] = ((key & 0xFFFF) - 32768).astype(I16)
            return carry

        lax.fori_loop(0, last + 1, score, 0)

        def count_ge(plane_scr, cand):
            cand_tile = jnp.broadcast_to(cand, (BF16_ROWS, nq)).astype(I16)

            def count(c, a):
                x = plane_scr[c]
                ge = [jnp.where(x[r * BF16_ROWS:(r + 1) * BF16_ROWS, :] >= cand_tile, jnp.int16(1), jnp.int16(0))
                      for r in range(kcs // BF16_ROWS)]
                chains = 4
                for r in range(chains, len(ge)):
                    ge[r % chains] = ge[r % chains] + ge[r]
                return a + ((ge[0] + ge[1]) + (ge[2] + ge[3]))

            a = lax.fori_loop(0, last + 1, count, jnp.zeros((BF16_ROWS, nq), I16))
            return jnp.sum(a.astype(F32), axis=0, keepdims=True)

        def bisect16(plane_scr, n_above):
            def step(i, thr):
                cand = thr + lax.shift_left(jnp.int32(1), 15 - i)
                return jnp.where(n_above + count_ge(plane_scr, cand) >= topk, cand, thr)
            return lax.fori_loop(0, 16, step, jnp.full((1, nq), -32768, I32))

        thr_hi = bisect16(hi_scr, 0.0)
        n_gt = jnp.where(thr_hi < 32767, count_ge(hi_scr, jnp.minimum(thr_hi + 1, 32767)), 0.0)
        hi_tile = jnp.broadcast_to(thr_hi, (BF16_ROWS, nq)).astype(I16)

        def mask_low(c, carry):
            for r in range(kcs // BF16_ROWS):
                rows = slice(r * BF16_ROWS, (r + 1) * BF16_ROWS)
                lo_scr[c, rows, :] = jnp.where(hi_scr[c, rows, :] == hi_tile, lo_scr[c, rows, :], jnp.int16(-32768))
            return carry

        lax.fori_loop(0, last + 1, mask_low, 0)
        thr_lo = bisect16(lo_scr, n_gt)
        thr = jnp.maximum(thr_hi * 65536 + (thr_lo + 32768), jnp.int32(INT_MIN + 1))
        thr_scr[...] = thr

        def count32(pred):
            def count(c, a):
                return a + _fold_rows(jnp.where(pred(key_scr[c]), 1.0, 0.0), jnp.add)
            a = lax.fori_loop(0, last + 1, count, jnp.zeros((8, nq), F32))
            return jnp.sum(a, axis=0, keepdims=True)

        surplus = jnp.max(count32(lambda k: k >= thr)) > topk

        @pl.when(surplus)
        def _():
            n_gt = count32(lambda k: k > thr)
            earlier = jnp.where(lax.broadcasted_iota(I32, (kcs, kcs), 1) < lax.broadcasted_iota(I32, (kcs, kcs), 0),
                                1.0, 0.0).astype(BF16)

            def drop_surplus(c, seen):
                key = key_scr[c]
                tied = key == thr
                tied_f = jnp.where(tied, 1.0, 0.0)
                rank = _dot(earlier, tied_f.astype(BF16)) + seen
                key_scr[c] = jnp.where(tied & (n_gt + rank >= topk), jnp.int32(INT_MIN), key)
                return seen + jnp.sum(_fold_rows(tied_f, jnp.add), axis=0, keepdims=True)

            lax.fori_loop(0, last + 1, drop_surplus, jnp.zeros((1, nq), F32))
        m_scr[...] = jnp.full_like(m_scr, NEG)
        l_scr[...] = jnp.zeros_like(l_scr)
        acc_scr[...] = jnp.zeros_like(acc_scr)

    @pl.when(kc <= last)
    def _():
        mask_scr[...] = jnp.where(key_scr[kc] >= thr_scr[...], 0.0, NEG)
        w0 = pl.multiple_of(jnp.maximum(ATT_D0 - qb * nq + kc * kcs, 0), LANE)
        m_new = []
        for h in range(HC):
            hs = slice(h * DHC, (h + 1) * DHC)
            lg = (_dot_nt(k_ref[:, hs], q_ref[:, hs]) * (DHC ** -0.5 * LOG2E) + tab_ref[h, pl.ds(w0, kcs), :]
                  + mask_scr[...])
            lg_scr[h] = lg
            m_new.append(jnp.maximum(m_scr[h], jnp.max(_fold_rows(lg, jnp.maximum), axis=0, keepdims=True)))
        alpha = []
        for h in range(HC):
            p = jnp.exp2(lg_scr[h] - m_new[h])
            p_scr[h] = p.astype(BF16)
            alpha.append(jnp.exp2(m_scr[h] - m_new[h]))
            l_scr[h] = alpha[h] * l_scr[h] + jnp.sum(_fold_rows(p, jnp.add), axis=0, keepdims=True)
            m_scr[h] = m_new[h]
        for h in range(HC):
            hs = slice(h * DHC, (h + 1) * DHC)
            acc_scr[hs, :] = alpha[h] * acc_scr[hs, :] + _dot(vt_ref[hs, :], p_scr[h])

    @pl.when(kc == last)
    def _():
        for h in range(HC):
            hs = slice(h * DHC, (h + 1) * DHC)
            o_ref[:, hs] = (acc_scr[hs, :] / l_scr[h]).T.astype(o_ref.dtype)


def _transpose_body(x_ref, o_ref):
    o_ref[...] = x_ref[...].T


def _transpose_v(ya16, batch, seq):
    hd = HC * DHC
    tt = 512
    nt = seq // tt
    return pl.pallas_call(
        _transpose_body,
        grid=(batch, nt),
        in_specs=[pl.BlockSpec((tt, hd), lambda b, t: (b * nt + t, 2))],
        out_specs=pl.BlockSpec((None, hd, tt), lambda b, t: (b, 0, t)),
        out_shape=jax.ShapeDtypeStruct((batch, hd, seq), BF16),
        compiler_params=_cp(("parallel", "parallel"), 32),
        name="transpose_v",
    )(ya16)


def _attn_prompt(ya16, yg, yg16, vt16, table, batch, seq):
    nqb = seq // ATT_QB
    nkc = seq // ATT_KC
    topk = min(TOPK_MAX, seq // 4)
    kv_blk = lambda qb, kc: jnp.minimum(kc, (qb * ATT_QB) // ATT_KC)
    q_row = lambda b, qb, kc: b * nqb + qb
    hd = HC * DHC
    return pl.pallas_call(
        functools.partial(_attn_body, topk=topk),
        grid=(batch, nqb, nkc),
        in_specs=[
            pl.BlockSpec((ATT_QB, hd), lambda b, qb, kc: (q_row(b, qb, kc), 0)),
            pl.BlockSpec((ATT_QB, IDX_HEADS * IDX_DIM), lambda b, qb, kc: (q_row(b, qb, kc), 3)),
            pl.BlockSpec((ATT_QB, LANE), lambda b, qb, kc: (q_row(b, qb, kc), 0)),
            pl.BlockSpec((seq, LANE), lambda b, qb, kc: (b, 0)),
            pl.BlockSpec((ATT_KC, hd), lambda b, qb, kc: (b * nkc + kv_blk(qb, kc), 1)),
            pl.BlockSpec((None, hd, ATT_KC), lambda b, qb, kc: (b, 0, kv_blk(qb, kc))),
            pl.BlockSpec((HC, ATT_W, ATT_QB), lambda b, qb, kc: (0, 0, 0)),
        ],
        out_specs=pl.BlockSpec((ATT_QB, hd), lambda b, qb, kc: (q_row(b, qb, kc), 0)),
        out_shape=jax.ShapeDtypeStruct((batch * seq, hd), BF16),
        scratch_shapes=[
            pltpu.VMEM((nkc, ATT_KC, ATT_QB), I32),
            pltpu.VMEM((nkc, ATT_KC, ATT_QB), I16),
            pltpu.VMEM((nkc, ATT_KC, ATT_QB), I16),
            pltpu.VMEM((1, ATT_QB), I32),
            pltpu.VMEM((ATT_KC, ATT_QB), F32),
            pltpu.VMEM((HC, ATT_KC, ATT_QB), F32),
            pltpu.VMEM((HC, ATT_KC, ATT_QB), BF16),
            pltpu.VMEM((HC, 1, ATT_QB), F32),
            pltpu.VMEM((HC, 1, ATT_QB), F32),
            pltpu.VMEM((hd, ATT_QB), F32),
        ],
        compiler_params=_cp(("parallel", "arbitrary", "arbitrary"), 56),
        name="attn_prompt",
    )(ya16, ya16, yg, yg16, ya16, vt16, table)


def _conv_body(yb_ref, cw_ref, cb_ref, gn_ref, bn_ref, o_ref, st_ref, ubuf, vbuf, *, tt):
    t = pl.program_id(1)
    halo = CONV_HALO
    rb = 32
    sl = 8
    first = halo - (CONV_W - 1)
    span = tt + halo - sl

    @pl.when(t == 0)
    def _():
        ubuf[0:halo, :] = jnp.zeros((halo, D_CONV), F32)

    @pl.when(t > 0)
    def _():
        ubuf[0:halo, :] = ubuf[tt:tt + halo, :]

    ubuf[halo:halo + tt, :] = yb_ref[:, 0:D_CONV] * _sigmoid(yb_ref[:, D_CONV:2 * D_CONV])
    step = 40
    for r in range(1, sl):
        for c0 in range(0, span, step):
            vbuf[r - 1, c0:c0 + step, :] = ubuf[c0 + r:c0 + r + step, :]
    for r in range(tt // rb):
        acc = jnp.broadcast_to(cb_ref[...], (rb, D_CONV))
        for w in range(CONV_W):
            a, res = divmod(first + w, sl)
            off = a * sl + r * rb
            src = ubuf[off:off + rb, :] if res == 0 else vbuf[res - 1, off:off + rb, :]
            acc = acc + src * cw_ref[w:w + 1, :]
        mu = jnp.mean(acc, axis=1, keepdims=True)
        var = jnp.mean(jnp.square(acc - mu), axis=1, keepdims=True)
        yn = (acc - mu) * lax.rsqrt(var + EPS) * gn_ref[...] + bn_ref[...]
        o_ref[r * rb:(r + 1) * rb, :] = (yn * _sigmoid(yn)).astype(o_ref.dtype)

    @pl.when(t == pl.num_programs(1) - 1)
    def _():
        st_ref[...] = ubuf[halo + tt - (CONV_W - 1):halo + tt, :]


def _conv_prompt(yb, cw, cb, gn, bn, batch, seq):
    tt = 256
    nt = seq // tt
    const = lambda b, t: (0, 0)
    return pl.pallas_call(
        functools.partial(_conv_body, tt=tt),
        grid=(batch, nt),
        in_specs=[
            pl.BlockSpec((tt, 2 * D_CONV), lambda b, t: (b * nt + t, 0)),
            pl.BlockSpec((CONV_W, D_CONV), const),
            pl.BlockSpec((1, D_CONV), const),
            pl.BlockSpec((1, D_CONV), const),
            pl.BlockSpec((1, D_CONV), const),
        ],
        out_specs=[
            pl.BlockSpec((tt, D_CONV), lambda b, t: (b * nt + t, 0)),
            pl.BlockSpec((None, CONV_W - 1, D_CONV), lambda b, t: (b, 0, 0)),
        ],
        out_shape=[
            jax.ShapeDtypeStruct((batch * seq, D_CONV), BF16),
            jax.ShapeDtypeStruct((batch, CONV_W - 1, D_CONV), F32),
        ],
        scratch_shapes=[pltpu.VMEM((CONV_HALO + tt, D_CONV), F32),
                        pltpu.VMEM((7, tt + CONV_HALO - 8, D_CONV), F32)],
        compiler_params=_cp(("parallel", "arbitrary"), 32),
        name="conv_prompt",
    )(yb, cw, cb.reshape(1, D_CONV), gn.reshape(1, D_CONV), bn.reshape(1, D_CONV))


def _rows_to_sublanes(row_vec, n, width, start=0):
    return jnp.concatenate([row_vec[:, start + i * width:start + (i + 1) * width] for i in range(n)], axis=0)


def _select_body(pt_ref, ya_ref, yg_ref, pool_ref, pos_ref, pbuf, sc_scr, slot_scr, sem, *, layer, n_pages, topk):
    b = pl.program_id(0)
    group = 16

    def page_copy(p):
        return pltpu.make_async_copy(pool_ref.at[layer, pt_ref[b, p]], pbuf.at[p], sem.at[0])

    def start(p, carry):
        page_copy(p).start()
        return carry

    def wait(p, carry):
        page_copy(p).wait()
        return carry

    lax.fori_loop(0, n_pages, start, 0)
    lax.fori_loop(0, n_pages, wait, 0)

    q_row = ya_ref[...]
    g_row = yg_ref[...]
    qi = _rows_to_sublanes(q_row, IDX_HEADS, IDX_DIM, start=3 * HC * DHC).astype(BF16)
    wi = _rows_to_sublanes(g_row, IDX_HEADS, 1, start=IDX_DIM)
    ki_new = g_row[:, 0:IDX_DIM].astype(BF16)

    for g in range(n_pages // group):
        keys = jnp.concatenate([pbuf[g * group + p] for p in range(group)], axis=1).astype(BF16)
        lg = _dot(qi, keys)
        sc = jnp.sum(jnp.maximum(lg, 0.0) * wi, axis=0, keepdims=True)
        for p in range(group):
            sc_scr[g * group + p:g * group + p + 1, :] = sc[:, p * PAGE_SIZE:(p + 1) * PAGE_SIZE]
    lg_new = jnp.sum(qi.astype(F32) * ki_new.astype(F32), axis=1, keepdims=True)
    sc_new = jnp.sum(jnp.maximum(lg_new, 0.0) * wi, axis=0, keepdims=True)
    key = _to_key(sc_scr[...] + 0.0)
    key_new = _to_key(sc_new + 0.0)

    def count(mask, mask_new):
        part = _fold_rows(jnp.where(mask, 1.0, 0.0), jnp.add)
        return jnp.sum(jnp.sum(part, axis=1, keepdims=True), axis=0, keepdims=True) + jnp.where(mask_new, 1.0, 0.0)

    def bisect(i, thr):
        cand = thr + lax.shift_left(jnp.int32(1), 31 - i)
        return jnp.where(count(key >= cand, key_new >= cand) >= topk, cand, thr)

    thr = lax.fori_loop(0, 32, bisect, jnp.full((1, 1), INT_MIN, I32))

    r_i = lax.broadcasted_iota(I32, (PAGE_SIZE, PAGE_SIZE), 0)
    c_i = lax.broadcasted_iota(I32, (PAGE_SIZE, PAGE_SIZE), 1)
    upper = jnp.where(r_i <= c_i, 1.0, 0.0).astype(BF16)
    pr = lax.broadcasted_iota(I32, (n_pages, n_pages), 0)
    pc = lax.broadcasted_iota(I32, (n_pages, n_pages), 1)
    before = jnp.where(pc < pr, 1.0, 0.0).astype(BF16)

    def rank(mask):
        inc = _dot(jnp.where(mask, 1.0, 0.0).astype(BF16), upper)
        tot = jnp.broadcast_to(inc[:, PAGE_SIZE - 1:PAGE_SIZE], (n_pages, PAGE_SIZE))
        return inc + _dot(before, tot.astype(BF16))

    gt = key > thr
    eq = key == thr
    n_gt = count(gt, key_new > thr)
    slot_eq = n_gt + rank(eq) - 1.0
    slot = jnp.where(gt, rank(gt) - 1.0, jnp.where(eq & (slot_eq < topk), slot_eq, -1.0))
    slot_scr[...] = slot

    r_iota = lax.broadcasted_iota(I32, (topk, PAGE_SIZE), 0).astype(F32)
    lane = lax.broadcasted_iota(I32, (1, PAGE_SIZE), 1).astype(F32)

    def compact(p, acc):
        val = lax.convert_element_type(p * PAGE_SIZE + 1, F32) + lane
        return acc + jnp.where(slot_scr[pl.ds(p, 1), :] == r_iota, val, 0.0)

    acc = lax.fori_loop(0, n_pages, compact, jnp.zeros((topk, PAGE_SIZE), F32))
    pos = jnp.sum(acc, axis=1, keepdims=True) - 1.0
    pos_ref[...] = jnp.broadcast_to(pos, (topk, PAGE_SIZE)).astype(I32)


def _select_sample(ya, yg, pool_idx, page_table, layer, topk):
    bs = ya.shape[0]
    n_pages = page_table.shape[1]
    row = lambda width: pl.BlockSpec((None, 1, width), lambda b, pt: (b, 0, 0))
    out = pl.pallas_call(
        functools.partial(_select_body, layer=layer, n_pages=n_pages, topk=topk),
        grid_spec=pltpu.PrefetchScalarGridSpec(
            num_scalar_prefetch=1,
            grid=(bs,),
            in_specs=[row(ya.shape[1]), row(yg.shape[1]), pl.BlockSpec(memory_space=pl.ANY)],
            out_specs=pl.BlockSpec((None, topk, PAGE_SIZE), lambda b, pt: (b, 0, 0)),
            scratch_shapes=[
                pltpu.VMEM((n_pages, IDX_DIM, PAGE_SIZE), F32),
                pltpu.VMEM((n_pages, PAGE_SIZE), F32),
                pltpu.VMEM((n_pages, PAGE_SIZE), F32),
                pltpu.SemaphoreType.DMA((1,)),
            ],
        ),
        out_shape=jax.ShapeDtypeStruct((bs, topk, PAGE_SIZE), I32),
        compiler_params=_cp(("arbitrary",), 40),
        name="select_sample",
    )(page_table, ya.reshape(bs, 1, -1), yg.reshape(bs, 1, -1), pool_idx)
    return out[:, :, 0]


def _sattn_body(pos_ref, pt_ref, ya_ref, posv_ref, rb_ref, kpool, vpool, o_ref, kbuf, vbuf, sem,
                *, layer, past, topk, thresholds):
    b = pl.program_id(0)

    def row_copies(r):
        pos = jnp.maximum(pos_ref[b, r], 0)
        page = pt_ref[b, pos // PAGE_SIZE]
        off = pos % PAGE_SIZE
        return (pltpu.make_async_copy(kpool.at[layer, page, off], kbuf.at[r], sem.at[0]),
                pltpu.make_async_copy(vpool.at[layer, page, off], vbuf.at[r], sem.at[1]))

    def start(r, new_slot):
        ck, cv = row_copies(r)
        ck.start()
        cv.start()
        return jnp.where(pos_ref[b, r] < 0, r, new_slot)

    new_slot = lax.fori_loop(0, topk, start, jnp.int32(-1))

    y_row = ya_ref[...]
    k_new = _rows_to_sublanes(y_row, HC, DHC, start=HC * DHC)
    v_new = _rows_to_sublanes(y_row, HC, DHC, start=2 * HC * DHC)

    def wait(r, carry):
        pltpu.make_async_copy(kpool.at[layer, 0, 0], kbuf.at[r], sem.at[0]).wait()
        pltpu.make_async_copy(vpool.at[layer, 0, 0], vbuf.at[r], sem.at[1]).wait()
        return carry

    lax.fori_loop(0, topk, wait, 0, unroll=8)

    @pl.when(new_slot >= 0)
    def _():
        kbuf[new_slot] = k_new
        vbuf[new_slot] = v_new

    posv = posv_ref[...]
    dist = jnp.where(posv < 0, 0, past - posv)
    bucket = jnp.zeros(dist.shape, I32)
    for thr in thresholds:
        bucket = bucket + jnp.where(dist >= thr, 1, 0)
    for h in range(HC):
        hs = slice(h * DHC, (h + 1) * DHC)
        bias = jnp.zeros(dist.shape, F32)
        for bkt in range(N_BUCKETS):
            bias = jnp.where(bucket == bkt, rb_ref[bkt, h], bias)
        q16 = jnp.broadcast_to(y_row[:, hs], (BF16_ROWS, DHC)).astype(BF16)
        lg = _dot_nt(q16, kbuf[:, h, :].astype(BF16)) * (DHC ** -0.5) + bias
        p = jnp.exp(lg - jnp.max(lg, axis=1, keepdims=True))
        p = p / jnp.sum(p, axis=1, keepdims=True)
        out = _dot(p.astype(BF16), vbuf[:, h, :].astype(BF16))
        o_ref[:, hs] = out[0:1, :]


def _attn_sample(ya, pos, rel_bias, pool_k, pool_v, page_table, layer, topk):
    bs = ya.shape[0]
    past = page_table.shape[1] * PAGE_SIZE
    row = lambda width: pl.BlockSpec((None, 1, width), lambda b, p, pt: (b, 0, 0))
    out = pl.pallas_call(
        functools.partial(_sattn_body, layer=layer, past=past, topk=topk, thresholds=_bucket_thresholds()),
        grid_spec=pltpu.PrefetchScalarGridSpec(
            num_scalar_prefetch=2,
            grid=(bs,),
            in_specs=[row(ya.shape[1]), row(topk), pl.BlockSpec(memory_space=pltpu.SMEM),
                      pl.BlockSpec(memory_space=pl.ANY), pl.BlockSpec(memory_space=pl.ANY)],
            out_specs=row(HC * DHC),
            scratch_shapes=[
                pltpu.VMEM((topk, HC, DHC), F32),
                pltpu.VMEM((topk, HC, DHC), F32),
                pltpu.SemaphoreType.DMA((2,)),
            ],
        ),
        out_shape=jax.ShapeDtypeStruct((bs, 1, HC * DHC), F32),
        compiler_params=_cp(("arbitrary",), 32),
        name="attn_sample",
    )(pos, page_table, ya.reshape(bs, 1, -1), pos.reshape(bs, 1, topk), rel_bias, pool_k, pool_v)
    return out.reshape(bs, HC * DHC)


def _conv_step_body(yb_ref, st_ref, cw_ref, cb_ref, gn_ref, bn_ref, o_ref, st_out):
    y = yb_ref[...]
    u = y[:, 0:D_CONV] * _sigmoid(y[:, D_CONV:2 * D_CONV])
    nprev = CONV_W - 1
    acc = jnp.sum(st_ref[...] * cw_ref[0:nprev, :], axis=0, keepdims=True) + u * cw_ref[nprev:CONV_W, :] + cb_ref[...]
    mu = jnp.mean(acc, axis=1, keepdims=True)
    var = jnp.mean(jnp.square(acc - mu), axis=1, keepdims=True)
    yn = (acc - mu) * lax.rsqrt(var + EPS) * gn_ref[...] + bn_ref[...]
    o_ref[...] = yn * _sigmoid(yn)
    st_out[0:nprev - 1, :] = st_ref[1:nprev, :]
    st_out[nprev - 1:nprev, :] = u


def _conv_step(yb, state, cw, cb, gn, bn):
    bs = yb.shape[0]
    full = lambda shape: pl.BlockSpec(shape, lambda b: tuple(0 for _ in shape))
    st_spec = pl.BlockSpec((None, CONV_W - 1, D_CONV), lambda b: (b, 0, 0))
    row = lambda width: pl.BlockSpec((None, 1, width), lambda b: (b, 0, 0))
    out, st = pl.pallas_call(
        _conv_step_body,
        grid=(bs,),
        in_specs=[row(2 * D_CONV), st_spec, full((CONV_W, D_CONV)), full((1, D_CONV)), full((1, D_CONV)),
                  full((1, D_CONV))],
        out_specs=[row(D_CONV), st_spec],
        out_shape=[jax.ShapeDtypeStruct((bs, 1, D_CONV), F32),
                   jax.ShapeDtypeStruct((bs, CONV_W - 1, D_CONV), F32)],
        compiler_params=_cp(("parallel",), 32),
        name="conv_step",
    )(yb.reshape(bs, 1, 2 * D_CONV), state, cw, cb.reshape(1, D_CONV), gn.reshape(1, D_CONV),
      bn.reshape(1, D_CONV))
    return out.reshape(bs, D_CONV), st


def kernel(x_prompt, x_sample, state_mlstm_c, state_mlstm_n, state_mlstm_m, state_ret, cache_k, cache_v,
           cache_idx_k, state_conv, page_table, norm_ffn1, w_ffn1_in, w_ffn1_out, norm_mix, norm_ffn2,
           w_ffn2_in, w_ffn2_out, norm_final, w_ab_in, b_ab_gate, g_mlstm_norm, g_ret_norm, w_ab_out,
           w_cd_in, w_cd_out, rel_bias, conv_w, conv_b, g_conv_norm, b_conv_norm):
    batch, seq, d = x_prompt.shape
    bs = x_sample.shape[0]
    depth = norm_ffn1.shape[0]
    past = page_table.shape[1] * PAGE_SIZE
    hd = HC * DHC
    assert x_sample.shape[1] == 1 and seq % ATT_KC == 0 and d == D_MODEL
    table = _bias_table(rel_bias)
    w_ab_in_t = jnp.swapaxes(w_ab_in, 1, 2)
    w_cd_in_t = jnp.swapaxes(w_cd_in, 1, 2)
    pool_idx_t = jnp.swapaxes(cache_idx_k, 2, 3)
    in_proj = functools.partial(_matmul, w_transposed=True)

    x = x_prompt.reshape(batch * seq, d)
    xs = x_sample.reshape(bs, d)
    topk_s = min(TOPK_MAX, (past + 1) // 4)
    p_ab, s_ab, p_cd, s_cd = [], [], [], []

    def cd_state(ya, yg, conv_state, lead):
        return (ya[:, hd:2 * hd].reshape(lead + (HC, DHC)), ya[:, 2 * hd:3 * hd].reshape(lead + (HC, DHC)),
                yg[:, 0:IDX_DIM].reshape(lead + (IDX_DIM,)), conv_state)

    for l in range(depth):
        j = l // 2
        x, xs = _ffn(x, xs, norm_ffn1[l], w_ffn1_in, w_ffn1_out, l)
        xn = _rmsnorm(x, norm_mix[l], BF16)
        xsn = _rmsnorm(xs, norm_mix[l], F32)
        if l % 2 == 0:
            ya, ya_s = in_proj([xn], [xsn], w_ab_in_t, j, 0, AB_A_W, 1024)
            yg, yg_s = in_proj([xn], [xsn], w_ab_in_t, j, AB_A_W, LANE, LANE)
            yb, yb_s = in_proj([xn], [xsn], w_ab_in_t, j, AB_B0, AB_B_W, 1024)
            gates = (b_ab_gate[j], g_mlstm_norm[j], g_ret_norm[j])
            ma, mb, c, n, mm, s = _ab_scan(ya, yg, yb, *gates, batch, seq)
            p_ab.append((c, n[:, :, 0, :], mm[:, :, 0, 0], s))
            mix_s, c, n, mm, s = _ab_step(ya_s, yg_s, yb_s, *gates, state_mlstm_c[j], state_mlstm_n[j],
                                          state_mlstm_m[j], state_ret[j], past)
            s_ab.append((c, n[:, :, 0, :], mm[:, :, 0, 0], s))
            x, xs = _matmul([ma, mb], [mix_s], w_ab_out, j, 0, d, 512, res=x, s_res=xs)
        else:
            ya, ya16, ya_s = in_proj([xn], [xsn], w_cd_in_t, j, 0, CD_A_W, 512, out_dtypes=(F32, BF16))
            yg, yg16, yg_s = in_proj([xn], [xsn], w_cd_in_t, j, CD_A_W, LANE, LANE, out_dtypes=(F32, BF16))
            yb, yb_s = in_proj([xn], [xsn], w_cd_in_t, j, CD_B0, CD_B_W, 1024)
            conv_args = (conv_w[j], conv_b[j], g_conv_norm[j], b_conv_norm[j])
            attn = _attn_prompt(ya16, yg, yg16, _transpose_v(ya16, batch, seq), table, batch, seq)
            conv, conv_state = _conv_prompt(yb, *conv_args, batch, seq)
            p_cd.append(cd_state(ya, yg, conv_state, (batch, seq)))
            pos = _select_sample(ya_s, yg_s, pool_idx_t, page_table, j, topk_s)
            attn_s = _attn_sample(ya_s, pos, rel_bias, cache_k, cache_v, page_table, j, topk_s)
            conv_s, conv_state_s = _conv_step(yb_s, state_conv[j], *conv_args)
            s_cd.append(cd_state(ya_s, yg_s, conv_state_s, (bs, 1)))
            x, xs = _matmul([attn, conv], [attn_s, conv_s], w_cd_out, j, 0, d, 512, res=x, s_res=xs)
        x, xs = _ffn(x, xs, norm_ffn2[l], w_ffn2_in, w_ffn2_out, l)
    y_p = _rmsnorm(x, norm_final, F32)
    y_s = _rmsnorm(xs, norm_final, F32)
    stack = lambda states: [jnp.stack(a) for a in zip(*states)]
    (pc, pn, pm, ps), (sc, sn, sm, ss) = stack(p_ab), stack(s_ab)
    (pk, pv, pik, pcv), (sk, sv, sik, scv) = stack(p_cd), stack(s_cd)
    return (y_p.reshape(batch, seq, d), y_s.reshape(bs, 1, d), pc, sc, pn, sn, pm, sm, ps, ss,
            pk, sk, pv, sv, pik, sik, pcv, scv)
```

```python
import functools
import math

import numpy as np
import jax
import jax.numpy as jnp
from jax import lax
from jax.experimental import pallas as pl
from jax.experimental.pallas import tpu as pltpu

F32 = jnp.float32
BF16 = jnp.bfloat16
I32 = jnp.int32
I16 = jnp.int16

D_MODEL = 2048
PAGE_SIZE = 128
HA, DKA, DVA = 4, 128, 256
HB, DKB, DVB = 4, 128, 256
CHUNK = 128
ROPE_BASE = 10000.0
HC, DHC = 8, 128
IDX_HEADS, IDX_DIM = 16, 64
TOPK_MAX = 256
QBLK = 128
N_BUCKETS = 32
MAX_DISTANCE = 128
D_CONV = 1024
CONV_W = 31
EPS = 1e-6

LANE = 128
BF16_ROWS = 16
MIB = 2 ** 20
INT_MIN = -(2 ** 31)
NEG = -1e30
LOG2E = 1.0 / math.log(2.0)
ATT_QB = 256
ATT_KC = 512
ATT_D0 = ATT_KC + LANE
ATT_W = 2 * ATT_KC + LANE
CONV_HALO = 32

AB_A_W = 2 * HA * DKA + 2 * HA * DVA
AB_B0 = AB_A_W + 2 * HA
AB_B_W = 2 * HB * DKB + 2 * HB * DVB
CD_A_W = 3 * HC * DHC + IDX_HEADS * IDX_DIM
CD_B0 = CD_A_W + IDX_DIM + IDX_HEADS
CD_B_W = 2 * D_CONV


def _cp(sem, vmem_mib):
    return pltpu.CompilerParams(dimension_semantics=sem, vmem_limit_bytes=vmem_mib * MIB)


def _sigmoid(x):
    return 1.0 / (1.0 + jnp.exp(-x))


def _log_sigmoid(x):
    return jnp.minimum(x, 0.0) - jnp.log1p(jnp.exp(-jnp.abs(x)))


def _dot(a, b):
    return jnp.dot(a, b, preferred_element_type=F32)


def _dot_nt(a, b):
    return lax.dot_general(a, b, (((1,), (1,)), ((), ())), preferred_element_type=F32)


def _pad_rows(a, rows):
    if a.shape[0] >= rows:
        return a
    return jnp.concatenate([a, jnp.zeros((rows - a.shape[0], a.shape[1]), a.dtype)], axis=0)


def _ffn_body(*refs, post):
    x_ref, xs_ref, g_ref, wa_ref, wb_ref, wo_ref = refs[:6]
    pos = 6
    g2_ref = None
    if post:
        g2_ref = refs[pos]
        pos += 1
    o_ref, os_ref = refs[pos:pos + 2]
    pos += 2
    on_ref = osn_ref = None
    if post == "norm":
        on_ref, osn_ref = refs[pos:pos + 2]
        pos += 2
    xn_ref, xsn_ref = refs[pos:pos + 2]
    i = pl.program_id(0)
    j = pl.program_id(1)
    last_j = pl.num_programs(1) - 1

    def normed(x, gain_ref):
        return x * lax.rsqrt(jnp.mean(x * x, axis=-1, keepdims=True) + EPS) * gain_ref[...]

    @pl.when(j == 0)
    def _():
        x = x_ref[...]
        xn_ref[...] = normed(x, g_ref).astype(BF16)
        o_ref[...] = x

    def half_swiglu(xn):
        a = _dot(xn, wa_ref[...].astype(BF16))
        b = _dot(xn, wb_ref[...].astype(BF16))
        return _dot((a * _sigmoid(a) * b * 0.5).astype(BF16), wo_ref[...].astype(BF16))

    def finish(out_ref, normed_ref):
        if post == "norm":
            normed_ref[...] = normed(out_ref[...], g2_ref).astype(normed_ref.dtype)
        elif post == "final":
            out_ref[...] = normed(out_ref[...], g2_ref)

    o_ref[...] += half_swiglu(xn_ref[...])
    if post:
        @pl.when(j == last_j)
        def _():
            finish(o_ref, on_ref)

    @pl.when(i == 0)
    def _():
        @pl.when(j == 0)
        def _():
            xs = xs_ref[...]
            xsn_ref[...] = _pad_rows(normed(xs, g_ref), BF16_ROWS).astype(BF16)
            os_ref[...] = xs

        os_ref[...] += half_swiglu(xsn_ref[...])[:os_ref.shape[0]]
        if post:
            @pl.when(j == last_j)
            def _():
                finish(os_ref, osn_ref)


def _ffn(x, xs, g, w_in, w_out, layer, post=None, g_post=None):
    m, d = x.shape
    ms = xs.shape[0]
    f = w_out.shape[1]
    tm = min(m, 1024)
    tf = 256 if post == "norm" else 512
    assert m % tm == 0 and f % tf == 0 and ms <= BF16_ROWS
    nf = f // tf
    row = lambda i, j: (i, 0)
    fixed = lambda i, j: (0, 0)
    in_specs = [
        pl.BlockSpec((tm, d), row, pipeline_mode=pl.Buffered(1)),
        pl.BlockSpec((ms, d), fixed),
        pl.BlockSpec((1, d), fixed),
        pl.BlockSpec((None, d, tf), lambda i, j: (layer, 0, j)),
        pl.BlockSpec((None, d, tf), lambda i, j: (layer, 0, j + nf)),
        pl.BlockSpec((None, tf, d), lambda i, j: (layer, j, 0)),
    ]
    args = [x, xs, g.reshape(1, d), w_in, w_in, w_out]
    out_specs = [pl.BlockSpec((tm, d), row), pl.BlockSpec((ms, d), fixed)]
    out_shape = [jax.ShapeDtypeStruct((m, d), F32), jax.ShapeDtypeStruct((ms, d), F32)]
    if post:
        in_specs.append(pl.BlockSpec((1, d), fixed))
        args.append(g_post.reshape(1, d))
    if post == "norm":
        out_specs += [pl.BlockSpec((tm, d), row), pl.BlockSpec((ms, d), fixed)]
        out_shape += [jax.ShapeDtypeStruct((m, d), BF16), jax.ShapeDtypeStruct((ms, d), F32)]
    return pl.pallas_call(
        functools.partial(_ffn_body, post=post),
        grid=(m // tm, nf),
        in_specs=in_specs,
        out_specs=out_specs,
        out_shape=out_shape,
        scratch_shapes=[pltpu.VMEM((tm, d), BF16), pltpu.VMEM((BF16_ROWS, d), BF16)],
        compiler_params=_cp(("arbitrary", "arbitrary"), 60),
        name="ffn",
    )(*args)


def _mm_body(*refs, k_sizes, ks_sizes, w_transposed, has_res, n_out):
    n_a, n_s = len(k_sizes), len(ks_sizes)
    a_refs = refs[:n_a]
    s_refs = refs[n_a:n_a + n_s]
    w_ref = refs[n_a + n_s]
    pos = n_a + n_s + 1
    res_ref = sres_ref = None
    if has_res:
        res_ref, sres_ref = refs[pos], refs[pos + 1]
        pos += 2
    out_refs = refs[pos:pos + n_out]
    os_ref = refs[pos + n_out]
    wt_ref = refs[pos + n_out + 1]
    rows_per_step = 64

    def product(parts, sizes):
        acc = None
        k0 = 0
        for p_ref, ks in zip(parts, sizes):
            a = _pad_rows(p_ref[...], BF16_ROWS).astype(BF16)
            d = _dot_nt(a, wt_ref[:, k0:k0 + ks]) if w_transposed else _dot(a, wt_ref[k0:k0 + ks, :])
            acc = d if acc is None else acc + d
            k0 += ks
        return acc

    @pl.when(pl.program_id(1) == 0)
    def _():
        def cast_rows(c, carry):
            rows = pl.ds(pl.multiple_of(c * rows_per_step, rows_per_step), rows_per_step)
            if w_transposed:
                wt_ref[rows, :] = w_ref[0, rows, :].astype(BF16)
            else:
                wt_ref[rows, :] = w_ref[rows, :].astype(BF16)
            return carry

        lax.fori_loop(0, wt_ref.shape[0] // rows_per_step, cast_rows, 0)
        acc_s = product(s_refs, ks_sizes)[:os_ref.shape[0]]
        if has_res:
            acc_s = acc_s + sres_ref[...]
        os_ref[...] = acc_s

    acc = product(a_refs, k_sizes)
    if has_res:
        acc = acc + res_ref[...]
    for o_ref in out_refs:
        o_ref[...] = acc.astype(o_ref.dtype)


def _matmul(a_parts, s_parts, w, layer, col0, width, tn, res=None, s_res=None, out_dtypes=(F32,),
            w_transposed=False):
    m, ms = a_parts[0].shape[0], s_parts[0].shape[0]
    k_sizes = tuple(a.shape[1] for a in a_parts)
    ks_sizes = tuple(a.shape[1] for a in s_parts)
    k_total = sum(k_sizes)
    assert w.shape[2 if w_transposed else 1] == k_total and sum(ks_sizes) == k_total
    assert width % tn == 0 and tn % 64 == 0 and ms <= BF16_ROWS and (res is None) == (s_res is None)
    tm = min(m, 2048 if tn <= 512 else 1024)
    assert m % tm == 0
    in_specs = [pl.BlockSpec((tm, ks), lambda j, i: (i, 0)) for ks in k_sizes]
    in_specs += [pl.BlockSpec((ms, ks), lambda j, i: (0, 0)) for ks in ks_sizes]
    if w_transposed:
        assert col0 % 8 == 0
        in_specs.append(pl.BlockSpec((pl.Element(1), pl.Element(tn), pl.Element(k_total)),
                                     lambda j, i: (layer, (col0 // 8 + j * (tn // 8)) * 8, 0)))
        wt_shape = (tn, k_total)
    else:
        assert col0 % tn == 0 and tn % LANE == 0
        in_specs.append(pl.BlockSpec((None, k_total, tn), lambda j, i: (layer, 0, col0 // tn + j)))
        wt_shape = (k_total, tn)
    args = list(a_parts) + list(s_parts) + [w]
    if res is not None:
        in_specs += [pl.BlockSpec((tm, tn), lambda j, i: (i, j)), pl.BlockSpec((ms, tn), lambda j, i: (0, j))]
        args += [res, s_res]
    outs = pl.pallas_call(
        functools.partial(_mm_body, k_sizes=k_sizes, ks_sizes=ks_sizes, w_transposed=w_transposed,
                          has_res=res is not None, n_out=len(out_dtypes)),
        grid=(width // tn, m // tm),
        in_specs=in_specs,
        out_specs=[pl.BlockSpec((tm, tn), lambda j, i: (i, j)) for _ in out_dtypes]
        + [pl.BlockSpec((ms, tn), lambda j, i: (0, j))],
        out_shape=[jax.ShapeDtypeStruct((m, width), dt) for dt in out_dtypes]
        + [jax.ShapeDtypeStruct((ms, width), F32)],
        scratch_shapes=[pltpu.VMEM(wt_shape, BF16)],
        compiler_params=_cp(("parallel", "arbitrary"), 56),
        name="matmul",
    )(*args)
    return outs


def _rope(x, cosf, sinf):
    return x * cosf + pltpu.roll(x, shift=x.shape[-1] // 2, axis=1) * sinf


def _ab_scan_body(sdec_ref, bg_ref, ya_ref, yg_ref, yb_ref, cos_ref, sin_ref, dmat_ref, qdec_ref, kdec_ref,
                  ga_ref, gb_ref, ma_ref, mb_ref, c_out, n_out, m_out, s_out, ct_scr, n_scr, m_scr, s_scr):
    c = pl.program_id(1)
    L = CHUNK

    @pl.when(c == 0)
    def _():
        ct_scr[...] = jnp.zeros_like(ct_scr)
        n_scr[...] = jnp.zeros_like(n_scr)
        m_scr[...] = jnp.zeros_like(m_scr)
        s_scr[...] = jnp.zeros_like(s_scr)

    g = yg_ref[...]
    g_t = g.T
    row = lax.broadcasted_iota(I32, (L, L), 0)
    col = lax.broadcasted_iota(I32, (L, L), 1)
    lower = col <= row
    cosf = cos_ref[...]
    sinf = sin_ref[...]

    for h in range(HA):
        q = ya_ref[:, h * DKA:(h + 1) * DKA]
        k = ya_ref[:, HA * DKA + h * DKA:HA * DKA + (h + 1) * DKA] * (DKA ** -0.5)
        v = ya_ref[:, 2 * HA * DKA + h * DVA:2 * HA * DKA + (h + 1) * DVA]
        o = ya_ref[:, 2 * HA * DKA + HA * DVA + h * DVA:2 * HA * DKA + HA * DVA + (h + 1) * DVA]
        b_i = bg_ref[h]
        b_f = bg_ref[HA + h]
        i_col = g[:, h:h + 1] + b_i
        f_col = _log_sigmoid(g[:, HA + h:HA + h + 1] + b_f)
        i_row = g_t[h:h + 1, :] + b_i
        f_row = _log_sigmoid(g_t[HA + h:HA + h + 1, :] + b_f)
        b_row = jnp.sum(jnp.where(row <= col, f_col, 0.0), axis=0, keepdims=True)
        b_col = jnp.sum(jnp.where(lower, f_row, 0.0), axis=1, keepdims=True)
        d_log = jnp.where(lower, b_col - b_row + i_row, -jnp.inf)
        m_prev = m_scr[h][:, 0:1]
        g_in = b_col + m_prev
        m = jnp.maximum(g_in, jnp.max(d_log, axis=1, keepdims=True))
        k_t = k.T.astype(BF16)
        q16 = q.astype(BF16)
        s = _dot(q16, k_t) * jnp.exp(d_log - m)
        inter = jnp.exp(g_in - m)
        ct = ct_scr[h]
        num = _dot(s.astype(BF16), v.astype(BF16)) + inter * _dot(q16, ct.astype(BF16))
        n_prev = n_scr[h]
        den = jnp.sum(s, axis=1, keepdims=True) + inter * jnp.sum(q * n_prev, axis=1, keepdims=True)
        hh = num / jnp.maximum(jnp.abs(den), jnp.exp(-m))
        m_new = m[L - 1:L, :]
        b_last = b_col[L - 1:L, :]
        decay = jnp.exp(b_last + m_prev - m_new)
        wk = jnp.exp(b_last - b_col + i_col - m_new)
        ct_scr[h] = decay * ct + _dot(k_t, (wk * v).astype(BF16))
        n_scr[h] = decay * n_prev + jnp.sum(wk * k, axis=0, keepdims=True)
        m_scr[h] = jnp.broadcast_to(m_new, (1, LANE))
        ha = hh * lax.rsqrt(jnp.mean(hh * hh, axis=1, keepdims=True) + EPS) * ga_ref[h]
        ma_ref[:, h * DVA:(h + 1) * DVA] = (ha * _sigmoid(o)).astype(BF16)

    for h in range(HB):
        qr = _rope(yb_ref[:, h * DKB:(h + 1) * DKB], cosf, sinf)
        kr = _rope(yb_ref[:, HB * DKB + h * DKB:HB * DKB + (h + 1) * DKB], cosf, sinf) * (DKB ** -0.5)
        vb = yb_ref[:, 2 * HB * DKB + h * DVB:2 * HB * DKB + (h + 1) * DVB]
        gate = yb_ref[:, 2 * HB * DKB + HB * DVB + h * DVB:2 * HB * DKB + HB * DVB + (h + 1) * DVB]
        kr_t = kr.T.astype(BF16)
        qr16 = qr.astype(BF16)
        a = _dot(qr16, kr_t) * dmat_ref[h]
        sp = s_scr[h]
        out = _dot(a.astype(BF16), vb.astype(BF16)) + qdec_ref[h] * _dot(qr16, sp.astype(BF16))
        s_scr[h] = sdec_ref[h] * sp + _dot(kr_t, (kdec_ref[h] * vb).astype(BF16))
        mu = jnp.mean(out, axis=1, keepdims=True)
        var = jnp.mean(jnp.square(out - mu), axis=1, keepdims=True)
        hb = (out - mu) * lax.rsqrt(var + EPS) * gb_ref[h]
        mb_ref[:, h * DVB:(h + 1) * DVB] = (hb * (gate * _sigmoid(gate))).astype(BF16)

    @pl.when(c == pl.num_programs(1) - 1)
    def _():
        for h in range(HA):
            c_out[h] = ct_scr[h].T
            n_out[h] = n_scr[h]
            m_out[h] = m_scr[h]
            s_out[h] = s_scr[h]


def _retention_consts(L):
    log_gamma = jnp.log1p(-jnp.exp2(-5.0 - jnp.arange(HB, dtype=F32)))
    j = jnp.arange(L, dtype=F32)
    diff = j[:, None] - j[None, :]
    decay_mat = jnp.where(diff >= 0, jnp.exp(log_gamma[:, None, None] * jnp.maximum(diff, 0.0)), 0.0)
    q_decay = jnp.exp(log_gamma[:, None] * (j + 1.0))
    k_decay = jnp.exp(log_gamma[:, None] * (L - 1.0 - j))
    state_decay = jnp.exp(log_gamma * L)
    return decay_mat, q_decay, k_decay, state_decay


def _rope_tables(pos):
    half = DKB // 2
    freqs = ROPE_BASE ** (-jnp.arange(half, dtype=F32) / half)
    ang = pos.astype(F32)[:, None] * freqs[None, :]
    cos, sin = jnp.cos(ang), jnp.sin(ang)
    return jnp.concatenate([cos, cos], axis=1), jnp.concatenate([-sin, sin], axis=1)


def _ab_scan(ya, yg, yb, b_gate, g_a, g_b, batch, seq):
    L = CHUNK
    nc = seq // L
    cosf, sinf = _rope_tables(jnp.arange(seq))
    dmat, qdec, kdec, sdec = _retention_consts(L)
    qdec = jnp.broadcast_to(qdec[:, :, None], (HB, L, DVB))
    kdec = jnp.broadcast_to(kdec[:, :, None], (HB, L, DVB))
    smem = pl.BlockSpec(memory_space=pltpu.SMEM)
    rows = lambda b, c: (b * nc + c, 0)
    const3 = lambda b, c: (0, 0, 0)
    state = lambda b, c: (b, 0, 0, 0)
    m = batch * seq
    return pl.pallas_call(
        _ab_scan_body,
        grid=(batch, nc),
        in_specs=[
            smem, smem,
            pl.BlockSpec((L, AB_A_W), rows),
            pl.BlockSpec((L, LANE), rows),
            pl.BlockSpec((L, AB_B_W), rows),
            pl.BlockSpec((L, DKB), lambda b, c: (c, 0)),
            pl.BlockSpec((L, DKB), lambda b, c: (c, 0)),
            pl.BlockSpec((HB, L, L), const3),
            pl.BlockSpec((HB, L, DVB), const3),
            pl.BlockSpec((HB, L, DVB), const3),
            pl.BlockSpec((HA, 1, DVA), const3),
            pl.BlockSpec((HB, 1, DVB), const3),
        ],
        out_specs=[
            pl.BlockSpec((L, HA * DVA), rows),
            pl.BlockSpec((L, HB * DVB), rows),
            pl.BlockSpec((None, HA, DVA, DKA), state),
            pl.BlockSpec((None, HA, 1, DKA), state),
            pl.BlockSpec((None, HA, 1, LANE), state),
            pl.BlockSpec((None, HB, DKB, DVB), state),
        ],
        out_shape=[
            jax.ShapeDtypeStruct((m, HA * DVA), BF16),
            jax.ShapeDtypeStruct((m, HB * DVB), BF16),
            jax.ShapeDtypeStruct((batch, HA, DVA, DKA), F32),
            jax.ShapeDtypeStruct((batch, HA, 1, DKA), F32),
            jax.ShapeDtypeStruct((batch, HA, 1, LANE), F32),
            jax.ShapeDtypeStruct((batch, HB, DKB, DVB), F32),
        ],
        scratch_shapes=[
            pltpu.VMEM((HA, DKA, DVA), F32),
            pltpu.VMEM((HA, 1, DKA), F32),
            pltpu.VMEM((HA, 1, LANE), F32),
            pltpu.VMEM((HB, DKB, DVB), F32),
        ],
        compiler_params=_cp(("parallel", "arbitrary"), 40),
        name="ab_scan",
    )(sdec, b_gate, ya, yg, yb, cosf, sinf, dmat, qdec, kdec,
      g_a.reshape(HA, 1, DVA), g_b.reshape(HB, 1, DVB))


def _lane_bcast_col(row_vec):
    n = row_vec.shape[1]
    return jnp.broadcast_to(row_vec, (LANE, n)).T


def _ab_step_body(sdec_ref, bg_ref, ya_ref, yg_ref, yb_ref, cos_ref, sin_ref, ga_ref, gb_ref,
                  c_ref, n_ref, m_ref, s_ref, mix_ref, c_out, n_out, m_out, s_out):
    ya = ya_ref[...]
    yg = yg_ref[...]
    yb = yb_ref[...]
    cosf = cos_ref[...]
    sinf = sin_ref[...]

    for h in range(HA):
        q = ya[:, h * DKA:(h + 1) * DKA]
        k = ya[:, HA * DKA + h * DKA:HA * DKA + (h + 1) * DKA] * (DKA ** -0.5)
        v = ya[:, 2 * HA * DKA + h * DVA:2 * HA * DKA + (h + 1) * DVA]
        o = ya[:, 2 * HA * DKA + HA * DVA + h * DVA:2 * HA * DKA + HA * DVA + (h + 1) * DVA]
        i_pre = yg[:, h:h + 1] + bg_ref[h]
        log_f = _log_sigmoid(yg[:, HA + h:HA + h + 1] + bg_ref[HA + h])
        m_prev = m_ref[h][:, 0:1]
        g_in = log_f + m_prev
        m = jnp.maximum(g_in, i_pre)
        w_in = jnp.exp(i_pre - m)
        inter = jnp.exp(g_in - m)
        s = jnp.sum(q * k, axis=1, keepdims=True) * w_in
        cmat = c_ref[h]
        cq = jnp.sum(cmat * q, axis=1, keepdims=True)
        cq_row = jnp.broadcast_to(cq, (DVA, LANE)).T[0:1, :]
        num = s * v + inter * cq_row
        n_prev = n_ref[h]
        den = s + inter * jnp.sum(n_prev * q, axis=1, keepdims=True)
        hrow = num / jnp.maximum(jnp.abs(den), jnp.exp(-m))
        c_out[h] = inter * cmat + (w_in * _lane_bcast_col(v)) * k
        n_out[h] = inter * n_prev + w_in * k
        m_out[h] = jnp.broadcast_to(m, (1, LANE))
        ha = hrow * lax.rsqrt(jnp.mean(hrow * hrow, axis=1, keepdims=True) + EPS) * ga_ref[h]
        mix_ref[:, h * DVA:(h + 1) * DVA] = ha * _sigmoid(o)

    base = HA * DVA
    for h in range(HB):
        qr = _rope(yb[:, h * DKB:(h + 1) * DKB], cosf, sinf)
        kr = _rope(yb[:, HB * DKB + h * DKB:HB * DKB + (h + 1) * DKB], cosf, sinf) * (DKB ** -0.5)
        vb = yb[:, 2 * HB * DKB + h * DVB:2 * HB * DKB + (h + 1) * DVB]
        gate = yb[:, 2 * HB * DKB + HB * DVB + h * DVB:2 * HB * DKB + HB * DVB + (h + 1) * DVB]
        gamma = sdec_ref[h]
        smat = s_ref[h]
        q_col = _lane_bcast_col(qr)
        k_col = _lane_bcast_col(kr)
        qs = jnp.concatenate([jnp.sum(q_col * smat[:, t * LANE:(t + 1) * LANE], axis=0, keepdims=True)
                              for t in range(DVB // LANE)], axis=1)
        out = jnp.sum(qr * kr, axis=1, keepdims=True) * vb + gamma * qs
        s_out[h] = gamma * smat + jnp.concatenate([k_col * vb[:, t * LANE:(t + 1) * LANE]
                                                   for t in range(DVB // LANE)], axis=1)
        mu = jnp.mean(out, axis=1, keepdims=True)
        var = jnp.mean(jnp.square(out - mu), axis=1, keepdims=True)
        hb = (out - mu) * lax.rsqrt(var + EPS) * gb_ref[h]
        mix_ref[:, base + h * DVB:base + (h + 1) * DVB] = hb * (gate * _sigmoid(gate))


def _ab_step(ya, yg, yb, b_gate, g_a, g_b, c0, n0, m0, s0, pos0):
    bs = ya.shape[0]
    cosf, sinf = _rope_tables(pos0 + jnp.arange(1))
    _, _, _, sdec = _retention_consts(1)
    smem = pl.BlockSpec(memory_space=pltpu.SMEM)
    full = lambda shape: pl.BlockSpec(shape, lambda b: tuple(0 for _ in shape))
    row = lambda width: pl.BlockSpec((None, 1, width), lambda b: (b, 0, 0))
    state = lambda b: (b, 0, 0, 0)
    mix_w = HA * DVA + HB * DVB
    outs = pl.pallas_call(
        _ab_step_body,
        grid=(bs,),
        in_specs=[
            smem, smem,
            row(AB_A_W), row(LANE), row(AB_B_W),
            full((1, DKB)), full((1, DKB)),
            full((HA, 1, DVA)), full((HB, 1, DVB)),
            pl.BlockSpec((None, HA, DVA, DKA), state),
            pl.BlockSpec((None, HA, 1, DKA), state),
            pl.BlockSpec((None, HA, 1, LANE), state),
            pl.BlockSpec((None, HB, DKB, DVB), state),
        ],
        out_specs=[
            row(mix_w),
            pl.BlockSpec((None, HA, DVA, DKA), state),
            pl.BlockSpec((None, HA, 1, DKA), state),
            pl.BlockSpec((None, HA, 1, LANE), state),
            pl.BlockSpec((None, HB, DKB, DVB), state),
        ],
        out_shape=[
            jax.ShapeDtypeStruct((bs, 1, mix_w), F32),
            jax.ShapeDtypeStruct((bs, HA, DVA, DKA), F32),
            jax.ShapeDtypeStruct((bs, HA, 1, DKA), F32),
            jax.ShapeDtypeStruct((bs, HA, 1, LANE), F32),
            jax.ShapeDtypeStruct((bs, HB, DKB, DVB), F32),
        ],
        compiler_params=_cp(("parallel",), 32),
        name="ab_step",
    )(sdec, b_gate, ya.reshape(bs, 1, AB_A_W), yg.reshape(bs, 1, LANE), yb.reshape(bs, 1, AB_B_W), cosf, sinf,
      g_a.reshape(HA, 1, DVA), g_b.reshape(HB, 1, DVB),
      c0, n0.reshape(bs, HA, 1, DKA), jnp.broadcast_to(m0[:, :, None, None], (bs, HA, 1, LANE)), s0)
    return (outs[0].reshape(bs, mix_w),) + tuple(outs[1:])


def _bucket_np(dist):
    exact = N_BUCKETS // 2
    dist = np.maximum(dist, 0)
    ratio = np.maximum(dist, 1).astype(np.float32) / np.float32(exact)
    log_ratio = np.log(ratio) / np.float32(math.log(MAX_DISTANCE / exact))
    large = np.minimum(exact + (log_ratio * (N_BUCKETS - exact)).astype(np.int32), N_BUCKETS - 1)
    return np.where(dist < exact, dist, large).astype(np.int32)


def _bucket_thresholds():
    table = _bucket_np(np.arange(4 * MAX_DISTANCE))
    return [int(np.argmax(table >= k)) for k in range(1, N_BUCKETS)]


def _bias_table_body(rb_ref, idx_ref, o_ref):
    idx = idx_ref[...]
    for h in range(HC):
        acc = jnp.zeros(idx.shape, F32)
        for bkt in range(N_BUCKETS):
            acc = jnp.where(idx == bkt, rb_ref[bkt, h] * LOG2E, acc)
        o_ref[h] = acc


def _bias_table(rel_bias):
    t = np.arange(ATT_QB)[None, :]
    w = np.arange(ATT_W)[:, None]
    idx = _bucket_np(t + ATT_D0 - w)
    return pl.pallas_call(
        _bias_table_body,
        in_specs=[pl.BlockSpec(memory_space=pltpu.SMEM), pl.BlockSpec((ATT_W, ATT_QB), lambda: (0, 0))],
        out_specs=pl.BlockSpec((HC, ATT_W, ATT_QB), lambda: (0, 0, 0)),
        out_shape=jax.ShapeDtypeStruct((HC, ATT_W, ATT_QB), F32),
        name="bias_table",
    )(rel_bias, jnp.asarray(idx))


def _to_key(x):
    bits = lax.bitcast_convert_type(x, I32)
    return bits ^ ((bits >> 31) & jnp.int32(0x7FFFFFFF))


def _fold_rows(x, op):
    slabs = x.shape[0] // 8
    chains = min(4, slabs)
    acc = [x[r * 8:(r + 1) * 8, :] for r in range(chains)]
    for r in range(chains, slabs):
        acc[r % chains] = op(acc[r % chains], x[r * 8:(r + 1) * 8, :])
    while len(acc) > 1:
        acc = [op(acc[i], acc[i + 1]) for i in range(0, len(acc) - 1, 2)] + ([acc[-1]] if len(acc) % 2 else [])
    return acc[0]


def _attn_body(qb_ref, kc_ref, q_ref, qi_ref, wi_ref, ki_ref, k_ref, vt_ref, tab_ref, o_ref,
               key_scr, hi_scr, lo_scr, thr_scr, mask_scr, lg_scr, p_scr, m_scr, l_scr, acc_scr, *, topk):
    step = pl.program_id(1)
    qb = qb_ref[step]
    kc = kc_ref[step]
    nq, kcs = ATT_QB, ATT_KC
    last = (qb * nq) // kcs

    @pl.when(kc == 0)
    def _():
        wi_t = wi_ref[...].T
        t_idx = qb * nq + lax.broadcasted_iota(I32, (kcs, nq), 1)
        score_scr = lg_scr.at[0]

        def score(c, carry):
            base = pl.multiple_of(c * kcs, kcs)
            ki = ki_ref[pl.ds(base, kcs), :][:, 0:IDX_DIM]
            for h in range(0, IDX_HEADS, 2):
                part = None
                for hh in (h, h + 1):
                    lg = _dot_nt(ki, qi_ref[:, hh * IDX_DIM:(hh + 1) * IDX_DIM])
                    term = wi_t[IDX_DIM + hh:IDX_DIM + hh + 1, :] * jnp.maximum(lg, 0.0)
                    part = term if part is None else part + term
                score_scr[...] = part if h == 0 else score_scr[...] + part
            s_idx = base + lax.broadcasted_iota(I32, (kcs, nq), 0)
            key = jnp.where(s_idx <= t_idx, _to_key(score_scr[...] + 0.0), jnp.int32(INT_MIN))
            key_scr[c] = key
            hi_scr[c] = (key >> 16).astype(I16)
            lo_scr[c] = ((key & 0xFFFF) - 32768).astype(I16)
            return carry

        lax.fori_loop(0, last + 1, score, 0)

        def count_ge(plane_scr, cand):
            cand_tile = jnp.broadcast_to(cand, (BF16_ROWS, nq)).astype(I16)

            def count(c, a):
                x = plane_scr[c]
                ge = [jnp.where(x[r * BF16_ROWS:(r + 1) * BF16_ROWS, :] >= cand_tile, jnp.int16(1), jnp.int16(0))
                      for r in range(kcs // BF16_ROWS)]
                chains = 4
                for r in range(chains, len(ge)):
                    ge[r % chains] = ge[r % chains] + ge[r]
                return a + ((ge[0] + ge[1]) + (ge[2] + ge[3]))

            a = lax.fori_loop(0, last + 1, count, jnp.zeros((BF16_ROWS, nq), I16))
            return jnp.sum(a.astype(F32), axis=0, keepdims=True)

        def bisect16(plane_scr, n_above):
            def step(i, thr):
                cand = thr + lax.shift_left(jnp.int32(1), 15 - i)
                return jnp.where(n_above + count_ge(plane_scr, cand) >= topk, cand, thr)
            return lax.fori_loop(0, 16, step, jnp.full((1, nq), -32768, I32))

        thr_hi = bisect16(hi_scr, 0.0)
        n_gt = jnp.where(thr_hi < 32767, count_ge(hi_scr, jnp.minimum(thr_hi + 1, 32767)), 0.0)
        hi_tile = jnp.broadcast_to(thr_hi, (BF16_ROWS, nq)).astype(I16)

        def mask_low(c, carry):
            for r in range(kcs // BF16_ROWS):
                rows = slice(r * BF16_ROWS, (r + 1) * BF16_ROWS)
                lo_scr[c, rows, :] = jnp.where(hi_scr[c, rows, :] == hi_tile, lo_scr[c, rows, :], jnp.int16(-32768))
            return carry

        lax.fori_loop(0, last + 1, mask_low, 0)
        thr_lo = bisect16(lo_scr, n_gt)
        thr = jnp.maximum(thr_hi * 65536 + (thr_lo + 32768), jnp.int32(INT_MIN + 1))
        thr_scr[...] = thr

        def count32(pred):
            def count(c, a):
                return a + _fold_rows(jnp.where(pred(key_scr[c]), 1.0, 0.0), jnp.add)
            a = lax.fori_loop(0, last + 1, count, jnp.zeros((8, nq), F32))
            return jnp.sum(a, axis=0, keepdims=True)

        surplus = jnp.max(count32(lambda k: k >= thr)) > topk

        @pl.when(surplus)
        def _():
            n_gt = count32(lambda k: k > thr)
            earlier = jnp.where(lax.broadcasted_iota(I32, (kcs, kcs), 1) < lax.broadcasted_iota(I32, (kcs, kcs), 0),
                                1.0, 0.0).astype(BF16)

            def drop_surplus(c, seen):
                key = key_scr[c]
                tied = key == thr
                tied_f = jnp.where(tied, 1.0, 0.0)
                rank = _dot(earlier, tied_f.astype(BF16)) + seen
                key_scr[c] = jnp.where(tied & (n_gt + rank >= topk), jnp.int32(INT_MIN), key)
                return seen + jnp.sum(_fold_rows(tied_f, jnp.add), axis=0, keepdims=True)

            lax.fori_loop(0, last + 1, drop_surplus, jnp.zeros((1, nq), F32))
        m_scr[...] = jnp.full_like(m_scr, NEG)
        l_scr[...] = jnp.zeros_like(l_scr)
        acc_scr[...] = jnp.zeros_like(acc_scr)

    mask_scr[...] = jnp.where(key_scr[kc] >= thr_scr[...], 0.0, NEG)
    w0 = pl.multiple_of(jnp.maximum(ATT_D0 - qb * nq + kc * kcs, 0), LANE)
    m_new = []
    for h in range(HC):
        hs = slice(h * DHC, (h + 1) * DHC)
        lg = (_dot_nt(k_ref[:, hs], q_ref[:, hs]) * (DHC ** -0.5 * LOG2E) + tab_ref[h, pl.ds(w0, kcs), :]
              + mask_scr[...])
        lg_scr[h] = lg
        m_new.append(jnp.maximum(m_scr[h], jnp.max(_fold_rows(lg, jnp.maximum), axis=0, keepdims=True)))
    alpha = []
    for h in range(HC):
        p = jnp.exp2(lg_scr[h] - m_new[h])
        p_scr[h] = p.astype(BF16)
        alpha.append(jnp.exp2(m_scr[h] - m_new[h]))
        l_scr[h] = alpha[h] * l_scr[h] + jnp.sum(_fold_rows(p, jnp.add), axis=0, keepdims=True)
        m_scr[h] = m_new[h]
    for h in range(HC):
        hs = slice(h * DHC, (h + 1) * DHC)
        acc_scr[hs, :] = alpha[h] * acc_scr[hs, :] + _dot(vt_ref[hs, :], p_scr[h])

    @pl.when(kc == last)
    def _():
        for h in range(HC):
            hs = slice(h * DHC, (h + 1) * DHC)
            o_ref[:, hs] = (acc_scr[hs, :] / l_scr[h]).T.astype(o_ref.dtype)


def _transpose_body(x_ref, o_ref):
    o_ref[...] = x_ref[...].T


def _transpose_v(ya16, batch, seq):
    hd = HC * DHC
    tt = 512
    nt = seq // tt
    return pl.pallas_call(
        _transpose_body,
        grid=(batch, nt),
        in_specs=[pl.BlockSpec((tt, hd), lambda b, t: (b * nt + t, 2))],
        out_specs=pl.BlockSpec((None, hd, tt), lambda b, t: (b, 0, t)),
        out_shape=jax.ShapeDtypeStruct((batch, hd, seq), BF16),
        compiler_params=_cp(("parallel", "parallel"), 32),
        name="transpose_v",
    )(ya16)


def _attn_prompt(ya16, yg, yg16, vt16, table, batch, seq):
    nqb = seq // ATT_QB
    nkc = seq // ATT_KC
    topk = min(TOPK_MAX, seq // 4)
    hd = HC * DHC
    pairs = [(qb, kc) for qb in range(nqb) for kc in range((qb * ATT_QB) // ATT_KC + 1)]
    qb_tab = jnp.asarray(np.array([p[0] for p in pairs], np.int32))
    kc_tab = jnp.asarray(np.array([p[1] for p in pairs], np.int32))
    q_rows = lambda col: (lambda b, s, qbt, kct: (b * nqb + qbt[s], col))
    return pl.pallas_call(
        functools.partial(_attn_body, topk=topk),
        grid_spec=pltpu.PrefetchScalarGridSpec(
            num_scalar_prefetch=2,
            grid=(batch, len(pairs)),
            in_specs=[
                pl.BlockSpec((ATT_QB, hd), q_rows(0)),
                pl.BlockSpec((ATT_QB, IDX_HEADS * IDX_DIM), q_rows(3)),
                pl.BlockSpec((ATT_QB, LANE), q_rows(0)),
                pl.BlockSpec((seq, LANE), lambda b, s, qbt, kct: (b, 0)),
                pl.BlockSpec((ATT_KC, hd), lambda b, s, qbt, kct: (b * nkc + kct[s], 1)),
                pl.BlockSpec((None, hd, ATT_KC), lambda b, s, qbt, kct: (b, 0, kct[s])),
                pl.BlockSpec((HC, ATT_W, ATT_QB), lambda b, s, qbt, kct: (0, 0, 0)),
            ],
            out_specs=pl.BlockSpec((ATT_QB, hd), q_rows(0)),
            scratch_shapes=[
                pltpu.VMEM((nkc, ATT_KC, ATT_QB), I32),
                pltpu.VMEM((nkc, ATT_KC, ATT_QB), I16),
                pltpu.VMEM((nkc, ATT_KC, ATT_QB), I16),
                pltpu.VMEM((1, ATT_QB), I32),
                pltpu.VMEM((ATT_KC, ATT_QB), F32),
                pltpu.VMEM((HC, ATT_KC, ATT_QB), F32),
                pltpu.VMEM((HC, ATT_KC, ATT_QB), BF16),
                pltpu.VMEM((HC, 1, ATT_QB), F32),
                pltpu.VMEM((HC, 1, ATT_QB), F32),
                pltpu.VMEM((hd, ATT_QB), F32),
            ],
        ),
        out_shape=jax.ShapeDtypeStruct((batch * seq, hd), BF16),
        compiler_params=_cp(("parallel", "arbitrary"), 56),
        name="attn_prompt",
    )(qb_tab, kc_tab, ya16, ya16, yg, yg16, ya16, vt16, table)


def _conv_body(yb_ref, cw_ref, cb_ref, gn_ref, bn_ref, o_ref, st_ref, ubuf, vbuf, *, tt):
    t = pl.program_id(1)
    halo = CONV_HALO
    rb = 32
    sl = 8
    first = halo - (CONV_W - 1)
    span = tt + halo - sl

    @pl.when(t == 0)
    def _():
        ubuf[0:halo, :] = jnp.zeros((halo, D_CONV), F32)

    @pl.when(t > 0)
    def _():
        ubuf[0:halo, :] = ubuf[tt:tt + halo, :]

    ubuf[halo:halo + tt, :] = yb_ref[:, 0:D_CONV] * _sigmoid(yb_ref[:, D_CONV:2 * D_CONV])
    step = 40
    for r in range(1, sl):
        for c0 in range(0, span, step):
            vbuf[r - 1, c0:c0 + step, :] = ubuf[c0 + r:c0 + r + step, :]
    for r in range(tt // rb):
        acc = jnp.broadcast_to(cb_ref[...], (rb, D_CONV))
        for w in range(CONV_W):
            a, res = divmod(first + w, sl)
            off = a * sl + r * rb
            src = ubuf[off:off + rb, :] if res == 0 else vbuf[res - 1, off:off + rb, :]
            acc = acc + src * cw_ref[w:w + 1, :]
        mu = jnp.mean(acc, axis=1, keepdims=True)
        var = jnp.mean(jnp.square(acc - mu), axis=1, keepdims=True)
        yn = (acc - mu) * lax.rsqrt(var + EPS) * gn_ref[...] + bn_ref[...]
        o_ref[r * rb:(r + 1) * rb, :] = (yn * _sigmoid(yn)).astype(o_ref.dtype)

    @pl.when(t == pl.num_programs(1) - 1)
    def _():
        st_ref[...] = ubuf[halo + tt - (CONV_W - 1):halo + tt, :]


def _conv_prompt(yb, cw, cb, gn, bn, batch, seq):
    tt = 256
    nt = seq // tt
    const = lambda b, t: (0, 0)
    return pl.pallas_call(
        functools.partial(_conv_body, tt=tt),
        grid=(batch, nt),
        in_specs=[
            pl.BlockSpec((tt, 2 * D_CONV), lambda b, t: (b * nt + t, 0)),
            pl.BlockSpec((CONV_W, D_CONV), const),
            pl.BlockSpec((1, D_CONV), const),
            pl.BlockSpec((1, D_CONV), const),
            pl.BlockSpec((1, D_CONV), const),
        ],
        out_specs=[
            pl.BlockSpec((tt, D_CONV), lambda b, t: (b * nt + t, 0)),
            pl.BlockSpec((None, CONV_W - 1, D_CONV), lambda b, t: (b, 0, 0)),
        ],
        out_shape=[
            jax.ShapeDtypeStruct((batch * seq, D_CONV), BF16),
            jax.ShapeDtypeStruct((batch, CONV_W - 1, D_CONV), F32),
        ],
        scratch_shapes=[pltpu.VMEM((CONV_HALO + tt, D_CONV), F32),
                        pltpu.VMEM((7, tt + CONV_HALO - 8, D_CONV), F32)],
        compiler_params=_cp(("parallel", "arbitrary"), 32),
        name="conv_prompt",
    )(yb, cw, cb.reshape(1, D_CONV), gn.reshape(1, D_CONV), bn.reshape(1, D_CONV))


def _rows_to_sublanes(row_vec, n, width, start=0):
    return jnp.concatenate([row_vec[:, start + i * width:start + (i + 1) * width] for i in range(n)], axis=0)


def _select_body(pt_ref, ya_ref, yg_ref, pool_ref, pos_ref, pbuf, sc_scr, slot_scr, sem, *, layer, n_pages, topk):
    b = pl.program_id(0)
    group = 16

    def page_copy(p):
        return pltpu.make_async_copy(pool_ref.at[layer, pt_ref[b, p]], pbuf.at[p], sem.at[0])

    def start(p, carry):
        page_copy(p).start()
        return carry

    def wait(p, carry):
        page_copy(p).wait()
        return carry

    lax.fori_loop(0, n_pages, start, 0)
    lax.fori_loop(0, n_pages, wait, 0)

    q_row = ya_ref[...]
    g_row = yg_ref[...]
    qi = _rows_to_sublanes(q_row, IDX_HEADS, IDX_DIM, start=3 * HC * DHC).astype(BF16)
    wi = _rows_to_sublanes(g_row, IDX_HEADS, 1, start=IDX_DIM)
    ki_new = g_row[:, 0:IDX_DIM].astype(BF16)

    for g in range(n_pages // group):
        keys = jnp.concatenate([pbuf[g * group + p] for p in range(group)], axis=1).astype(BF16)
        lg = _dot(qi, keys)
        sc = jnp.sum(jnp.maximum(lg, 0.0) * wi, axis=0, keepdims=True)
        for p in range(group):
            sc_scr[g * group + p:g * group + p + 1, :] = sc[:, p * PAGE_SIZE:(p + 1) * PAGE_SIZE]
    lg_new = jnp.sum(qi.astype(F32) * ki_new.astype(F32), axis=1, keepdims=True)
    sc_new = jnp.sum(jnp.maximum(lg_new, 0.0) * wi, axis=0, keepdims=True)
    key = _to_key(sc_scr[...] + 0.0)
    key_new = _to_key(sc_new + 0.0)

    def count(mask, mask_new):
        part = _fold_rows(jnp.where(mask, 1.0, 0.0), jnp.add)
        return jnp.sum(jnp.sum(part, axis=1, keepdims=True), axis=0, keepdims=True) + jnp.where(mask_new, 1.0, 0.0)

    def bisect(i, thr):
        cand = thr + lax.shift_left(jnp.int32(1), 31 - i)
        return jnp.where(count(key >= cand, key_new >= cand) >= topk, cand, thr)

    thr = lax.fori_loop(0, 32, bisect, jnp.full((1, 1), INT_MIN, I32))

    r_i = lax.broadcasted_iota(I32, (PAGE_SIZE, PAGE_SIZE), 0)
    c_i = lax.broadcasted_iota(I32, (PAGE_SIZE, PAGE_SIZE), 1)
    upper = jnp.where(r_i <= c_i, 1.0, 0.0).astype(BF16)
    pr = lax.broadcasted_iota(I32, (n_pages, n_pages), 0)
    pc = lax.broadcasted_iota(I32, (n_pages, n_pages), 1)
    before = jnp.where(pc < pr, 1.0, 0.0).astype(BF16)

    def rank(mask):
        inc = _dot(jnp.where(mask, 1.0, 0.0).astype(BF16), upper)
        tot = jnp.broadcast_to(inc[:, PAGE_SIZE - 1:PAGE_SIZE], (n_pages, PAGE_SIZE))
        return inc + _dot(before, tot.astype(BF16))

    gt = key > thr
    eq = key == thr
    n_gt = count(gt, key_new > thr)
    slot_eq = n_gt + rank(eq) - 1.0
    slot = jnp.where(gt, rank(gt) - 1.0, jnp.where(eq & (slot_eq < topk), slot_eq, -1.0))
    slot_scr[...] = slot

    r_iota = lax.broadcasted_iota(I32, (topk, PAGE_SIZE), 0).astype(F32)
    lane = lax.broadcasted_iota(I32, (1, PAGE_SIZE), 1).astype(F32)

    def compact(p, acc):
        val = lax.convert_element_type(p * PAGE_SIZE + 1, F32) + lane
        return acc + jnp.where(slot_scr[pl.ds(p, 1), :] == r_iota, val, 0.0)

    acc = lax.fori_loop(0, n_pages, compact, jnp.zeros((topk, PAGE_SIZE), F32))
    pos = jnp.sum(acc, axis=1, keepdims=True) - 1.0
    pos_ref[...] = jnp.broadcast_to(pos, (topk, PAGE_SIZE)).astype(I32)


def _select_sample(ya, yg, pool_idx, page_table, layer, topk):
    bs = ya.shape[0]
    n_pages = page_table.shape[1]
    row = lambda width: pl.BlockSpec((None, 1, width), lambda b, pt: (b, 0, 0))
    out = pl.pallas_call(
        functools.partial(_select_body, layer=layer, n_pages=n_pages, topk=topk),
        grid_spec=pltpu.PrefetchScalarGridSpec(
            num_scalar_prefetch=1,
            grid=(bs,),
            in_specs=[row(ya.shape[1]), row(yg.shape[1]), pl.BlockSpec(memory_space=pl.ANY)],
            out_specs=pl.BlockSpec((None, topk, PAGE_SIZE), lambda b, pt: (b, 0, 0)),
            scratch_shapes=[
                pltpu.VMEM((n_pages, IDX_DIM, PAGE_SIZE), F32),
                pltpu.VMEM((n_pages, PAGE_SIZE), F32),
                pltpu.VMEM((n_pages, PAGE_SIZE), F32),
                pltpu.SemaphoreType.DMA((1,)),
            ],
        ),
        out_shape=jax.ShapeDtypeStruct((bs, topk, PAGE_SIZE), I32),
        compiler_params=_cp(("arbitrary",), 40),
        name="select_sample",
    )(page_table, ya.reshape(bs, 1, -1), yg.reshape(bs, 1, -1), pool_idx)
    return out[:, :, 0]


def _sattn_body(pos_ref, pt_ref, ya_ref, posv_ref, rb_ref, kpool, vpool, o_ref, kbuf, vbuf, sem,
                *, layer, past, topk, thresholds):
    b = pl.program_id(0)

    def row_copies(r):
        pos = jnp.maximum(pos_ref[b, r], 0)
        page = pt_ref[b, pos // PAGE_SIZE]
        off = pos % PAGE_SIZE
        return (pltpu.make_async_copy(kpool.at[layer, page, off], kbuf.at[r], sem.at[0]),
                pltpu.make_async_copy(vpool.at[layer, page, off], vbuf.at[r], sem.at[1]))

    def start(r, new_slot):
        ck, cv = row_copies(r)
        ck.start()
        cv.start()
        return jnp.where(pos_ref[b, r] < 0, r, new_slot)

    new_slot = lax.fori_loop(0, topk, start, jnp.int32(-1))

    y_row = ya_ref[...]
    k_new = _rows_to_sublanes(y_row, HC, DHC, start=HC * DHC)
    v_new = _rows_to_sublanes(y_row, HC, DHC, start=2 * HC * DHC)

    def wait(r, carry):
        pltpu.make_async_copy(kpool.at[layer, 0, 0], kbuf.at[r], sem.at[0]).wait()
        pltpu.make_async_copy(vpool.at[layer, 0, 0], vbuf.at[r], sem.at[1]).wait()
        return carry

    lax.fori_loop(0, topk, wait, 0, unroll=8)

    @pl.when(new_slot >= 0)
    def _():
        kbuf[new_slot] = k_new
        vbuf[new_slot] = v_new

    posv = posv_ref[...]
    dist = jnp.where(posv < 0, 0, past - posv)
    bucket = jnp.zeros(dist.shape, I32)
    for thr in thresholds:
        bucket = bucket + jnp.where(dist >= thr, 1, 0)
    for h in range(HC):
        hs = slice(h * DHC, (h + 1) * DHC)
        bias = jnp.zeros(dist.shape, F32)
        for bkt in range(N_BUCKETS):
            bias = jnp.where(bucket == bkt, rb_ref[bkt, h], bias)
        q16 = jnp.broadcast_to(y_row[:, hs], (BF16_ROWS, DHC)).astype(BF16)
        lg = _dot_nt(q16, kbuf[:, h, :].astype(BF16)) * (DHC ** -0.5) + bias
        p = jnp.exp(lg - jnp.max(lg, axis=1, keepdims=True))
        p = p / jnp.sum(p, axis=1, keepdims=True)
        out = _dot(p.astype(BF16), vbuf[:, h, :].astype(BF16))
        o_ref[:, hs] = out[0:1, :]


def _attn_sample(ya, pos, rel_bias, pool_k, pool_v, page_table, layer, topk):
    bs = ya.shape[0]
    past = page_table.shape[1] * PAGE_SIZE
    row = lambda width: pl.BlockSpec((None, 1, width), lambda b, p, pt: (b, 0, 0))
    out = pl.pallas_call(
        functools.partial(_sattn_body, layer=layer, past=past, topk=topk, thresholds=_bucket_thresholds()),
        grid_spec=pltpu.PrefetchScalarGridSpec(
            num_scalar_prefetch=2,
            grid=(bs,),
            in_specs=[row(ya.shape[1]), row(topk), pl.BlockSpec(memory_space=pltpu.SMEM),
                      pl.BlockSpec(memory_space=pl.ANY), pl.BlockSpec(memory_space=pl.ANY)],
            out_specs=row(HC * DHC),
            scratch_shapes=[
                pltpu.VMEM((topk, HC, DHC), F32),
                pltpu.VMEM((topk, HC, DHC), F32),
                pltpu.SemaphoreType.DMA((2,)),
            ],
        ),
        out_shape=jax.ShapeDtypeStruct((bs, 1, HC * DHC), F32),
        compiler_params=_cp(("arbitrary",), 32),
        name="attn_sample",
    )(pos, page_table, ya.reshape(bs, 1, -1), pos.reshape(bs, 1, topk), rel_bias, pool_k, pool_v)
    return out.reshape(bs, HC * DHC)


def _conv_step_body(yb_ref, st_ref, cw_ref, cb_ref, gn_ref, bn_ref, o_ref, st_out):
    y = yb_ref[...]
    u = y[:, 0:D_CONV] * _sigmoid(y[:, D_CONV:2 * D_CONV])
    nprev = CONV_W - 1
    acc = jnp.sum(st_ref[...] * cw_ref[0:nprev, :], axis=0, keepdims=True) + u * cw_ref[nprev:CONV_W, :] + cb_ref[...]
    mu = jnp.mean(acc, axis=1, keepdims=True)
    var = jnp.mean(jnp.square(acc - mu), axis=1, keepdims=True)
    yn = (acc - mu) * lax.rsqrt(var + EPS) * gn_ref[...] + bn_ref[...]
    o_ref[...] = yn * _sigmoid(yn)
    st_out[0:nprev - 1, :] = st_ref[1:nprev, :]
    st_out[nprev - 1:nprev, :] = u


def _conv_step(yb, state, cw, cb, gn, bn):
    bs = yb.shape[0]
    full = lambda shape: pl.BlockSpec(shape, lambda b: tuple(0 for _ in shape))
    st_spec = pl.BlockSpec((None, CONV_W - 1, D_CONV), lambda b: (b, 0, 0))
    row = lambda width: pl.BlockSpec((None, 1, width), lambda b: (b, 0, 0))
    out, st = pl.pallas_call(
        _conv_step_body,
        grid=(bs,),
        in_specs=[row(2 * D_CONV), st_spec, full((CONV_W, D_CONV)), full((1, D_CONV)), full((1, D_CONV)),
                  full((1, D_CONV))],
        out_specs=[row(D_CONV), st_spec],
        out_shape=[jax.ShapeDtypeStruct((bs, 1, D_CONV), F32),
                   jax.ShapeDtypeStruct((bs, CONV_W - 1, D_CONV), F32)],
        compiler_params=_cp(("parallel",), 32),
        name="conv_step",
    )(yb.reshape(bs, 1, 2 * D_CONV), state, cw, cb.reshape(1, D_CONV), gn.reshape(1, D_CONV),
      bn.reshape(1, D_CONV))
    return out.reshape(bs, D_CONV), st


def kernel(x_prompt, x_sample, state_mlstm_c, state_mlstm_n, state_mlstm_m, state_ret, cache_k, cache_v,
           cache_idx_k, state_conv, page_table, norm_ffn1, w_ffn1_in, w_ffn1_out, norm_mix, norm_ffn2,
           w_ffn2_in, w_ffn2_out, norm_final, w_ab_in, b_ab_gate, g_mlstm_norm, g_ret_norm, w_ab_out,
           w_cd_in, w_cd_out, rel_bias, conv_w, conv_b, g_conv_norm, b_conv_norm):
    batch, seq, d = x_prompt.shape
    bs = x_sample.shape[0]
    depth = norm_ffn1.shape[0]
    past = page_table.shape[1] * PAGE_SIZE
    hd = HC * DHC
    assert x_sample.shape[1] == 1 and seq % ATT_KC == 0 and d == D_MODEL
    table = _bias_table(rel_bias)
    w_ab_in_t = jnp.swapaxes(w_ab_in, 1, 2)
    w_cd_in_t = jnp.swapaxes(w_cd_in, 1, 2)
    pool_idx_t = jnp.swapaxes(cache_idx_k, 2, 3)
    in_proj = functools.partial(_matmul, w_transposed=True)

    x = x_prompt.reshape(batch * seq, d)
    xs = x_sample.reshape(bs, d)
    topk_s = min(TOPK_MAX, (past + 1) // 4)
    p_ab, s_ab, p_cd, s_cd = [], [], [], []

    def cd_state(ya, yg, conv_state, lead):
        return (ya[:, hd:2 * hd].reshape(lead + (HC, DHC)), ya[:, 2 * hd:3 * hd].reshape(lead + (HC, DHC)),
                yg[:, 0:IDX_DIM].reshape(lead + (IDX_DIM,)), conv_state)

    for l in range(depth):
        j = l // 2
        x, xs, xn, xsn = _ffn(x, xs, norm_ffn1[l], w_ffn1_in, w_ffn1_out, l, post="norm", g_post=norm_mix[l])
        if l % 2 == 0:
            ya, ya_s = in_proj([xn], [xsn], w_ab_in_t, j, 0, AB_A_W, 1024)
            yg, yg_s = in_proj([xn], [xsn], w_ab_in_t, j, AB_A_W, LANE, LANE)
            yb, yb_s = in_proj([xn], [xsn], w_ab_in_t, j, AB_B0, AB_B_W, 1024)
            gates = (b_ab_gate[j], g_mlstm_norm[j], g_ret_norm[j])
            ma, mb, c, n, mm, s = _ab_scan(ya, yg, yb, *gates, batch, seq)
            p_ab.append((c, n[:, :, 0, :], mm[:, :, 0, 0], s))
            mix_s, c, n, mm, s = _ab_step(ya_s, yg_s, yb_s, *gates, state_mlstm_c[j], state_mlstm_n[j],
                                          state_mlstm_m[j], state_ret[j], past)
            s_ab.append((c, n[:, :, 0, :], mm[:, :, 0, 0], s))
            x, xs = _matmul([ma, mb], [mix_s], w_ab_out, j, 0, d, 512, res=x, s_res=xs)
        else:
            ya, ya16, ya_s = in_proj([xn], [xsn], w_cd_in_t, j, 0, CD_A_W, 512, out_dtypes=(F32, BF16))
            yg, yg16, yg_s = in_proj([xn], [xsn], w_cd_in_t, j, CD_A_W, LANE, LANE, out_dtypes=(F32, BF16))
            yb, yb_s = in_proj([xn], [xsn], w_cd_in_t, j, CD_B0, CD_B_W, 1024)
            conv_args = (conv_w[j], conv_b[j], g_conv_norm[j], b_conv_norm[j])
            attn = _attn_prompt(ya16, yg, yg16, _transpose_v(ya16, batch, seq), table, batch, seq)
            conv, conv_state = _conv_prompt(yb, *conv_args, batch, seq)
            p_cd.append(cd_state(ya, yg, conv_state, (batch, seq)))
            pos = _select_sample(ya_s, yg_s, pool_idx_t, page_table, j, topk_s)
            attn_s = _attn_sample(ya_s, pos, rel_bias, cache_k, cache_v, page_table, j, topk_s)
            conv_s, conv_state_s = _conv_step(yb_s, state_conv[j], *conv_args)
            s_cd.append(cd_state(ya_s, yg_s, conv_state_s, (bs, 1)))
            x, xs = _matmul([attn, conv], [attn_s, conv_s], w_cd_out, j, 0, d, 512, res=x, s_res=xs)
        if l == depth - 1:
            y_p, y_s = _ffn(x, xs, norm_ffn2[l], w_ffn2_in, w_ffn2_out, l, post="final", g_post=norm_final)
        else:
            x, xs = _ffn(x, xs, norm_ffn2[l], w_ffn2_in, w_ffn2_out, l)
    stack = lambda states: [jnp.stack(a) for a in zip(*states)]
    (pc, pn, pm, ps), (sc, sn, sm, ss) = stack(p_ab), stack(s_ab)
    (pk, pv, pik, pcv), (sk, sv, sik, scv) = stack(p_cd), stack(s_cd)
    return (y_p.reshape(batch, seq, d), y_s.reshape(bs, 1, d), pc, sc, pn, sn, pm, sm, ps, ss,
            pk, sk, pv, sv, pik, sik, pcv, scv)
```

```python
import functools
import math

import numpy as np
import jax
import jax.numpy as jnp
from jax import lax
from jax.experimental import pallas as pl
from jax.experimental.pallas import tpu as pltpu

F32 = jnp.float32
BF16 = jnp.bfloat16
I32 = jnp.int32
I16 = jnp.int16

D_MODEL = 2048
PAGE_SIZE = 128
HA, DKA, DVA = 4, 128, 256
HB, DKB, DVB = 4, 128, 256
CHUNK = 128
ROPE_BASE = 10000.0
HC, DHC = 8, 128
IDX_HEADS, IDX_DIM = 16, 64
TOPK_MAX = 256
QBLK = 128
N_BUCKETS = 32
MAX_DISTANCE = 128
D_CONV = 1024
CONV_W = 31
EPS = 1e-6

LANE = 128
BF16_ROWS = 16
MIB = 2 ** 20
INT_MIN = -(2 ** 31)
NEG = -1e30
LOG2E = 1.0 / math.log(2.0)
ATT_QB = 256
ATT_KC = 512
ATT_D0 = ATT_KC + LANE
ATT_W = 2 * ATT_KC + LANE
CONV_HALO = 32

AB_A_W = 2 * HA * DKA + 2 * HA * DVA
AB_B0 = AB_A_W + 2 * HA
AB_B_W = 2 * HB * DKB + 2 * HB * DVB
CD_A_W = 3 * HC * DHC + IDX_HEADS * IDX_DIM
CD_B0 = CD_A_W + IDX_DIM + IDX_HEADS
CD_B_W = 2 * D_CONV


def _cp(sem, vmem_mib):
    return pltpu.CompilerParams(dimension_semantics=sem, vmem_limit_bytes=vmem_mib * MIB)


def _sigmoid(x):
    return 1.0 / (1.0 + jnp.exp(-x))


def _log_sigmoid(x):
    return jnp.minimum(x, 0.0) - jnp.log1p(jnp.exp(-jnp.abs(x)))


def _dot(a, b):
    return jnp.dot(a, b, preferred_element_type=F32)


def _dot_nt(a, b):
    return lax.dot_general(a, b, (((1,), (1,)), ((), ())), preferred_element_type=F32)


def _pad_rows(a, rows):
    if a.shape[0] >= rows:
        return a
    return jnp.concatenate([a, jnp.zeros((rows - a.shape[0], a.shape[1]), a.dtype)], axis=0)


def _ffn_body(*refs, post):
    x_ref, xs_ref, g_ref, wa_ref, wb_ref, wo_ref = refs[:6]
    pos = 6
    g2_ref = None
    if post:
        g2_ref = refs[pos]
        pos += 1
    o_ref, os_ref = refs[pos:pos + 2]
    pos += 2
    on_ref = osn_ref = None
    if post == "norm":
        on_ref, osn_ref = refs[pos:pos + 2]
        pos += 2
    xn_ref, xsn_ref = refs[pos:pos + 2]
    i = pl.program_id(0)
    j = pl.program_id(1)
    last_j = pl.num_programs(1) - 1

    def normed(x, gain_ref):
        return x * lax.rsqrt(jnp.mean(x * x, axis=-1, keepdims=True) + EPS) * gain_ref[...]

    @pl.when(j == 0)
    def _():
        x = x_ref[...]
        xn_ref[...] = normed(x, g_ref).astype(BF16)
        o_ref[...] = x

    def half_swiglu(xn):
        a = _dot(xn, wa_ref[...].astype(BF16))
        b = _dot(xn, wb_ref[...].astype(BF16))
        return _dot((a * _sigmoid(a) * b * 0.5).astype(BF16), wo_ref[...].astype(BF16))

    def finish(out_ref, normed_ref):
        if post == "norm":
            normed_ref[...] = normed(out_ref[...], g2_ref).astype(normed_ref.dtype)
        elif post == "final":
            out_ref[...] = normed(out_ref[...], g2_ref)

    o_ref[...] += half_swiglu(xn_ref[...])
    if post:
        @pl.when(j == last_j)
        def _():
            finish(o_ref, on_ref)

    @pl.when(i == 0)
    def _():
        @pl.when(j == 0)
        def _():
            xs = xs_ref[...]
            xsn_ref[...] = _pad_rows(normed(xs, g_ref), BF16_ROWS).astype(BF16)
            os_ref[...] = xs

        os_ref[...] += half_swiglu(xsn_ref[...])[:os_ref.shape[0]]
        if post:
            @pl.when(j == last_j)
            def _():
                finish(os_ref, osn_ref)


def _ffn(x, xs, g, w_in, w_out, layer, post=None, g_post=None):
    m, d = x.shape
    ms = xs.shape[0]
    f = w_out.shape[1]
    tm = min(m, 1024)
    tf = 512
    assert m % tm == 0 and f % tf == 0 and ms <= BF16_ROWS
    nf = f // tf
    row = lambda i, j: (i, 0)
    fixed = lambda i, j: (0, 0)
    in_specs = [
        pl.BlockSpec((tm, d), row, pipeline_mode=pl.Buffered(1)),
        pl.BlockSpec((ms, d), fixed),
        pl.BlockSpec((1, d), fixed),
        pl.BlockSpec((None, d, tf), lambda i, j: (layer, 0, j)),
        pl.BlockSpec((None, d, tf), lambda i, j: (layer, 0, j + nf)),
        pl.BlockSpec((None, tf, d), lambda i, j: (layer, j, 0)),
    ]
    args = [x, xs, g.reshape(1, d), w_in, w_in, w_out]
    out_specs = [pl.BlockSpec((tm, d), row), pl.BlockSpec((ms, d), fixed)]
    out_shape = [jax.ShapeDtypeStruct((m, d), F32), jax.ShapeDtypeStruct((ms, d), F32)]
    if post:
        in_specs.append(pl.BlockSpec((1, d), fixed))
        args.append(g_post.reshape(1, d))
    if post == "norm":
        out_specs += [pl.BlockSpec((tm, d), row, pipeline_mode=pl.Buffered(1)), pl.BlockSpec((ms, d), fixed)]
        out_shape += [jax.ShapeDtypeStruct((m, d), BF16), jax.ShapeDtypeStruct((ms, d), F32)]
    return pl.pallas_call(
        functools.partial(_ffn_body, post=post),
        grid=(m // tm, nf),
        in_specs=in_specs,
        out_specs=out_specs,
        out_shape=out_shape,
        scratch_shapes=[pltpu.VMEM((tm, d), BF16), pltpu.VMEM((BF16_ROWS, d), BF16)],
        compiler_params=_cp(("arbitrary", "arbitrary"), 63 if post == "norm" else 60),
        name="ffn",
    )(*args)


def _mm_body(*refs, k_sizes, ks_sizes, w_transposed, has_res, n_out):
    n_a, n_s = len(k_sizes), len(ks_sizes)
    a_refs = refs[:n_a]
    s_refs = refs[n_a:n_a + n_s]
    w_ref = refs[n_a + n_s]
    pos = n_a + n_s + 1
    res_ref = sres_ref = None
    if has_res:
        res_ref, sres_ref = refs[pos], refs[pos + 1]
        pos += 2
    out_refs = refs[pos:pos + n_out]
    os_ref = refs[pos + n_out]
    wt_ref = refs[pos + n_out + 1]
    rows_per_step = 64

    def product(parts, sizes):
        acc = None
        k0 = 0
        for p_ref, ks in zip(parts, sizes):
            a = _pad_rows(p_ref[...], BF16_ROWS).astype(BF16)
            d = _dot_nt(a, wt_ref[:, k0:k0 + ks]) if w_transposed else _dot(a, wt_ref[k0:k0 + ks, :])
            acc = d if acc is None else acc + d
            k0 += ks
        return acc

    @pl.when(pl.program_id(1) == 0)
    def _():
        def cast_rows(c, carry):
            rows = pl.ds(pl.multiple_of(c * rows_per_step, rows_per_step), rows_per_step)
            if w_transposed:
                wt_ref[rows, :] = w_ref[0, rows, :].astype(BF16)
            else:
                wt_ref[rows, :] = w_ref[rows, :].astype(BF16)
            return carry

        lax.fori_loop(0, wt_ref.shape[0] // rows_per_step, cast_rows, 0)
        acc_s = product(s_refs, ks_sizes)[:os_ref.shape[0]]
        if has_res:
            acc_s = acc_s + sres_ref[...]
        os_ref[...] = acc_s

    acc = product(a_refs, k_sizes)
    if has_res:
        acc = acc + res_ref[...]
    for o_ref in out_refs:
        o_ref[...] = acc.astype(o_ref.dtype)


def _matmul(a_parts, s_parts, w, layer, col0, width, tn, res=None, s_res=None, out_dtypes=(F32,),
            w_transposed=False):
    m, ms = a_parts[0].shape[0], s_parts[0].shape[0]
    k_sizes = tuple(a.shape[1] for a in a_parts)
    ks_sizes = tuple(a.shape[1] for a in s_parts)
    k_total = sum(k_sizes)
    assert w.shape[2 if w_transposed else 1] == k_total and sum(ks_sizes) == k_total
    assert width % tn == 0 and tn % 64 == 0 and ms <= BF16_ROWS and (res is None) == (s_res is None)
    tm = min(m, 2048 if tn <= 512 else 1024)
    assert m % tm == 0
    in_specs = [pl.BlockSpec((tm, ks), lambda j, i: (i, 0)) for ks in k_sizes]
    in_specs += [pl.BlockSpec((ms, ks), lambda j, i: (0, 0)) for ks in ks_sizes]
    if w_transposed:
        assert col0 % 8 == 0
        in_specs.append(pl.BlockSpec((pl.Element(1), pl.Element(tn), pl.Element(k_total)),
                                     lambda j, i: (layer, (col0 // 8 + j * (tn // 8)) * 8, 0)))
        wt_shape = (tn, k_total)
    else:
        assert col0 % tn == 0 and tn % LANE == 0
        in_specs.append(pl.BlockSpec((None, k_total, tn), lambda j, i: (layer, 0, col0 // tn + j)))
        wt_shape = (k_total, tn)
    args = list(a_parts) + list(s_parts) + [w]
    if res is not None:
        in_specs += [pl.BlockSpec((tm, tn), lambda j, i: (i, j)), pl.BlockSpec((ms, tn), lambda j, i: (0, j))]
        args += [res, s_res]
    outs = pl.pallas_call(
        functools.partial(_mm_body, k_sizes=k_sizes, ks_sizes=ks_sizes, w_transposed=w_transposed,
                          has_res=res is not None, n_out=len(out_dtypes)),
        grid=(width // tn, m // tm),
        in_specs=in_specs,
        out_specs=[pl.BlockSpec((tm, tn), lambda j, i: (i, j)) for _ in out_dtypes]
        + [pl.BlockSpec((ms, tn), lambda j, i: (0, j))],
        out_shape=[jax.ShapeDtypeStruct((m, width), dt) for dt in out_dtypes]
        + [jax.ShapeDtypeStruct((ms, width), F32)],
        scratch_shapes=[pltpu.VMEM(wt_shape, BF16)],
        compiler_params=_cp(("parallel", "arbitrary"), 56),
        name="matmul",
    )(*args)
    return outs


def _rope(x, cosf, sinf):
    return x * cosf + pltpu.roll(x, shift=x.shape[-1] // 2, axis=1) * sinf


def _ab_scan_body(sdec_ref, bg_ref, ya_ref, yg_ref, yb_ref, cos_ref, sin_ref, dmat_ref, qdec_ref, kdec_ref,
                  ga_ref, gb_ref, ma_ref, mb_ref, c_out, n_out, m_out, s_out, ct_scr, n_scr, m_scr, s_scr):
    c = pl.program_id(1)
    L = CHUNK

    @pl.when(c == 0)
    def _():
        ct_scr[...] = jnp.zeros_like(ct_scr)
        n_scr[...] = jnp.zeros_like(n_scr)
        m_scr[...] = jnp.zeros_like(m_scr)
        s_scr[...] = jnp.zeros_like(s_scr)

    g = yg_ref[...]
    g_t = g.T
    row = lax.broadcasted_iota(I32, (L, L), 0)
    col = lax.broadcasted_iota(I32, (L, L), 1)
    lower = col <= row
    cosf = cos_ref[...]
    sinf = sin_ref[...]

    for h in range(HA):
        q = ya_ref[:, h * DKA:(h + 1) * DKA]
        k = ya_ref[:, HA * DKA + h * DKA:HA * DKA + (h + 1) * DKA] * (DKA ** -0.5)
        v = ya_ref[:, 2 * HA * DKA + h * DVA:2 * HA * DKA + (h + 1) * DVA]
        o = ya_ref[:, 2 * HA * DKA + HA * DVA + h * DVA:2 * HA * DKA + HA * DVA + (h + 1) * DVA]
        b_i = bg_ref[h]
        b_f = bg_ref[HA + h]
        i_col = g[:, h:h + 1] + b_i
        f_col = _log_sigmoid(g[:, HA + h:HA + h + 1] + b_f)
        i_row = g_t[h:h + 1, :] + b_i
        f_row = _log_sigmoid(g_t[HA + h:HA + h + 1, :] + b_f)
        b_row = jnp.sum(jnp.where(row <= col, f_col, 0.0), axis=0, keepdims=True)
        b_col = jnp.sum(jnp.where(lower, f_row, 0.0), axis=1, keepdims=True)
        d_log = jnp.where(lower, b_col - b_row + i_row, -jnp.inf)
        m_prev = m_scr[h][:, 0:1]
        g_in = b_col + m_prev
        m = jnp.maximum(g_in, jnp.max(d_log, axis=1, keepdims=True))
        k_t = k.T.astype(BF16)
        q16 = q.astype(BF16)
        s = _dot(q16, k_t) * jnp.exp(d_log - m)
        inter = jnp.exp(g_in - m)
        ct = ct_scr[h]
        num = _dot(s.astype(BF16), v.astype(BF16)) + inter * _dot(q16, ct.astype(BF16))
        n_prev = n_scr[h]
        den = jnp.sum(s, axis=1, keepdims=True) + inter * jnp.sum(q * n_prev, axis=1, keepdims=True)
        hh = num / jnp.maximum(jnp.abs(den), jnp.exp(-m))
        m_new = m[L - 1:L, :]
        b_last = b_col[L - 1:L, :]
        decay = jnp.exp(b_last + m_prev - m_new)
        wk = jnp.exp(b_last - b_col + i_col - m_new)
        ct_scr[h] = decay * ct + _dot(k_t, (wk * v).astype(BF16))
        n_scr[h] = decay * n_prev + jnp.sum(wk * k, axis=0, keepdims=True)
        m_scr[h] = jnp.broadcast_to(m_new, (1, LANE))
        ha = hh * lax.rsqrt(jnp.mean(hh * hh, axis=1, keepdims=True) + EPS) * ga_ref[h]
        ma_ref[:, h * DVA:(h + 1) * DVA] = (ha * _sigmoid(o)).astype(BF16)

    for h in range(HB):
        qr = _rope(yb_ref[:, h * DKB:(h + 1) * DKB], cosf, sinf)
        kr = _rope(yb_ref[:, HB * DKB + h * DKB:HB * DKB + (h + 1) * DKB], cosf, sinf) * (DKB ** -0.5)
        vb = yb_ref[:, 2 * HB * DKB + h * DVB:2 * HB * DKB + (h + 1) * DVB]
        gate = yb_ref[:, 2 * HB * DKB + HB * DVB + h * DVB:2 * HB * DKB + HB * DVB + (h + 1) * DVB]
        kr_t = kr.T.astype(BF16)
        qr16 = qr.astype(BF16)
        a = _dot(qr16, kr_t) * dmat_ref[h]
        sp = s_scr[h]
        out = _dot(a.astype(BF16), vb.astype(BF16)) + qdec_ref[h] * _dot(qr16, sp.astype(BF16))
        s_scr[h] = sdec_ref[h] * sp + _dot(kr_t, (kdec_ref[h] * vb).astype(BF16))
        mu = jnp.mean(out, axis=1, keepdims=True)
        var = jnp.mean(jnp.square(out - mu), axis=1, keepdims=True)
        hb = (out - mu) * lax.rsqrt(var + EPS) * gb_ref[h]
        mb_ref[:, h * DVB:(h + 1) * DVB] = (hb * (gate * _sigmoid(gate))).astype(BF16)

    @pl.when(c == pl.num_programs(1) - 1)
    def _():
        for h in range(HA):
            c_out[h] = ct_scr[h].T
            n_out[h] = n_scr[h]
            m_out[h] = m_scr[h]
            s_out[h] = s_scr[h]


def _retention_consts(L):
    log_gamma = jnp.log1p(-jnp.exp2(-5.0 - jnp.arange(HB, dtype=F32)))
    j = jnp.arange(L, dtype=F32)
    diff = j[:, None] - j[None, :]
    decay_mat = jnp.where(diff >= 0, jnp.exp(log_gamma[:, None, None] * jnp.maximum(diff, 0.0)), 0.0)
    q_decay = jnp.exp(log_gamma[:, None] * (j + 1.0))
    k_decay = jnp.exp(log_gamma[:, None] * (L - 1.0 - j))
    state_decay = jnp.exp(log_gamma * L)
    return decay_mat, q_decay, k_decay, state_decay


def _rope_tables(pos):
    half = DKB // 2
    freqs = ROPE_BASE ** (-jnp.arange(half, dtype=F32) / half)
    ang = pos.astype(F32)[:, None] * freqs[None, :]
    cos, sin = jnp.cos(ang), jnp.sin(ang)
    return jnp.concatenate([cos, cos], axis=1), jnp.concatenate([-sin, sin], axis=1)


def _ab_scan(ya, yg, yb, b_gate, g_a, g_b, batch, seq):
    L = CHUNK
    nc = seq // L
    cosf, sinf = _rope_tables(jnp.arange(seq))
    dmat, qdec, kdec, sdec = _retention_consts(L)
    qdec = jnp.broadcast_to(qdec[:, :, None], (HB, L, DVB))
    kdec = jnp.broadcast_to(kdec[:, :, None], (HB, L, DVB))
    smem = pl.BlockSpec(memory_space=pltpu.SMEM)
    rows = lambda b, c: (b * nc + c, 0)
    const3 = lambda b, c: (0, 0, 0)
    state = lambda b, c: (b, 0, 0, 0)
    m = batch * seq
    return pl.pallas_call(
        _ab_scan_body,
        grid=(batch, nc),
        in_specs=[
            smem, smem,
            pl.BlockSpec((L, AB_A_W), rows),
            pl.BlockSpec((L, LANE), rows),
            pl.BlockSpec((L, AB_B_W), rows),
            pl.BlockSpec((L, DKB), lambda b, c: (c, 0)),
            pl.BlockSpec((L, DKB), lambda b, c: (c, 0)),
            pl.BlockSpec((HB, L, L), const3),
            pl.BlockSpec((HB, L, DVB), const3),
            pl.BlockSpec((HB, L, DVB), const3),
            pl.BlockSpec((HA, 1, DVA), const3),
            pl.BlockSpec((HB, 1, DVB), const3),
        ],
        out_specs=[
            pl.BlockSpec((L, HA * DVA), rows),
            pl.BlockSpec((L, HB * DVB), rows),
            pl.BlockSpec((None, HA, DVA, DKA), state),
            pl.BlockSpec((None, HA, 1, DKA), state),
            pl.BlockSpec((None, HA, 1, LANE), state),
            pl.BlockSpec((None, HB, DKB, DVB), state),
        ],
        out_shape=[
            jax.ShapeDtypeStruct((m, HA * DVA), BF16),
            jax.ShapeDtypeStruct((m, HB * DVB), BF16),
            jax.ShapeDtypeStruct((batch, HA, DVA, DKA), F32),
            jax.ShapeDtypeStruct((batch, HA, 1, DKA), F32),
            jax.ShapeDtypeStruct((batch, HA, 1, LANE), F32),
            jax.ShapeDtypeStruct((batch, HB, DKB, DVB), F32),
        ],
        scratch_shapes=[
            pltpu.VMEM((HA, DKA, DVA), F32),
            pltpu.VMEM((HA, 1, DKA), F32),
            pltpu.VMEM((HA, 1, LANE), F32),
            pltpu.VMEM((HB, DKB, DVB), F32),
        ],
        compiler_params=_cp(("parallel", "arbitrary"), 40),
        name="ab_scan",
    )(sdec, b_gate, ya, yg, yb, cosf, sinf, dmat, qdec, kdec,
      g_a.reshape(HA, 1, DVA), g_b.reshape(HB, 1, DVB))


def _lane_bcast_col(row_vec):
    n = row_vec.shape[1]
    return jnp.broadcast_to(row_vec, (LANE, n)).T


def _ab_step_body(sdec_ref, bg_ref, ya_ref, yg_ref, yb_ref, cos_ref, sin_ref, ga_ref, gb_ref,
                  c_ref, n_ref, m_ref, s_ref, mix_ref, c_out, n_out, m_out, s_out):
    ya = ya_ref[...]
    yg = yg_ref[...]
    yb = yb_ref[...]
    cosf = cos_ref[...]
    sinf = sin_ref[...]

    for h in range(HA):
        q = ya[:, h * DKA:(h + 1) * DKA]
        k = ya[:, HA * DKA + h * DKA:HA * DKA + (h + 1) * DKA] * (DKA ** -0.5)
        v = ya[:, 2 * HA * DKA + h * DVA:2 * HA * DKA + (h + 1) * DVA]
        o = ya[:, 2 * HA * DKA + HA * DVA + h * DVA:2 * HA * DKA + HA * DVA + (h + 1) * DVA]
        i_pre = yg[:, h:h + 1] + bg_ref[h]
        log_f = _log_sigmoid(yg[:, HA + h:HA + h + 1] + bg_ref[HA + h])
        m_prev = m_ref[h][:, 0:1]
        g_in = log_f + m_prev
        m = jnp.maximum(g_in, i_pre)
        w_in = jnp.exp(i_pre - m)
        inter = jnp.exp(g_in - m)
        s = jnp.sum(q * k, axis=1, keepdims=True) * w_in
        cmat = c_ref[h]
        cq = jnp.sum(cmat * q, axis=1, keepdims=True)
        cq_row = jnp.broadcast_to(cq, (DVA, LANE)).T[0:1, :]
        num = s * v + inter * cq_row
        n_prev = n_ref[h]
        den = s + inter * jnp.sum(n_prev * q, axis=1, keepdims=True)
        hrow = num / jnp.maximum(jnp.abs(den), jnp.exp(-m))
        c_out[h] = inter * cmat + (w_in * _lane_bcast_col(v)) * k
        n_out[h] = inter * n_prev + w_in * k
        m_out[h] = jnp.broadcast_to(m, (1, LANE))
        ha = hrow * lax.rsqrt(jnp.mean(hrow * hrow, axis=1, keepdims=True) + EPS) * ga_ref[h]
        mix_ref[:, h * DVA:(h + 1) * DVA] = ha * _sigmoid(o)

    base = HA * DVA
    for h in range(HB):
        qr = _rope(yb[:, h * DKB:(h + 1) * DKB], cosf, sinf)
        kr = _rope(yb[:, HB * DKB + h * DKB:HB * DKB + (h + 1) * DKB], cosf, sinf) * (DKB ** -0.5)
        vb = yb[:, 2 * HB * DKB + h * DVB:2 * HB * DKB + (h + 1) * DVB]
        gate = yb[:, 2 * HB * DKB + HB * DVB + h * DVB:2 * HB * DKB + HB * DVB + (h + 1) * DVB]
        gamma = sdec_ref[h]
        smat = s_ref[h]
        q_col = _lane_bcast_col(qr)
        k_col = _lane_bcast_col(kr)
        qs = jnp.concatenate([jnp.sum(q_col * smat[:, t * LANE:(t + 1) * LANE], axis=0, keepdims=True)
                              for t in range(DVB // LANE)], axis=1)
        out = jnp.sum(qr * kr, axis=1, keepdims=True) * vb + gamma * qs
        s_out[h] = gamma * smat + jnp.concatenate([k_col * vb[:, t * LANE:(t + 1) * LANE]
                                                   for t in range(DVB // LANE)], axis=1)
        mu = jnp.mean(out, axis=1, keepdims=True)
        var = jnp.mean(jnp.square(out - mu), axis=1, keepdims=True)
        hb = (out - mu) * lax.rsqrt(var + EPS) * gb_ref[h]
        mix_ref[:, base + h * DVB:base + (h + 1) * DVB] = hb * (gate * _sigmoid(gate))


def _ab_step(ya, yg, yb, b_gate, g_a, g_b, c0, n0, m0, s0, pos0):
    bs = ya.shape[0]
    cosf, sinf = _rope_tables(pos0 + jnp.arange(1))
    _, _, _, sdec = _retention_consts(1)
    smem = pl.BlockSpec(memory_space=pltpu.SMEM)
    full = lambda shape: pl.BlockSpec(shape, lambda b: tuple(0 for _ in shape))
    row = lambda width: pl.BlockSpec((None, 1, width), lambda b: (b, 0, 0))
    state = lambda b: (b, 0, 0, 0)
    mix_w = HA * DVA + HB * DVB
    outs = pl.pallas_call(
        _ab_step_body,
        grid=(bs,),
        in_specs=[
            smem, smem,
            row(AB_A_W), row(LANE), row(AB_B_W),
            full((1, DKB)), full((1, DKB)),
            full((HA, 1, DVA)), full((HB, 1, DVB)),
            pl.BlockSpec((None, HA, DVA, DKA), state),
            pl.BlockSpec((None, HA, 1, DKA), state),
            pl.BlockSpec((None, HA, 1, LANE), state),
            pl.BlockSpec((None, HB, DKB, DVB), state),
        ],
        out_specs=[
            row(mix_w),
            pl.BlockSpec((None, HA, DVA, DKA), state),
            pl.BlockSpec((None, HA, 1, DKA), state),
            pl.BlockSpec((None, HA, 1, LANE), state),
            pl.BlockSpec((None, HB, DKB, DVB), state),
        ],
        out_shape=[
            jax.ShapeDtypeStruct((bs, 1, mix_w), F32),
            jax.ShapeDtypeStruct((bs, HA, DVA, DKA), F32),
            jax.ShapeDtypeStruct((bs, HA, 1, DKA), F32),
            jax.ShapeDtypeStruct((bs, HA, 1, LANE), F32),
            jax.ShapeDtypeStruct((bs, HB, DKB, DVB), F32),
        ],
        compiler_params=_cp(("parallel",), 32),
        name="ab_step",
    )(sdec, b_gate, ya.reshape(bs, 1, AB_A_W), yg.reshape(bs, 1, LANE), yb.reshape(bs, 1, AB_B_W), cosf, sinf,
      g_a.reshape(HA, 1, DVA), g_b.reshape(HB, 1, DVB),
      c0, n0.reshape(bs, HA, 1, DKA), jnp.broadcast_to(m0[:, :, None, None], (bs, HA, 1, LANE)), s0)
    return (outs[0].reshape(bs, mix_w),) + tuple(outs[1:])


def _bucket_np(dist):
    exact = N_BUCKETS // 2
    dist = np.maximum(dist, 0)
    ratio = np.maximum(dist, 1).astype(np.float32) / np.float32(exact)
    log_ratio = np.log(ratio) / np.float32(math.log(MAX_DISTANCE / exact))
    large = np.minimum(exact + (log_ratio * (N_BUCKETS - exact)).astype(np.int32), N_BUCKETS - 1)
    return np.where(dist < exact, dist, large).astype(np.int32)


def _bucket_thresholds():
    table = _bucket_np(np.arange(4 * MAX_DISTANCE))
    return [int(np.argmax(table >= k)) for k in range(1, N_BUCKETS)]


def _bias_table_body(rb_ref, idx_ref, o_ref):
    idx = idx_ref[...]
    for h in range(HC):
        acc = jnp.zeros(idx.shape, F32)
        for bkt in range(N_BUCKETS):
            acc = jnp.where(idx == bkt, rb_ref[bkt, h] * LOG2E, acc)
        o_ref[h] = acc


def _bias_table(rel_bias):
    t = np.arange(ATT_QB)[None, :]
    w = np.arange(ATT_W)[:, None]
    idx = _bucket_np(t + ATT_D0 - w)
    return pl.pallas_call(
        _bias_table_body,
        in_specs=[pl.BlockSpec(memory_space=pltpu.SMEM), pl.BlockSpec((ATT_W, ATT_QB), lambda: (0, 0))],
        out_specs=pl.BlockSpec((HC, ATT_W, ATT_QB), lambda: (0, 0, 0)),
        out_shape=jax.ShapeDtypeStruct((HC, ATT_W, ATT_QB), F32),
        name="bias_table",
    )(rel_bias, jnp.asarray(idx))


def _to_key(x):
    bits = lax.bitcast_convert_type(x, I32)
    return bits ^ ((bits >> 31) & jnp.int32(0x7FFFFFFF))


def _fold_rows(x, op):
    slabs = x.shape[0] // 8
    chains = min(4, slabs)
    acc = [x[r * 8:(r + 1) * 8, :] for r in range(chains)]
    for r in range(chains, slabs):
        acc[r % chains] = op(acc[r % chains], x[r * 8:(r + 1) * 8, :])
    while len(acc) > 1:
        acc = [op(acc[i], acc[i + 1]) for i in range(0, len(acc) - 1, 2)] + ([acc[-1]] if len(acc) % 2 else [])
    return acc[0]


def _attn_body(qb_ref, kc_ref, q_ref, qi_ref, wi_ref, ki_ref, k_ref, vt_ref, tab_ref, o_ref,
               key_scr, hi_scr, lo_scr, thr_scr, mask_scr, lg_scr, p_scr, m_scr, l_scr, acc_scr, *, topk):
    step = pl.program_id(1)
    qb = qb_ref[step]
    kc = kc_ref[step]
    nq, kcs = ATT_QB, ATT_KC
    last = (qb * nq) // kcs

    @pl.when(kc == 0)
    def _():
        wi_t = wi_ref[...].T
        t_idx = qb * nq + lax.broadcasted_iota(I32, (kcs, nq), 1)
        score_scr = lg_scr.at[0]

        def score(c, carry):
            base = pl.multiple_of(c * kcs, kcs)
            ki = ki_ref[pl.ds(base, kcs), :][:, 0:IDX_DIM]
            for h in range(0, IDX_HEADS, 2):
                part = None
                for hh in (h, h + 1):
                    lg = _dot_nt(ki, qi_ref[:, hh * IDX_DIM:(hh + 1) * IDX_DIM])
                    term = wi_t[IDX_DIM + hh:IDX_DIM + hh + 1, :] * jnp.maximum(lg, 0.0)
                    part = term if part is None else part + term
                score_scr[...] = part if h == 0 else score_scr[...] + part
            s_idx = base + lax.broadcasted_iota(I32, (kcs, nq), 0)
            key = jnp.where(s_idx <= t_idx, _to_key(score_scr[...] + 0.0), jnp.int32(INT_MIN))
            key_scr[c] = key
            hi_scr[c] = (key >> 16).astype(I16)
            lo_scr[c] = ((key & 0xFFFF) - 32768).astype(I16)
            return carry

        lax.fori_loop(0, last + 1, score, 0)

        def count_ge(plane_scr, cand):
            cand_tile = jnp.broadcast_to(cand, (BF16_ROWS, nq)).astype(I16)

            def count(c, a):
                x = plane_scr[c]
                ge = [jnp.where(x[r * BF16_ROWS:(r + 1) * BF16_ROWS, :] >= cand_tile, jnp.int16(1), jnp.int16(0))
                      for r in range(kcs // BF16_ROWS)]
                chains = 4
                for r in range(chains, len(ge)):
                    ge[r % chains] = ge[r % chains] + ge[r]
                return a + ((ge[0] + ge[1]) + (ge[2] + ge[3]))

            a = lax.fori_loop(0, last + 1, count, jnp.zeros((BF16_ROWS, nq), I16))
            return jnp.sum(a.astype(F32), axis=0, keepdims=True)

        def bisect16(plane_scr, n_above):
            def step(i, thr):
                cand = thr + lax.shift_left(jnp.int32(1), 15 - i)
                return jnp.where(n_above + count_ge(plane_scr, cand) >= topk, cand, thr)
            return lax.fori_loop(0, 16, step, jnp.full((1, nq), -32768, I32))

        thr_hi = bisect16(hi_scr, 0.0)
        n_gt = jnp.where(thr_hi < 32767, count_ge(hi_scr, jnp.minimum(thr_hi + 1, 32767)), 0.0)
        hi_tile = jnp.broadcast_to(thr_hi, (BF16_ROWS, nq)).astype(I16)

        def mask_low(c, carry):
            for r in range(kcs // BF16_ROWS):
                rows = slice(r * BF16_ROWS, (r + 1) * BF16_ROWS)
                lo_scr[c, rows, :] = jnp.where(hi_scr[c, rows, :] == hi_tile, lo_scr[c, rows, :], jnp.int16(-32768))
            return carry

        lax.fori_loop(0, last + 1, mask_low, 0)
        thr_lo = bisect16(lo_scr, n_gt)
        thr = jnp.maximum(thr_hi * 65536 + (thr_lo + 32768), jnp.int32(INT_MIN + 1))
        thr_scr[...] = thr

        def count32(pred):
            def count(c, a):
                return a + _fold_rows(jnp.where(pred(key_scr[c]), 1.0, 0.0), jnp.add)
            a = lax.fori_loop(0, last + 1, count, jnp.zeros((8, nq), F32))
            return jnp.sum(a, axis=0, keepdims=True)

        surplus = jnp.max(count32(lambda k: k >= thr)) > topk

        @pl.when(surplus)
        def _():
            n_gt = count32(lambda k: k > thr)
            earlier = jnp.where(lax.broadcasted_iota(I32, (kcs, kcs), 1) < lax.broadcasted_iota(I32, (kcs, kcs), 0),
                                1.0, 0.0).astype(BF16)

            def drop_surplus(c, seen):
                key = key_scr[c]
                tied = key == thr
                tied_f = jnp.where(tied, 1.0, 0.0)
                rank = _dot(earlier, tied_f.astype(BF16)) + seen
                key_scr[c] = jnp.where(tied & (n_gt + rank >= topk), jnp.int32(INT_MIN), key)
                return seen + jnp.sum(_fold_rows(tied_f, jnp.add), axis=0, keepdims=True)

            lax.fori_loop(0, last + 1, drop_surplus, jnp.zeros((1, nq), F32))
        m_scr[...] = jnp.full_like(m_scr, NEG)
        l_scr[...] = jnp.zeros_like(l_scr)
        acc_scr[...] = jnp.zeros_like(acc_scr)

    mask_scr[...] = jnp.where(key_scr[kc] >= thr_scr[...], 0.0, NEG)
    w0 = pl.multiple_of(jnp.maximum(ATT_D0 - qb * nq + kc * kcs, 0), LANE)
    m_new = []
    for h in range(HC):
        hs = slice(h * DHC, (h + 1) * DHC)
        lg = (_dot_nt(k_ref[:, hs], q_ref[:, hs]) * (DHC ** -0.5 * LOG2E) + tab_ref[h, pl.ds(w0, kcs), :]
              + mask_scr[...])
        lg_scr[h] = lg
        m_new.append(jnp.maximum(m_scr[h], jnp.max(_fold_rows(lg, jnp.maximum), axis=0, keepdims=True)))
    alpha = []
    for h in range(HC):
        p = jnp.exp2(lg_scr[h] - m_new[h])
        p_scr[h] = p.astype(BF16)
        alpha.append(jnp.exp2(m_scr[h] - m_new[h]))
        l_scr[h] = alpha[h] * l_scr[h] + jnp.sum(_fold_rows(p, jnp.add), axis=0, keepdims=True)
        m_scr[h] = m_new[h]
    for h in range(HC):
        hs = slice(h * DHC, (h + 1) * DHC)
        acc_scr[hs, :] = alpha[h] * acc_scr[hs, :] + _dot(vt_ref[hs, :], p_scr[h])

    @pl.when(kc == last)
    def _():
        for h in range(HC):
            hs = slice(h * DHC, (h + 1) * DHC)
            o_ref[:, hs] = (acc_scr[hs, :] / l_scr[h]).T.astype(o_ref.dtype)


def _transpose_body(x_ref, o_ref):
    o_ref[...] = x_ref[...].T


def _transpose_v(ya16, batch, seq):
    hd = HC * DHC
    tt = 512
    nt = seq // tt
    return pl.pallas_call(
        _transpose_body,
        grid=(batch, nt),
        in_specs=[pl.BlockSpec((tt, hd), lambda b, t: (b * nt + t, 2))],
        out_specs=pl.BlockSpec((None, hd, tt), lambda b, t: (b, 0, t)),
        out_shape=jax.ShapeDtypeStruct((batch, hd, seq), BF16),
        compiler_params=_cp(("parallel", "parallel"), 32),
        name="transpose_v",
    )(ya16)


def _attn_prompt(ya16, yg, yg16, vt16, table, batch, seq):
    nqb = seq // ATT_QB
    nkc = seq // ATT_KC
    topk = min(TOPK_MAX, seq // 4)
    hd = HC * DHC
    pairs = [(qb, kc) for qb in range(nqb) for kc in range((qb * ATT_QB) // ATT_KC + 1)]
    qb_tab = jnp.asarray(np.array([p[0] for p in pairs], np.int32))
    kc_tab = jnp.asarray(np.array([p[1] for p in pairs], np.int32))
    q_rows = lambda col: (lambda b, s, qbt, kct: (b * nqb + qbt[s], col))
    return pl.pallas_call(
        functools.partial(_attn_body, topk=topk),
        grid_spec=pltpu.PrefetchScalarGridSpec(
            num_scalar_prefetch=2,
            grid=(batch, len(pairs)),
            in_specs=[
                pl.BlockSpec((ATT_QB, hd), q_rows(0)),
                pl.BlockSpec((ATT_QB, IDX_HEADS * IDX_DIM), q_rows(3)),
                pl.BlockSpec((ATT_QB, LANE), q_rows(0)),
                pl.BlockSpec((seq, LANE), lambda b, s, qbt, kct: (b, 0)),
                pl.BlockSpec((ATT_KC, hd), lambda b, s, qbt, kct: (b * nkc + kct[s], 1)),
                pl.BlockSpec((None, hd, ATT_KC), lambda b, s, qbt, kct: (b, 0, kct[s])),
                pl.BlockSpec((HC, ATT_W, ATT_QB), lambda b, s, qbt, kct: (0, 0, 0)),
            ],
            out_specs=pl.BlockSpec((ATT_QB, hd), q_rows(0)),
            scratch_shapes=[
                pltpu.VMEM((nkc, ATT_KC, ATT_QB), I32),
                pltpu.VMEM((nkc, ATT_KC, ATT_QB), I16),
                pltpu.VMEM((nkc, ATT_KC, ATT_QB), I16),
                pltpu.VMEM((1, ATT_QB), I32),
                pltpu.VMEM((ATT_KC, ATT_QB), F32),
                pltpu.VMEM((HC, ATT_KC, ATT_QB), F32),
                pltpu.VMEM((HC, ATT_KC, ATT_QB), BF16),
                pltpu.VMEM((HC, 1, ATT_QB), F32),
                pltpu.VMEM((HC, 1, ATT_QB), F32),
                pltpu.VMEM((hd, ATT_QB), F32),
            ],
        ),
        out_shape=jax.ShapeDtypeStruct((batch * seq, hd), BF16),
        compiler_params=_cp(("parallel", "arbitrary"), 56),
        name="attn_prompt",
    )(qb_tab, kc_tab, ya16, ya16, yg, yg16, ya16, vt16, table)


def _conv_body(yb_ref, cw_ref, cb_ref, gn_ref, bn_ref, o_ref, st_ref, ubuf, vbuf, *, tt):
    t = pl.program_id(1)
    halo = CONV_HALO
    rb = 32
    sl = 8
    first = halo - (CONV_W - 1)
    span = tt + halo - sl

    @pl.when(t == 0)
    def _():
        ubuf[0:halo, :] = jnp.zeros((halo, D_CONV), F32)

    @pl.when(t > 0)
    def _():
        ubuf[0:halo, :] = ubuf[tt:tt + halo, :]

    ubuf[halo:halo + tt, :] = yb_ref[:, 0:D_CONV] * _sigmoid(yb_ref[:, D_CONV:2 * D_CONV])
    step = 40
    for r in range(1, sl):
        for c0 in range(0, span, step):
            vbuf[r - 1, c0:c0 + step, :] = ubuf[c0 + r:c0 + r + step, :]
    for r in range(tt // rb):
        acc = jnp.broadcast_to(cb_ref[...], (rb, D_CONV))
        for w in range(CONV_W):
            a, res = divmod(first + w, sl)
            off = a * sl + r * rb
            src = ubuf[off:off + rb, :] if res == 0 else vbuf[res - 1, off:off + rb, :]
            acc = acc + src * cw_ref[w:w + 1, :]
        mu = jnp.mean(acc, axis=1, keepdims=True)
        var = jnp.mean(jnp.square(acc - mu), axis=1, keepdims=True)
        yn = (acc - mu) * lax.rsqrt(var + EPS) * gn_ref[...] + bn_ref[...]
        o_ref[r * rb:(r + 1) * rb, :] = (yn * _sigmoid(yn)).astype(o_ref.dtype)

    @pl.when(t == pl.num_programs(1) - 1)
    def _():
        st_ref[...] = ubuf[halo + tt - (CONV_W - 1):halo + tt, :]


def _conv_prompt(yb, cw, cb, gn, bn, batch, seq):
    tt = 256
    nt = seq // tt
    const = lambda b, t: (0, 0)
    return pl.pallas_call(
        functools.partial(_conv_body, tt=tt),
        grid=(batch, nt),
        in_specs=[
            pl.BlockSpec((tt, 2 * D_CONV), lambda b, t: (b * nt + t, 0)),
            pl.BlockSpec((CONV_W, D_CONV), const),
            pl.BlockSpec((1, D_CONV), const),
            pl.BlockSpec((1, D_CONV), const),
            pl.BlockSpec((1, D_CONV), const),
        ],
        out_specs=[
            pl.BlockSpec((tt, D_CONV), lambda b, t: (b * nt + t, 0)),
            pl.BlockSpec((None, CONV_W - 1, D_CONV), lambda b, t: (b, 0, 0)),
        ],
        out_shape=[
            jax.ShapeDtypeStruct((batch * seq, D_CONV), BF16),
            jax.ShapeDtypeStruct((batch, CONV_W - 1, D_CONV), F32),
        ],
        scratch_shapes=[pltpu.VMEM((CONV_HALO + tt, D_CONV), F32),
                        pltpu.VMEM((7, tt + CONV_HALO - 8, D_CONV), F32)],
        compiler_params=_cp(("parallel", "arbitrary"), 32),
        name="conv_prompt",
    )(yb, cw, cb.reshape(1, D_CONV), gn.reshape(1, D_CONV), bn.reshape(1, D_CONV))


def _rows_to_sublanes(row_vec, n, width, start=0):
    return jnp.concatenate([row_vec[:, start + i * width:start + (i + 1) * width] for i in range(n)], axis=0)


def _select_body(pt_ref, ya_ref, yg_ref, pool_ref, pos_ref, pbuf, sc_scr, slot_scr, sem, *, layer, n_pages, topk):
    b = pl.program_id(0)
    group = 16

    def page_copy(p):
        return pltpu.make_async_copy(pool_ref.at[layer, pt_ref[b, p]], pbuf.at[p], sem.at[0])

    def start(p, carry):
        page_copy(p).start()
        return carry

    def wait(p, carry):
        page_copy(p).wait()
        return carry

    lax.fori_loop(0, n_pages, start, 0)
    lax.fori_loop(0, n_pages, wait, 0)

    q_row = ya_ref[...]
    g_row = yg_ref[...]
    qi = _rows_to_sublanes(q_row, IDX_HEADS, IDX_DIM, start=3 * HC * DHC).astype(BF16)
    wi = _rows_to_sublanes(g_row, IDX_HEADS, 1, start=IDX_DIM)
    ki_new = g_row[:, 0:IDX_DIM].astype(BF16)

    for g in range(n_pages // group):
        keys = jnp.concatenate([pbuf[g * group + p] for p in range(group)], axis=1).astype(BF16)
        lg = _dot(qi, keys)
        sc = jnp.sum(jnp.maximum(lg, 0.0) * wi, axis=0, keepdims=True)
        for p in range(group):
            sc_scr[g * group + p:g * group + p + 1, :] = sc[:, p * PAGE_SIZE:(p + 1) * PAGE_SIZE]
    lg_new = jnp.sum(qi.astype(F32) * ki_new.astype(F32), axis=1, keepdims=True)
    sc_new = jnp.sum(jnp.maximum(lg_new, 0.0) * wi, axis=0, keepdims=True)
    key = _to_key(sc_scr[...] + 0.0)
    key_new = _to_key(sc_new + 0.0)

    def count(mask, mask_new):
        part = _fold_rows(jnp.where(mask, 1.0, 0.0), jnp.add)
        return jnp.sum(jnp.sum(part, axis=1, keepdims=True), axis=0, keepdims=True) + jnp.where(mask_new, 1.0, 0.0)

    def bisect(i, thr):
        cand = thr + lax.shift_left(jnp.int32(1), 31 - i)
        return jnp.where(count(key >= cand, key_new >= cand) >= topk, cand, thr)

    thr = lax.fori_loop(0, 32, bisect, jnp.full((1, 1), INT_MIN, I32))

    r_i = lax.broadcasted_iota(I32, (PAGE_SIZE, PAGE_SIZE), 0)
    c_i = lax.broadcasted_iota(I32, (PAGE_SIZE, PAGE_SIZE), 1)
    upper = jnp.where(r_i <= c_i, 1.0, 0.0).astype(BF16)
    pr = lax.broadcasted_iota(I32, (n_pages, n_pages), 0)
    pc = lax.broadcasted_iota(I32, (n_pages, n_pages), 1)
    before = jnp.where(pc < pr, 1.0, 0.0).astype(BF16)

    def rank(mask):
        inc = _dot(jnp.where(mask, 1.0, 0.0).astype(BF16), upper)
        tot = jnp.broadcast_to(inc[:, PAGE_SIZE - 1:PAGE_SIZE], (n_pages, PAGE_SIZE))
        return inc + _dot(before, tot.astype(BF16))

    gt = key > thr
    eq = key == thr
    n_gt = count(gt, key_new > thr)
    slot_eq = n_gt + rank(eq) - 1.0
    slot = jnp.where(gt, rank(gt) - 1.0, jnp.where(eq & (slot_eq < topk), slot_eq, -1.0))
    slot_scr[...] = slot

    r_iota = lax.broadcasted_iota(I32, (topk, PAGE_SIZE), 0).astype(F32)
    lane = lax.broadcasted_iota(I32, (1, PAGE_SIZE), 1).astype(F32)

    def compact(p, acc):
        val = lax.convert_element_type(p * PAGE_SIZE + 1, F32) + lane
        return acc + jnp.where(slot_scr[pl.ds(p, 1), :] == r_iota, val, 0.0)

    acc = lax.fori_loop(0, n_pages, compact, jnp.zeros((topk, PAGE_SIZE), F32))
    pos = jnp.sum(acc, axis=1, keepdims=True) - 1.0
    pos_ref[...] = jnp.broadcast_to(pos, (topk, PAGE_SIZE)).astype(I32)


def _select_sample(ya, yg, pool_idx, page_table, layer, topk):
    bs = ya.shape[0]
    n_pages = page_table.shape[1]
    row = lambda width: pl.BlockSpec((None, 1, width), lambda b, pt: (b, 0, 0))
    out = pl.pallas_call(
        functools.partial(_select_body, layer=layer, n_pages=n_pages, topk=topk),
        grid_spec=pltpu.PrefetchScalarGridSpec(
            num_scalar_prefetch=1,
            grid=(bs,),
            in_specs=[row(ya.shape[1]), row(yg.shape[1]), pl.BlockSpec(memory_space=pl.ANY)],
            out_specs=pl.BlockSpec((None, topk, PAGE_SIZE), lambda b, pt: (b, 0, 0)),
            scratch_shapes=[
                pltpu.VMEM((n_pages, IDX_DIM, PAGE_SIZE), F32),
                pltpu.VMEM((n_pages, PAGE_SIZE), F32),
                pltpu.VMEM((n_pages, PAGE_SIZE), F32),
                pltpu.SemaphoreType.DMA((1,)),
            ],
        ),
        out_shape=jax.ShapeDtypeStruct((bs, topk, PAGE_SIZE), I32),
        compiler_params=_cp(("arbitrary",), 40),
        name="select_sample",
    )(page_table, ya.reshape(bs, 1, -1), yg.reshape(bs, 1, -1), pool_idx)
    return out[:, :, 0]


def _sattn_body(pos_ref, pt_ref, ya_ref, posv_ref, rb_ref, kpool, vpool, o_ref, kbuf, vbuf, sem,
                *, layer, past, topk, thresholds):
    b = pl.program_id(0)

    def row_copies(r):
        pos = jnp.maximum(pos_ref[b, r], 0)
        page = pt_ref[b, pos // PAGE_SIZE]
        off = pos % PAGE_SIZE
        return (pltpu.make_async_copy(kpool.at[layer, page, off], kbuf.at[r], sem.at[0]),
                pltpu.make_async_copy(vpool.at[layer, page, off], vbuf.at[r], sem.at[1]))

    def start(r, new_slot):
        ck, cv = row_copies(r)
        ck.start()
        cv.start()
        return jnp.where(pos_ref[b, r] < 0, r, new_slot)

    new_slot = lax.fori_loop(0, topk, start, jnp.int32(-1))

    y_row = ya_ref[...]
    k_new = _rows_to_sublanes(y_row, HC, DHC, start=HC * DHC)
    v_new = _rows_to_sublanes(y_row, HC, DHC, start=2 * HC * DHC)

    def wait(r, carry):
        pltpu.make_async_copy(kpool.at[layer, 0, 0], kbuf.at[r], sem.at[0]).wait()
        pltpu.make_async_copy(vpool.at[layer, 0, 0], vbuf.at[r], sem.at[1]).wait()
        return carry

    lax.fori_loop(0, topk, wait, 0, unroll=8)

    @pl.when(new_slot >= 0)
    def _():
        kbuf[new_slot] = k_new
        vbuf[new_slot] = v_new

    posv = posv_ref[...]
    dist = jnp.where(posv < 0, 0, past - posv)
    bucket = jnp.zeros(dist.shape, I32)
    for thr in thresholds:
        bucket = bucket + jnp.where(dist >= thr, 1, 0)
    for h in range(HC):
        hs = slice(h * DHC, (h + 1) * DHC)
        bias = jnp.zeros(dist.shape, F32)
        for bkt in range(N_BUCKETS):
            bias = jnp.where(bucket == bkt, rb_ref[bkt, h], bias)
        q16 = jnp.broadcast_to(y_row[:, hs], (BF16_ROWS, DHC)).astype(BF16)
        lg = _dot_nt(q16, kbuf[:, h, :].astype(BF16)) * (DHC ** -0.5) + bias
        p = jnp.exp(lg - jnp.max(lg, axis=1, keepdims=True))
        p = p / jnp.sum(p, axis=1, keepdims=True)
        out = _dot(p.astype(BF16), vbuf[:, h, :].astype(BF16))
        o_ref[:, hs] = out[0:1, :]


def _attn_sample(ya, pos, rel_bias, pool_k, pool_v, page_table, layer, topk):
    bs = ya.shape[0]
    past = page_table.shape[1] * PAGE_SIZE
    row = lambda width: pl.BlockSpec((None, 1, width), lambda b, p, pt: (b, 0, 0))
    out = pl.pallas_call(
        functools.partial(_sattn_body, layer=layer, past=past, topk=topk, thresholds=_bucket_thresholds()),
        grid_spec=pltpu.PrefetchScalarGridSpec(
            num_scalar_prefetch=2,
            grid=(bs,),
            in_specs=[row(ya.shape[1]), row(topk), pl.BlockSpec(memory_space=pltpu.SMEM),
                      pl.BlockSpec(memory_space=pl.ANY), pl.BlockSpec(memory_space=pl.ANY)],
            out_specs=row(HC * DHC),
            scratch_shapes=[
                pltpu.VMEM((topk, HC, DHC), F32),
                pltpu.VMEM((topk, HC, DHC), F32),
                pltpu.SemaphoreType.DMA((2,)),
            ],
        ),
        out_shape=jax.ShapeDtypeStruct((bs, 1, HC * DHC), F32),
        compiler_params=_cp(("arbitrary",), 32),
        name="attn_sample",
    )(pos, page_table, ya.reshape(bs, 1, -1), pos.reshape(bs, 1, topk), rel_bias, pool_k, pool_v)
    return out.reshape(bs, HC * DHC)


def _conv_step_body(yb_ref, st_ref, cw_ref, cb_ref, gn_ref, bn_ref, o_ref, st_out):
    y = yb_ref[...]
    u = y[:, 0:D_CONV] * _sigmoid(y[:, D_CONV:2 * D_CONV])
    nprev = CONV_W - 1
    acc = jnp.sum(st_ref[...] * cw_ref[0:nprev, :], axis=0, keepdims=True) + u * cw_ref[nprev:CONV_W, :] + cb_ref[...]
    mu = jnp.mean(acc, axis=1, keepdims=True)
    var = jnp.mean(jnp.square(acc - mu), axis=1, keepdims=True)
    yn = (acc - mu) * lax.rsqrt(var + EPS) * gn_ref[...] + bn_ref[...]
    o_ref[...] = yn * _sigmoid(yn)
    st_out[0:nprev - 1, :] = st_ref[1:nprev, :]
    st_out[nprev - 1:nprev, :] = u


def _conv_step(yb, state, cw, cb, gn, bn):
    bs = yb.shape[0]
    full = lambda shape: pl.BlockSpec(shape, lambda b: tuple(0 for _ in shape))
    st_spec = pl.BlockSpec((None, CONV_W - 1, D_CONV), lambda b: (b, 0, 0))
    row = lambda width: pl.BlockSpec((None, 1, width), lambda b: (b, 0, 0))
    out, st = pl.pallas_call(
        _conv_step_body,
        grid=(bs,),
        in_specs=[row(2 * D_CONV), st_spec, full((CONV_W, D_CONV)), full((1, D_CONV)), full((1, D_CONV)),
                  full((1, D_CONV))],
        out_specs=[row(D_CONV), st_spec],
        out_shape=[jax.ShapeDtypeStruct((bs, 1, D_CONV), F32),
                   jax.ShapeDtypeStruct((bs, CONV_W - 1, D_CONV), F32)],
        compiler_params=_cp(("parallel",), 32),
        name="conv_step",
    )(yb.reshape(bs, 1, 2 * D_CONV), state, cw, cb.reshape(1, D_CONV), gn.reshape(1, D_CONV),
      bn.reshape(1, D_CONV))
    return out.reshape(bs, D_CONV), st


def kernel(x_prompt, x_sample, state_mlstm_c, state_mlstm_n, state_mlstm_m, state_ret, cache_k, cache_v,
           cache_idx_k, state_conv, page_table, norm_ffn1, w_ffn1_in, w_ffn1_out, norm_mix, norm_ffn2,
           w_ffn2_in, w_ffn2_out, norm_final, w_ab_in, b_ab_gate, g_mlstm_norm, g_ret_norm, w_ab_out,
           w_cd_in, w_cd_out, rel_bias, conv_w, conv_b, g_conv_norm, b_conv_norm):
    batch, seq, d = x_prompt.shape
    bs = x_sample.shape[0]
    depth = norm_ffn1.shape[0]
    past = page_table.shape[1] * PAGE_SIZE
    hd = HC * DHC
    assert x_sample.shape[1] == 1 and seq % ATT_KC == 0 and d == D_MODEL
    table = _bias_table(rel_bias)
    w_ab_in_t = jnp.swapaxes(w_ab_in, 1, 2)
    w_cd_in_t = jnp.swapaxes(w_cd_in, 1, 2)
    pool_idx_t = jnp.swapaxes(cache_idx_k, 2, 3)
    in_proj = functools.partial(_matmul, w_transposed=True)

    x = x_prompt.reshape(batch * seq, d)
    xs = x_sample.reshape(bs, d)
    topk_s = min(TOPK_MAX, (past + 1) // 4)
    p_ab, s_ab, p_cd, s_cd = [], [], [], []

    def cd_state(ya, yg, conv_state, lead):
        return (ya[:, hd:2 * hd].reshape(lead + (HC, DHC)), ya[:, 2 * hd:3 * hd].reshape(lead + (HC, DHC)),
                yg[:, 0:IDX_DIM].reshape(lead + (IDX_DIM,)), conv_state)

    for l in range(depth):
        j = l // 2
        x, xs, xn, xsn = _ffn(x, xs, norm_ffn1[l], w_ffn1_in, w_ffn1_out, l, post="norm", g_post=norm_mix[l])
        if l % 2 == 0:
            ya, ya_s = in_proj([xn], [xsn], w_ab_in_t, j, 0, AB_A_W, 1024)
            yg, yg_s = in_proj([xn], [xsn], w_ab_in_t, j, AB_A_W, LANE, LANE)
            yb, yb_s = in_proj([xn], [xsn], w_ab_in_t, j, AB_B0, AB_B_W, 1024)
            gates = (b_ab_gate[j], g_mlstm_norm[j], g_ret_norm[j])
            ma, mb, c, n, mm, s = _ab_scan(ya, yg, yb, *gates, batch, seq)
            p_ab.append((c, n[:, :, 0, :], mm[:, :, 0, 0], s))
            mix_s, c, n, mm, s = _ab_step(ya_s, yg_s, yb_s, *gates, state_mlstm_c[j], state_mlstm_n[j],
                                          state_mlstm_m[j], state_ret[j], past)
            s_ab.append((c, n[:, :, 0, :], mm[:, :, 0, 0], s))
            x, xs = _matmul([ma, mb], [mix_s], w_ab_out, j, 0, d, 512, res=x, s_res=xs)
        else:
            ya, ya16, ya_s = in_proj([xn], [xsn], w_cd_in_t, j, 0, CD_A_W, 512, out_dtypes=(F32, BF16))
            yg, yg16, yg_s = in_proj([xn], [xsn], w_cd_in_t, j, CD_A_W, LANE, LANE, out_dtypes=(F32, BF16))
            yb, yb_s = in_proj([xn], [xsn], w_cd_in_t, j, CD_B0, CD_B_W, 1024)
            conv_args = (conv_w[j], conv_b[j], g_conv_norm[j], b_conv_norm[j])
            attn = _attn_prompt(ya16, yg, yg16, _transpose_v(ya16, batch, seq), table, batch, seq)
            conv, conv_state = _conv_prompt(yb, *conv_args, batch, seq)
            p_cd.append(cd_state(ya, yg, conv_state, (batch, seq)))
            pos = _select_sample(ya_s, yg_s, pool_idx_t, page_table, j, topk_s)
            attn_s = _attn_sample(ya_s, pos, rel_bias, cache_k, cache_v, page_table, j, topk_s)
            conv_s, conv_state_s = _conv_step(yb_s, state_conv[j], *conv_args)
            s_cd.append(cd_state(ya_s, yg_s, conv_state_s, (bs, 1)))
            x, xs = _matmul([attn, conv], [attn_s, conv_s], w_cd_out, j, 0, d, 512, res=x, s_res=xs)
        if l == depth - 1:
            y_p, y_s = _ffn(x, xs, norm_ffn2[l], w_ffn2_in, w_ffn2_out, l, post="final", g_post=norm_final)
        else:
            x, xs = _ffn(x, xs, norm_ffn2[l], w_ffn2_in, w_ffn2_out, l)
    stack = lambda states: [jnp.stack(a) for a in zip(*states)]
    (pc, pn, pm, ps), (sc, sn, sm, ss) = stack(p_ab), stack(s_ab)
    (pk, pv, pik, pcv), (sk, sv, sik, scv) = stack(p_cd), stack(s_cd)
    return (y_p.reshape(batch, seq, d), y_s.reshape(bs, 1, d), pc, sc, pn, sn, pm, sm, ps, ss,
            pk, sk, pv, sv, pik, sik, pcv, scv)
```
